```python
import math
import jax, jax.numpy as jnp
from jax import lax
import numpy as np

D_MODEL = 1024
BATCH = 4
SEQ = 4096
DEPTH = 4
DEC_BATCH = 32
DEC_SEQ = 32
PAST_LEN = 2048

CHUNK = 64
HEAD_DIM = 64
H_A = 8
H_B = 4
H_C = 16
D_A = H_A * HEAD_DIM
D_B = H_B * 2 * HEAD_DIM
D_C = H_C * HEAD_DIM
IN_AB = 3 * D_A + 3 * D_B
IN_C = 3 * D_C + H_C
A_LEFT_CHUNKS = 8
A_PAST = A_LEFT_CHUNKS * CHUNK
A_BAND = A_PAST + CHUNK
REL_CLIP = 128
D_FF = 2816
N_SUB = 3
MACARON_W = 0.5
Q_BLOCK = 128
EPS = 1e-6
NEG_INF = -1e30
FORGET_BIAS_INIT = 3.0
ATTN_SCALE = HEAD_DIM ** -0.5

kernel_name = 'hybrid_chunk_stream_encoder_step'


def _rmsnorm(x, g):
    x32 = x.astype(jnp.float32)
    y = x32 * lax.rsqrt(jnp.mean(x32 * x32, axis=-1, keepdims=True) + EPS)
    return (y * g.astype(jnp.float32)).astype(x.dtype)


def _swiglu(h, wg, wu, wd):
    return (jax.nn.silu(h @ wg) * (h @ wu)) @ wd


def _alibi_slopes(n):
    return jnp.asarray([2.0 ** (-8.0 * (i + 1) / n) for i in range(n)], jnp.float32)


def _lambda_init(li):
    return 0.8 - 0.6 * math.exp(-0.3 * li)


def _ada(c, w, b):
    mod = jax.nn.silu(c) @ w + b
    return mod.reshape(c.shape[0], N_SUB, 3, c.shape[-1])


def _sub_in(x, mod, i, g_pre):
    return _rmsnorm(x, g_pre) * (1.0 + mod[:, i, 1][:, None]) + mod[:, i, 0][:, None]


def _sub_out(x, mod, i, g_post, out, res_w):
    return x + res_w * (1.0 + mod[:, i, 2][:, None]) * _rmsnorm(out, g_post)


def _blocks(a):
    b, s = a.shape[:2]
    return jnp.moveaxis(a.reshape((b, s // Q_BLOCK, Q_BLOCK) + a.shape[2:]), 1, 0)


def _unblocks(a):
    a = jnp.moveaxis(a, 0, 1)
    return a.reshape((a.shape[0], a.shape[1] * a.shape[2]) + a.shape[3:])


def _band(a):
    b, s = a.shape[:2]
    nc = s // CHUNK
    ap = jnp.pad(a, ((0, 0), (A_PAST, 0), (0, 0), (0, 0)))
    ac = ap.reshape((b, nc + A_LEFT_CHUNKS, CHUNK) + a.shape[2:])
    return jnp.concatenate([ac[:, j:j + nc] for j in range(A_LEFT_CHUNKS + 1)], axis=2)


def _chunk_relpos_core(q, k, v, q_pos, k_pos, valid, table):
    rel = jnp.clip(q_pos[:, None] - k_pos[None, :], -REL_CLIP, REL_CLIP) + REL_CLIP
    bias = table.astype(jnp.float32)[:, rel]
    s = jnp.einsum('bcqhd,bckhd->bchqk', q, k).astype(jnp.float32) * ATTN_SCALE + bias[None, None]
    s = jnp.where(valid[None, :, None, None, :], s, NEG_INF)
    p = jax.nn.softmax(s, axis=-1).astype(v.dtype)
    return jnp.einsum('bchqk,bckhd->bcqhd', p, v)


def _diff_core(q, k, v, q_pos, k_pos, lam, slopes):
    s = jnp.einsum('bqhid,bkhid->bhiqk', q, k).astype(jnp.float32) * ATTN_SCALE
    dist = jnp.abs(q_pos[:, None] - k_pos[None, :]).astype(jnp.float32)
    s = s - slopes[None, :, None, None, None] * dist[None, None, None]
    vis = (k_pos[None, :] // CHUNK) <= (q_pos[:, None] // CHUNK)
    s = jnp.where(vis, s, NEG_INF)
    p = jax.nn.softmax(s, axis=-1)
    attn = (p[:, :, 0] - lam * p[:, :, 1]).astype(v.dtype)
    return jnp.einsum('bhqk,bkhd->bqhd', attn, v)


def _forget_core(q, k, v, dq, dk, q_pos, k_pos):
    s = jnp.einsum('bqhd,bkhd->bhqk', q, k).astype(jnp.float32) * ATTN_SCALE
    s = s + jnp.swapaxes(dq, 1, 2)[..., :, None] - jnp.swapaxes(dk, 1, 2)[..., None, :]
    s = jnp.where(k_pos[None, :] <= q_pos[:, None], s, NEG_INF)
    p = jax.nn.softmax(s, axis=-1).astype(v.dtype)
    return jnp.einsum('bhqk,bkhd->bqhd', p, v)


def _mixer_ab(h, w_in, w_out, relpos, lam_vec, subln, lam_init, cache):
    bsz, t = h.shape[:2]
    proj = h @ w_in
    qa, ka, va, qb, kb, vb = jnp.split(
        proj, [D_A, 2 * D_A, 3 * D_A, 3 * D_A + D_B, 3 * D_A + 2 * D_B], axis=-1)
    qa = qa.reshape(bsz, t, H_A, HEAD_DIM)
    ka = ka.reshape(bsz, t, H_A, HEAD_DIM)
    va = va.reshape(bsz, t, H_A, HEAD_DIM)
    qb = qb.reshape(bsz, t, H_B, 2, HEAD_DIM)
    kb = kb.reshape(bsz, t, H_B, 2 * HEAD_DIM)
    vb = vb.reshape(bsz, t, H_B, 2 * HEAD_DIM)
    lv = lam_vec.astype(jnp.float32)
    lam = jnp.exp(jnp.sum(lv[0] * lv[1])) - jnp.exp(jnp.sum(lv[2] * lv[3])) + lam_init
    slopes = _alibi_slopes(H_B)
    if cache is None:
        nc = t // CHUNK
        valid = (jnp.arange(nc)[:, None] * CHUNK - A_PAST + jnp.arange(A_BAND)[None, :]) >= 0
        out_a = _chunk_relpos_core(qa.reshape(bsz, nc, CHUNK, H_A, HEAD_DIM), _band(ka), _band(va),
                                   A_PAST + jnp.arange(CHUNK), jnp.arange(A_BAND), valid, relpos)
        pos = jnp.arange(t)
        kb4 = kb.reshape(bsz, t, H_B, 2, HEAD_DIM)
        out_b = _unblocks(lax.map(lambda xs: _diff_core(xs[0], kb4, vb, xs[1], pos, lam, slopes),
                                  (_blocks(qb), pos.reshape(-1, Q_BLOCK))))
        w = min(A_PAST, t)
        new = (ka[:, t - w:], va[:, t - w:], kb, vb)
    else:
        cak, cav, cbk, cbv = cache
        past = cbk.shape[1]
        w = cak.shape[1]
        q_pos = past + jnp.arange(t)
        k_pos_a = jnp.concatenate([past - w + jnp.arange(w), q_pos])
        ka_all = jnp.concatenate([cak, ka], axis=1)[:, None]
        va_all = jnp.concatenate([cav, va], axis=1)[:, None]
        out_a = _chunk_relpos_core(qa[:, None], ka_all, va_all, q_pos, k_pos_a,
                                   jnp.ones((1, w + t), dtype=bool), relpos)
        kb_all = jnp.concatenate([cbk, kb], axis=1).reshape(bsz, past + t, H_B, 2, HEAD_DIM)
        vb_all = jnp.concatenate([cbv, vb], axis=1)
        out_b = _diff_core(qb, kb_all, vb_all, q_pos, jnp.arange(past + t), lam, slopes)
        new = (ka, va, kb, vb)
    out_a = out_a.reshape(bsz, t, D_A)
    out_b = (_rmsnorm(out_b, subln) * (1.0 - lam_init)).reshape(bsz, t, D_B)
    return jnp.concatenate([out_a, out_b], axis=-1) @ w_out, new


def _mixer_c(h, w_in, b_f, w_out, cache):
    bsz, t = h.shape[:2]
    proj = h @ w_in
    q, k, v, fl = jnp.split(proj, [D_C, 2 * D_C, 3 * D_C], axis=-1)
    q = q.reshape(bsz, t, H_C, HEAD_DIM)
    k = k.reshape(bsz, t, H_C, HEAD_DIM)
    v = v.reshape(bsz, t, H_C, HEAD_DIM)
    logf = jax.nn.log_sigmoid(fl.astype(jnp.float32) + b_f.astype(jnp.float32))
    if cache is None:
        dcum = jnp.cumsum(logf, axis=1)
        pos = jnp.arange(t)
        out = _unblocks(lax.map(lambda xs: _forget_core(xs[0], k, v, xs[1], dcum, xs[2], pos),
                                (_blocks(q), _blocks(dcum), pos.reshape(-1, Q_BLOCK))))
    else:
        ck, cv, clf = cache
        past = ck.shape[1]
        dpast = jnp.cumsum(clf.astype(jnp.float32), axis=1)
        dpast = dpast - dpast[:, -1:]
        dq = jnp.cumsum(logf, axis=1)
        out = _forget_core(q, jnp.concatenate([ck, k], axis=1), jnp.concatenate([cv, v], axis=1),
                           dq, jnp.concatenate([dpast, dq], axis=1),
                           past + jnp.arange(t), jnp.arange(past + t))
    return out.reshape(bsz, t, D_C) @ w_out, (k, v, logf)


def setup_inputs(seed: int = 0) -> dict:
    key = jax.random.key(seed)
    ks = jax.random.split(key, 32)
    n_even = (DEPTH + 1) // 2
    n_odd = DEPTH // 2
    a_win = min(A_PAST, PAST_LEN)

    def nrm(k, shape, s):
        return jax.random.normal(k, shape, jnp.float32) * s

    return {
        'x_prompt': nrm(ks[0], (BATCH, SEQ, D_MODEL), 1.0),
        'x_sample': nrm(ks[1], (DEC_BATCH, DEC_SEQ, D_MODEL), 1.0),
        'c_prompt': nrm(ks[2], (BATCH, D_MODEL), 1.0),
        'c_sample': nrm(ks[3], (DEC_BATCH, D_MODEL), 1.0),
        'cache_a_k': nrm(ks[4], (n_even, DEC_BATCH, a_win, H_A, HEAD_DIM), 1.0),
        'cache_a_v': nrm(ks[5], (n_even, DEC_BATCH, a_win, H_A, HEAD_DIM), 1.0),
        'cache_b_k': nrm(ks[6], (n_even, DEC_BATCH, PAST_LEN, H_B, 2 * HEAD_DIM), 1.0),
        'cache_b_v': nrm(ks[7], (n_even, DEC_BATCH, PAST_LEN, H_B, 2 * HEAD_DIM), 1.0),
        'cache_c_k': nrm(ks[8], (n_odd, DEC_BATCH, PAST_LEN, H_C, HEAD_DIM), 1.0),
        'cache_c_v': nrm(ks[9], (n_odd, DEC_BATCH, PAST_LEN, H_C, HEAD_DIM), 1.0),
        'cache_c_logf': jax.nn.log_sigmoid(FORGET_BIAS_INIT + nrm(ks[10], (n_odd, DEC_BATCH, PAST_LEN, H_C), 1.0)),
        'w_ada': nrm(ks[11], (DEPTH, D_MODEL, 3 * N_SUB * D_MODEL), 0.1 * D_MODEL ** -0.5),
        'b_ada': nrm(ks[12], (DEPTH, 3 * N_SUB * D_MODEL), 0.02),
        'norm_pre': 1.0 + nrm(ks[13], (DEPTH, N_SUB, D_MODEL), 0.05),
        'norm_post': 1.0 + nrm(ks[14], (DEPTH, N_SUB, D_MODEL), 0.05),
        'ffn_w_gate': nrm(ks[15], (DEPTH, 2, D_MODEL, D_FF), D_MODEL ** -0.5),
        'ffn_w_up': nrm(ks[16], (DEPTH, 2, D_MODEL, D_FF), D_MODEL ** -0.5),
        'ffn_w_down': nrm(ks[17], (DEPTH, 2, D_FF, D_MODEL), D_FF ** -0.5),
        'w_in_ab': nrm(ks[18], (n_even, D_MODEL, IN_AB), D_MODEL ** -0.5),
        'w_out_ab': nrm(ks[19], (n_even, D_A + D_B, D_MODEL), (D_A + D_B) ** -0.5),
        'relpos_a': nrm(ks[20], (n_even, H_A, 2 * REL_CLIP + 1), 0.2),
        'lambda_b': nrm(ks[21], (n_even, 4, HEAD_DIM), 0.1),
        'subln_b': 1.0 + nrm(ks[22], (n_even, 2 * HEAD_DIM), 0.05),
        'w_in_c': nrm(ks[23], (n_odd, D_MODEL, IN_C), D_MODEL ** -0.5),
        'b_f': FORGET_BIAS_INIT + nrm(ks[24], (n_odd, H_C), 0.5),
        'w_out_c': nrm(ks[25], (n_odd, D_C, D_MODEL), D_C ** -0.5),
    }


def reference(x_prompt, x_sample, c_prompt, c_sample,
              cache_a_k, cache_a_v, cache_b_k, cache_b_v, cache_c_k, cache_c_v, cache_c_logf,
              w_ada, b_ada, norm_pre, norm_post, ffn_w_gate, ffn_w_up, ffn_w_down,
              w_in_ab, w_out_ab, relpos_a, lambda_b, subln_b, w_in_c, b_f, w_out_c):

    def run(x, c, sample):
        a_k, a_v, b_k, b_v, c_k, c_v, c_lf = [], [], [], [], [], [], []
        for li in range(DEPTH):
            mod = _ada(c, w_ada[li], b_ada[li])
            h = _sub_in(x, mod, 0, norm_pre[li, 0])
            x = _sub_out(x, mod, 0, norm_post[li, 0],
                         _swiglu(h, ffn_w_gate[li, 0], ffn_w_up[li, 0], ffn_w_down[li, 0]), MACARON_W)
            h = _sub_in(x, mod, 1, norm_pre[li, 1])
            j = li // 2
            if li % 2 == 0:
                cache = (cache_a_k[j], cache_a_v[j], cache_b_k[j], cache_b_v[j]) if sample else None
                out, new = _mixer_ab(h, w_in_ab[j], w_out_ab[j], relpos_a[j], lambda_b[j], subln_b[j],
                                     _lambda_init(li), cache)
                a_k.append(new[0]); a_v.append(new[1]); b_k.append(new[2]); b_v.append(new[3])
            else:
                cache = (cache_c_k[j], cache_c_v[j], cache_c_logf[j]) if sample else None
                out, new = _mixer_c(h, w_in_c[j], b_f[j], w_out_c[j], cache)
                c_k.append(new[0]); c_v.append(new[1]); c_lf.append(new[2])
            x = _sub_out(x, mod, 1, norm_post[li, 1], out, 1.0)
            h = _sub_in(x, mod, 2, norm_pre[li, 2])
            x = _sub_out(x, mod, 2, norm_post[li, 2],
                         _swiglu(h, ffn_w_gate[li, 1], ffn_w_up[li, 1], ffn_w_down[li, 1]), MACARON_W)
        return (x, jnp.stack(a_k), jnp.stack(a_v), jnp.stack(b_k), jnp.stack(b_v),
                jnp.stack(c_k), jnp.stack(c_v), jnp.stack(c_lf))

    y_prompt, p_a_k, p_a_v, p_b_k, p_b_v, p_c_k, p_c_v, p_c_logf = run(x_prompt, c_prompt, False)
    y_sample, s_a_k, s_a_v, s_b_k, s_b_v, s_c_k, s_c_v, s_c_logf = run(x_sample, c_sample, True)
    return (y_prompt, y_sample,
            p_a_k, p_a_v, p_b_k, p_b_v, p_c_k, p_c_v, p_c_logf,
            s_a_k, s_a_v, s_b_k, s_b_v, s_c_k, s_c_v, s_c_logf)
```

```python
import functools
import math

import jax
import jax.numpy as jnp
from jax import lax
from jax.experimental import pallas as pl
from jax.experimental.pallas import tpu as pltpu

D_MODEL = 1024
DEPTH = 4
CHUNK = 64
HEAD_DIM = 64
H_A = 8
H_B = 4
H_C = 16
D_A = H_A * HEAD_DIM
D_B = H_B * 2 * HEAD_DIM
D_C = H_C * HEAD_DIM
A_LEFT_CHUNKS = 8
A_PAST = A_LEFT_CHUNKS * CHUNK
REL_CLIP = 128
D_FF = 2816
N_SUB = 3
MACARON_W = 0.5
EPS = 1e-6
NEG_INF = -1e30
ATTN_SCALE = HEAD_DIM ** -0.5

LANES = 128
TOK_TILE = 256
SAMPLE_GROUP = 8
TQ_A = 256
BAND_A = 3 * TQ_A
TQ = 512
TAB_PAD = 384
TOEP = 1024
SBIAS_W = 640
ADA_TN = 1536
VMEM_LIMIT = 56 * 1024 * 1024

BF16 = jnp.bfloat16
F32 = jnp.float32


def _lam_init(li):
    return 0.8 - 0.6 * math.exp(-0.3 * li)


def _cparams(n_axes, vmem=None):
    return pltpu.CompilerParams(dimension_semantics=("arbitrary",) * n_axes,
                                vmem_limit_bytes=vmem)


def _dot(a, b):
    return jnp.dot(a, b, preferred_element_type=F32)


def _dot_nt(a, b):
    return lax.dot_general(a, b, (((1,), (1,)), ((), ())), preferred_element_type=F32)


def _dot_exact(a, b):
    return jnp.dot(a, b, preferred_element_type=F32, precision=lax.Precision.HIGHEST)


def _dot_nt_exact(a, b):
    return lax.dot_general(a, b, (((1,), (1,)), ((), ())), preferred_element_type=F32,
                           precision=lax.Precision.HIGHEST)


def _norm_mod(x, m, g_pre):
    y = x * lax.rsqrt(jnp.mean(x * x, axis=-1, keepdims=True) + EPS)
    return (y * g_pre) * (1.0 + m[:, 1:2, :]) + m[:, 0:1, :]


def _gated_residual(x, m, g_post, out, res_w):
    y = out * lax.rsqrt(jnp.mean(out * out, axis=-1, keepdims=True) + EPS)
    return x + (res_w * (1.0 + m[:, 2:3, :])) * (y * g_post)


def _tok_layout(x):
    nb_total, sb_total, _ = x.shape
    if sb_total >= TOK_TILE:
        nb, sb = 1, TOK_TILE
    else:
        nb, sb = SAMPLE_GROUP, sb_total
    tiles = sb_total // sb
    grid = (nb_total // nb) * tiles

    def spec(width):
        return pl.BlockSpec((nb, sb, width), lambda i: (i // tiles, i % tiles, 0))

    mod_spec = pl.BlockSpec((nb, 3, D_MODEL), lambda i: (i // tiles, 0, 0))
    return nb, sb, grid, spec, mod_spec


def _full(shape):
    return pl.BlockSpec(shape, lambda *_: (0,) * len(shape))


def _ada_kernel(c_ref, w_ref, b_ref, o_ref):
    c = c_ref[...]
    a = (c * jax.nn.sigmoid(c)).astype(BF16)
    o_ref[0] = _dot(a, w_ref[0].astype(BF16)) + b_ref[0]


def _ada(c_all, w_ada, b_ada):
    rows = c_all.shape[0]
    n_out = w_ada.shape[-1]
    return pl.pallas_call(
        _ada_kernel,
        grid=(DEPTH, n_out // ADA_TN),
        in_specs=[pl.BlockSpec((rows, D_MODEL), lambda l, n: (0, 0)),
                  pl.BlockSpec((1, D_MODEL, ADA_TN), lambda l, n: (l, 0, n)),
                  pl.BlockSpec((1, 1, ADA_TN), lambda l, n: (l, 0, n))],
        out_specs=pl.BlockSpec((1, rows, ADA_TN), lambda l, n: (l, 0, n)),
        out_shape=jax.ShapeDtypeStruct((DEPTH, rows, n_out), F32),
        compiler_params=_cparams(2, VMEM_LIMIT),
        name="ada",
    )(c_all, w_ada, b_ada.reshape(DEPTH, 1, n_out))


def _ffn_kernel(x_ref, mod_ref, gpre_ref, gpost_ref, wg_ref, wu_ref, wd_ref, o_ref):
    x = x_ref[...]
    m = mod_ref[...]
    nb, sb, d = x.shape
    h = _norm_mod(x, m, gpre_ref[...]).reshape(nb * sb, d).astype(BF16)
    g = _dot(h, wg_ref[...])
    u = _dot(h, wu_ref[...])
    a = ((g * jax.nn.sigmoid(g)) * u).astype(BF16)
    out = _dot(a, wd_ref[...]).reshape(nb, sb, d)
    o_ref[...] = _gated_residual(x, m, gpost_ref[...], out, MACARON_W)


def _ffn(x, mod, g_pre, g_post, wg, wu, wd):
    nb, sb, grid, spec, mod_spec = _tok_layout(x)
    return pl.pallas_call(
        _ffn_kernel,
        grid=(grid,),
        in_specs=[spec(D_MODEL), mod_spec, _full((1, D_MODEL)), _full((1, D_MODEL)),
                  _full((D_MODEL, D_FF)), _full((D_MODEL, D_FF)), _full((D_FF, D_MODEL))],
        out_specs=spec(D_MODEL),
        out_shape=jax.ShapeDtypeStruct(x.shape, F32),
        compiler_params=_cparams(1, VMEM_LIMIT),
        name="ffn",
    )(x, mod, g_pre, g_post, wg, wu, wd)


def _inproj_ab_kernel(x_ref, mod_ref, gpre_ref, w_ref, qkv_ref, ka_ref, va_ref, kb_ref, vb_ref):
    x = x_ref[...]
    nb, sb, d = x.shape
    h = _norm_mod(x, mod_ref[...], gpre_ref[...]).reshape(nb * sb, d).astype(BF16)
    f32_outs = {1: ka_ref, 2: va_ref, 4: kb_ref, 5: vb_ref}
    for c in range(6):
        cols = slice(c * D_A, (c + 1) * D_A)
        p = _dot(h, w_ref[:, cols]).reshape(nb, sb, D_A)
        if c in f32_outs:
            f32_outs[c][...] = p
            qkv_ref[:, :, cols] = p.astype(BF16)
        else:
            qkv_ref[:, :, cols] = (p * ATTN_SCALE).astype(BF16)


def _inproj_ab(x, mod, g_pre, w_in):
    nb, sb, grid, spec, mod_spec = _tok_layout(x)
    nbt, sbt, _ = x.shape
    f32_out = jax.ShapeDtypeStruct((nbt, sbt, D_A), F32)
    return pl.pallas_call(
        _inproj_ab_kernel,
        grid=(grid,),
        in_specs=[spec(D_MODEL), mod_spec, _full((1, D_MODEL)), _full((D_MODEL, 6 * D_A))],
        out_specs=[spec(6 * D_A)] + [spec(D_A)] * 4,
        out_shape=[jax.ShapeDtypeStruct((nbt, sbt, 6 * D_A), BF16)] + [f32_out] * 4,
        compiler_params=_cparams(1, VMEM_LIMIT),
        name="inproj_ab",
    )(x, mod, g_pre, w_in)


def _inproj_c_kernel(x_ref, mod_ref, gpre_ref, w_ref, wf_ref, bf_ref,
                     qkv_ref, k_ref, v_ref, logf_ref):
    x = x_ref[...]
    nb, sb, d = x.shape
    h = _norm_mod(x, mod_ref[...], gpre_ref[...]).reshape(nb * sb, d).astype(BF16)
    f32_outs = {1: k_ref, 2: v_ref}
    for c in range(3):
        cols = slice(c * D_C, (c + 1) * D_C)
        p = _dot(h, w_ref[:, cols]).reshape(nb, sb, D_C)
        if c in f32_outs:
            f32_outs[c][...] = p
            qkv_ref[:, :, cols] = p.astype(BF16)
        else:
            qkv_ref[:, :, cols] = (p * ATTN_SCALE).astype(BF16)
    z = _dot(h, wf_ref[...]) + bf_ref[...]
    logf = jnp.minimum(z, 0.0) - jnp.log1p(jnp.exp(-jnp.abs(z)))
    logf_ref[...] = logf.reshape(nb, sb, LANES)


def _inproj_c(x, mod, g_pre, w_qkv, w_f, b_f):
    nb, sb, grid, spec, mod_spec = _tok_layout(x)
    nbt, sbt, _ = x.shape
    f32_out = jax.ShapeDtypeStruct((nbt, sbt, D_C), F32)
    return pl.pallas_call(
        _inproj_c_kernel,
        grid=(grid,),
        in_specs=[spec(D_MODEL), mod_spec, _full((1, D_MODEL)), _full((D_MODEL, 3 * D_C)),
                  _full((D_MODEL, LANES)), _full((1, LANES))],
        out_specs=[spec(3 * D_C), spec(D_C), spec(D_C), spec(LANES)],
        out_shape=[jax.ShapeDtypeStruct((nbt, sbt, 3 * D_C), BF16), f32_out, f32_out,
                   jax.ShapeDtypeStruct((nbt, sbt, LANES), F32)],
        compiler_params=_cparams(1, VMEM_LIMIT),
        name="inproj_c",
    )(x, mod, g_pre, w_qkv, w_f, b_f)


def _outproj_kernel(*refs, n_in):
    x_ref, mod_ref, gpost_ref = refs[:3]
    a_refs = refs[3:3 + n_in]
    w_refs = refs[3 + n_in:3 + 2 * n_in]
    o_ref = refs[-1]
    x = x_ref[...]
    nb, sb, d = x.shape
    out = None
    for a_ref, w_ref in zip(a_refs, w_refs):
        a = a_ref[...]
        t = _dot(a.reshape(nb * sb, a.shape[-1]), w_ref[...])
        out = t if out is None else out + t
    o_ref[...] = _gated_residual(x, mod_ref[...], gpost_ref[...], out.reshape(nb, sb, d), 1.0)


def _outproj(x, mod, g_post, acts, weights):
    nb, sb, grid, spec, mod_spec = _tok_layout(x)
    n_in = len(acts)
    return pl.pallas_call(
        functools.partial(_outproj_kernel, n_in=n_in),
        grid=(grid,),
        in_specs=([spec(D_MODEL), mod_spec, _full((1, D_MODEL))]
                  + [spec(a.shape[-1]) for a in acts]
                  + [_full(w.shape) for w in weights]),
        out_specs=spec(D_MODEL),
        out_shape=jax.ShapeDtypeStruct(x.shape, F32),
        compiler_params=_cparams(1, VMEM_LIMIT),
        name="outproj",
    )(x, mod, g_post, *acts, *weights)


def _split_maps(q):
    lane = lax.broadcasted_iota(jnp.int32, q.shape, q.ndim - 1)
    zero = jnp.zeros_like(q)
    return jnp.where(lane < HEAD_DIM, q, zero), jnp.where(lane >= HEAD_DIM, q, zero)


def _softmax_init(m_ref, l_ref, acc_ref):
    m_ref[...] = jnp.full(m_ref.shape, NEG_INF, F32)
    l_ref[...] = jnp.zeros(l_ref.shape, F32)
    acc_ref[...] = jnp.zeros(acc_ref.shape, F32)


def _softmax_update(m_ref, l_ref, acc_ref, mi, s, v):
    m_prev = m_ref[mi]
    m_new = jnp.maximum(m_prev, jnp.max(s, axis=-1, keepdims=True))
    alpha = jnp.exp(m_prev - m_new)
    p = jnp.exp(s - m_new)
    l_ref[mi] = alpha * l_ref[mi] + jnp.sum(p, axis=-1, keepdims=True)
    acc_ref[mi] = alpha * acc_ref[mi] + _dot(p.astype(BF16), v)
    m_ref[mi] = m_new


def _softmax_result(l_ref, acc_ref, mi):
    return acc_ref[mi] * (1.0 / l_ref[mi])


def _pair_heads(o0, o1):
    lane = lax.broadcasted_iota(jnp.int32, o0.shape, 1)
    return jnp.where(lane < HEAD_DIM, o0, o1)


def _diff_lambda(lam_ref, lam_init):
    lv = lam_ref[...]
    a = jnp.sum(lv[0:1] * lv[1:2], axis=1, keepdims=True)
    b = jnp.sum(lv[2:3] * lv[3:4], axis=1, keepdims=True)
    return jnp.exp(a) - jnp.exp(b) + lam_init


def _diff_out(o0, o1, lam, subln, lam_init):
    o = o0 - lam * o1
    y = o * lax.rsqrt(jnp.mean(o * o, axis=-1, keepdims=True) + EPS)
    return (y * subln) * (1.0 - lam_init)


def _alibi_slope(g):
    slope = jnp.float32(0.0)
    for h in range(H_B):
        slope = jnp.where(g == h, jnp.float32(2.0 ** (-8.0 * (h + 1) / H_B)), slope)
    return slope


def _select_lane(a, idx):
    lane = lax.broadcasted_iota(jnp.int32, a.shape, 1)
    return jnp.sum(jnp.where(lane == idx, a, 0.0), axis=1, keepdims=True)


def _select_row(a, idx):
    row = lax.broadcasted_iota(jnp.int32, a.shape, 0)
    return jnp.sum(jnp.where(row == idx, a, 0.0), axis=0, keepdims=True)


def _relpos_bias_kernel(tab_ref, pbias_ref, sbias_ref):
    t = tab_ref[0]
    t_hi = t.astype(BF16)
    r1 = t - t_hi.astype(F32)
    t_mid = r1.astype(BF16)
    t_lo = (r1 - t_mid.astype(F32)).astype(BF16)
    ent = lax.broadcasted_iota(jnp.int32, (TAB_PAD, TOEP), 0)
    n = lax.broadcasted_iota(jnp.int32, (TAB_PAD, TOEP), 1)
    rel = jnp.clip(A_PAST + TQ_A - 1 - n, -REL_CLIP, REL_CLIP) + REL_CLIP
    onehot = jnp.where(ent == rel, 1.0, 0.0).astype(BF16)
    gen = _dot(t_hi, onehot) + _dot(t_mid, onehot) + _dot(t_lo, onehot)
    d = (lax.broadcasted_iota(jnp.int32, (TQ_A, BAND_A), 1) // CHUNK
         - lax.broadcasted_iota(jnp.int32, (TQ_A, BAND_A), 0) // CHUNK)
    valid = (d >= 0) & (d <= A_LEFT_CHUNKS)
    t_sample = sbias_ref.shape[2]
    for h in range(H_A):
        rows = jnp.broadcast_to(gen[h:h + 1, :], (TQ_A, TOEP))
        toep = pltpu.roll(rows, TOEP - TQ_A + 1, 1, stride=1, stride_axis=0)
        pbias_ref[0, h] = jnp.where(valid, toep[:, :BAND_A], NEG_INF)
        sbias_ref[0, h] = toep[:t_sample, :SBIAS_W]


def _relpos_bias(relpos_a, sample_len):
    ne = relpos_a.shape[0]
    tab = jnp.pad(relpos_a, ((0, 0), (0, 0), (0, TAB_PAD - relpos_a.shape[-1])))
    return pl.pallas_call(
        _relpos_bias_kernel,
        grid=(ne,),
        in_specs=[pl.BlockSpec((1, H_A, TAB_PAD), lambda l: (l, 0, 0))],
        out_specs=[pl.BlockSpec((1, H_A, TQ_A, BAND_A), lambda l: (l, 0, 0, 0)),
                   pl.BlockSpec((1, H_A, sample_len, SBIAS_W), lambda l: (l, 0, 0, 0))],
        out_shape=[jax.ShapeDtypeStruct((ne, H_A, TQ_A, BAND_A), F32),
                   jax.ShapeDtypeStruct((ne, H_A, sample_len, SBIAS_W), F32)],
        compiler_params=_cparams(1, VMEM_LIMIT),
        name="relpos_bias",
    )(tab)


def _attn_a_prompt_kernel(q_ref, k_ref, v_ref, bias_ref, o_ref, m_ref, l_ref, acc_ref):
    qi = pl.program_id(2)
    q0, q1 = _split_maps(q_ref[0])
    _softmax_init(m_ref, l_ref, acc_ref)
    for j in range(BAND_A // TQ_A):
        kt = qi - (BAND_A // TQ_A - 1) + j

        def tile(j=j, kt=kt):
            start = pl.multiple_of(kt * TQ_A, TQ_A)
            k = k_ref[0, pl.ds(start, TQ_A), :]
            v = v_ref[0, pl.ds(start, TQ_A), :]
            cols = slice(j * TQ_A, (j + 1) * TQ_A)
            _softmax_update(m_ref, l_ref, acc_ref, 0, _dot_nt(q0, k) + bias_ref[0, :, cols], v)
            _softmax_update(m_ref, l_ref, acc_ref, 1, _dot_nt(q1, k) + bias_ref[1, :, cols], v)

        if j == BAND_A // TQ_A - 1:
            tile()
        else:
            pl.when(kt >= 0)(tile)
    o = _pair_heads(_softmax_result(l_ref, acc_ref, 0), _softmax_result(l_ref, acc_ref, 1))
    o_ref[0] = o.astype(o_ref.dtype)


def _softmax_scratch(tq):
    return [pltpu.VMEM((2, tq, 1), F32), pltpu.VMEM((2, tq, 1), F32),
            pltpu.VMEM((2, tq, LANES), F32)]


def _attn_a_prompt(qkv, pbias):
    b, s, _ = qkv.shape
    groups = D_A // LANES
    return pl.pallas_call(
        _attn_a_prompt_kernel,
        grid=(b, groups, s // TQ_A),
        in_specs=[pl.BlockSpec((1, TQ_A, LANES), lambda b, g, i: (b, i, g)),
                  pl.BlockSpec((1, s, LANES), lambda b, g, i: (b, 0, groups + g)),
                  pl.BlockSpec((1, s, LANES), lambda b, g, i: (b, 0, 2 * groups + g)),
                  pl.BlockSpec((2, TQ_A, BAND_A), lambda b, g, i: (g, 0, 0))],
        out_specs=pl.BlockSpec((1, TQ_A, LANES), lambda b, g, i: (b, i, g)),
        out_shape=jax.ShapeDtypeStruct((b, s, D_A), BF16),
        scratch_shapes=_softmax_scratch(TQ_A),
        compiler_params=_cparams(3, VMEM_LIMIT),
        name="attn_a_prompt",
    )(qkv, qkv, qkv, pbias)


def _attn_b_prompt_kernel(q_ref, k_ref, v_ref, lam_ref, subln_ref, o_ref,
                          m_ref, l_ref, acc_ref, off_ref, diag_ref, *, lam_init):
    g = pl.program_id(1)
    qi = pl.program_id(2)
    slope = _alibi_slope(g)

    @pl.when(qi == 0)
    def _():
        r = lax.broadcasted_iota(jnp.int32, (TQ, TQ), 0)
        c = lax.broadcasted_iota(jnp.int32, (TQ, TQ), 1)
        rc = (r - c).astype(F32)
        off_ref[...] = -slope * rc
        diag_ref[...] = jnp.where(c // CHUNK <= r // CHUNK, -slope * jnp.abs(rc), NEG_INF)

    q0, q1 = _split_maps(q_ref[0])
    _softmax_init(m_ref, l_ref, acc_ref)

    def tile(j, bias):
        start = pl.multiple_of(j * TQ, TQ)
        k = k_ref[0, pl.ds(start, TQ), :]
        v = v_ref[0, pl.ds(start, TQ), :]
        _softmax_update(m_ref, l_ref, acc_ref, 0, _dot_nt(q0, k) + bias, v)
        _softmax_update(m_ref, l_ref, acc_ref, 1, _dot_nt(q1, k) + bias, v)

    def past_tile(j, carry):
        tile(j, off_ref[...] - slope * ((qi - j) * TQ).astype(F32))
        return carry

    lax.fori_loop(0, qi, past_tile, 0)
    tile(qi, diag_ref[...])
    lam = _diff_lambda(lam_ref, lam_init)
    o = _diff_out(_softmax_result(l_ref, acc_ref, 0), _softmax_result(l_ref, acc_ref, 1),
                  lam, subln_ref[...], lam_init)
    o_ref[0] = o.astype(o_ref.dtype)


def _attn_b_prompt(qkv, lam_vec, subln, lam_init):
    b, s, _ = qkv.shape
    groups = D_B // LANES
    col0 = 3 * D_A // LANES
    return pl.pallas_call(
        functools.partial(_attn_b_prompt_kernel, lam_init=lam_init),
        grid=(b, groups, s // TQ),
        in_specs=[pl.BlockSpec((1, TQ, LANES), lambda b, g, i: (b, i, col0 + g)),
                  pl.BlockSpec((1, s, LANES), lambda b, g, i: (b, 0, col0 + groups + g)),
                  pl.BlockSpec((1, s, LANES), lambda b, g, i: (b, 0, col0 + 2 * groups + g)),
                  _full(lam_vec.shape), _full(subln.shape)],
        out_specs=pl.BlockSpec((1, TQ, LANES), lambda b, g, i: (b, i, g)),
        out_shape=jax.ShapeDtypeStruct((b, s, D_B), BF16),
        scratch_shapes=_softmax_scratch(TQ) + [pltpu.VMEM((TQ, TQ), F32), pltpu.VMEM((TQ, TQ), F32)],
        compiler_params=_cparams(3, VMEM_LIMIT),
        name="attn_b_prompt",
    )(qkv, qkv, qkv, lam_vec, subln)


def _logf_scan_kernel(logf_ref, dcum_ref, dcum_t_ref):
    r = lax.broadcasted_iota(jnp.int32, (TQ, TQ), 0)
    c = lax.broadcasted_iota(jnp.int32, (TQ, TQ), 1)
    tri = jnp.where(r >= c, 1.0, 0.0).astype(F32)
    carry = jnp.zeros((1, LANES), F32)
    for t in range(logf_ref.shape[1] // TQ):
        rows = slice(t * TQ, (t + 1) * TQ)
        cs = _dot_exact(tri, logf_ref[0, rows, :]) + carry
        dcum_ref[0, rows, :] = cs
        dcum_t_ref[0, :, rows] = cs.T[:H_C, :]
        carry = cs[TQ - 1:TQ, :]


def _logf_scan(logf):
    b, s, _ = logf.shape
    return pl.pallas_call(
        _logf_scan_kernel,
        grid=(b,),
        in_specs=[pl.BlockSpec((1, s, LANES), lambda b: (b, 0, 0))],
        out_specs=[pl.BlockSpec((1, s, LANES), lambda b: (b, 0, 0)),
                   pl.BlockSpec((1, H_C, s), lambda b: (b, 0, 0))],
        out_shape=[jax.ShapeDtypeStruct((b, s, LANES), F32),
                   jax.ShapeDtypeStruct((b, H_C, s), F32)],
        compiler_params=_cparams(1, VMEM_LIMIT),
        name="logf_scan",
    )(logf)


def _attn_c_prompt_kernel(q_ref, k_ref, v_ref, dq_ref, dk_ref, o_ref, m_ref, l_ref, acc_ref):
    g = pl.program_id(1)
    qi = pl.program_id(2)
    q0, q1 = _split_maps(q_ref[0])
    dq = dq_ref[0]
    dq0 = _select_lane(dq, 2 * g)
    dq1 = _select_lane(dq, 2 * g + 1)
    _softmax_init(m_ref, l_ref, acc_ref)

    def tile(j, masked):
        start = pl.multiple_of(j * TQ, TQ)
        k = k_ref[0, pl.ds(start, TQ), :]
        v = v_ref[0, pl.ds(start, TQ), :]
        dk = dk_ref[0, 0, :, pl.ds(start, TQ)]
        s0 = _dot_nt(q0, k) + dq0 - dk[0:1, :]
        s1 = _dot_nt(q1, k) + dq1 - dk[1:2, :]
        if masked:
            r = lax.broadcasted_iota(jnp.int32, (TQ, TQ), 0)
            c = lax.broadcasted_iota(jnp.int32, (TQ, TQ), 1)
            s0 = jnp.where(c <= r, s0, NEG_INF)
            s1 = jnp.where(c <= r, s1, NEG_INF)
        _softmax_update(m_ref, l_ref, acc_ref, 0, s0, v)
        _softmax_update(m_ref, l_ref, acc_ref, 1, s1, v)

    def past_tile(j, carry):
        tile(j, False)
        return carry

    lax.fori_loop(0, qi, past_tile, 0)
    tile(qi, True)
    o = _pair_heads(_softmax_result(l_ref, acc_ref, 0), _softmax_result(l_ref, acc_ref, 1))
    o_ref[0] = o.astype(o_ref.dtype)


def _attn_c_prompt(qkv, dcum, dcum_t):
    b, s, _ = qkv.shape
    groups = D_C // LANES
    dcum_t = dcum_t.reshape(b, groups, 2, s)
    return pl.pallas_call(
        _attn_c_prompt_kernel,
        grid=(b, groups, s // TQ),
        in_specs=[pl.BlockSpec((1, TQ, LANES), lambda b, g, i: (b, i, g)),
                  pl.BlockSpec((1, s, LANES), lambda b, g, i: (b, 0, groups + g)),
                  pl.BlockSpec((1, s, LANES), lambda b, g, i: (b, 0, 2 * groups + g)),
                  pl.BlockSpec((1, TQ, LANES), lambda b, g, i: (b, i, 0)),
                  pl.BlockSpec((1, 1, 2, s), lambda b, g, i: (b, g, 0, 0))],
        out_specs=pl.BlockSpec((1, TQ, LANES), lambda b, g, i: (b, i, g)),
        out_shape=jax.ShapeDtypeStruct((b, s, D_C), BF16),
        scratch_shapes=_softmax_scratch(TQ),
        compiler_params=_cparams(3, VMEM_LIMIT),
        name="attn_c_prompt",
    )(qkv, qkv, qkv, dcum, dcum_t)


def _sample_scores(q_ref, kn_ref, kc_ref):
    q0, q1 = _split_maps(q_ref[0])
    qs = jnp.concatenate([q0, q1], axis=0)
    s_c = _dot_nt(qs, kc_ref[0].astype(BF16))
    s_n = _dot_nt(qs, kn_ref[0])
    return s_c, s_n


def _sample_softmax_pv(s_c, s_n, vn_ref, vc_ref):
    m = jnp.maximum(jnp.max(s_c, axis=-1, keepdims=True), jnp.max(s_n, axis=-1, keepdims=True))
    p_c = jnp.exp(s_c - m)
    p_n = jnp.exp(s_n - m)
    l = jnp.sum(p_c, axis=-1, keepdims=True) + jnp.sum(p_n, axis=-1, keepdims=True)
    o = _dot(p_c.astype(BF16), vc_ref[0].astype(BF16)) + _dot(p_n.astype(BF16), vn_ref[0])
    o = o * (1.0 / l)
    t = o.shape[0] // 2
    return o[:t], o[t:]


def _attn_a_sample_kernel(q_ref, kn_ref, vn_ref, kc_ref, vc_ref, bias_ref, o_ref):
    s_c, s_n = _sample_scores(q_ref, kn_ref, kc_ref)
    t = q_ref.shape[1]
    w = kc_ref.shape[1]
    bias = bias_ref[...].reshape(2 * t, SBIAS_W)
    o0, o1 = _sample_softmax_pv(s_c + bias[:, :w], s_n + bias[:, w:w + t], vn_ref, vc_ref)
    o_ref[0] = _pair_heads(o0, o1).astype(o_ref.dtype)


def _attn_b_sample_kernel(q_ref, kn_ref, vn_ref, kc_ref, vc_ref, lam_ref, subln_ref, o_ref,
                          *, lam_init):
    slope = _alibi_slope(pl.program_id(1))
    s_c, s_n = _sample_scores(q_ref, kn_ref, kc_ref)
    t = q_ref.shape[1]
    past = kc_ref.shape[1]
    qpos_c = past + lax.broadcasted_iota(jnp.int32, s_c.shape, 0) % t
    kpos_c = lax.broadcasted_iota(jnp.int32, s_c.shape, 1)
    qpos_n = lax.broadcasted_iota(jnp.int32, s_n.shape, 0) % t
    kpos_n = lax.broadcasted_iota(jnp.int32, s_n.shape, 1)
    s_c = s_c - slope * jnp.abs(qpos_c - kpos_c).astype(F32)
    s_n = s_n - slope * jnp.abs(qpos_n - kpos_n).astype(F32)
    o0, o1 = _sample_softmax_pv(s_c, s_n, vn_ref, vc_ref)
    lam = _diff_lambda(lam_ref, lam_init)
    o_ref[0] = _diff_out(o0, o1, lam, subln_ref[...], lam_init).astype(o_ref.dtype)


def _attn_c_sample_kernel(q_ref, kn_ref, vn_ref, kc_ref, vc_ref, logf_ref, clogf_ref, o_ref,
                          dpast_ref):
    g = pl.program_id(1)
    t = q_ref.shape[1]
    past = kc_ref.shape[1]

    @pl.when(g == 0)
    def _():
        r = lax.broadcasted_iota(jnp.int32, (TQ, TQ), 0)
        c = lax.broadcasted_iota(jnp.int32, (TQ, TQ), 1)
        triu = jnp.where(r <= c, 1.0, 0.0).astype(F32)
        carry = jnp.zeros((H_C, 1), F32)
        for i in range(past // TQ):
            cols = slice(i * TQ, (i + 1) * TQ)
            cs = _dot_exact(clogf_ref[0, :, cols], triu) + carry
            dpast_ref[:, cols] = cs
            carry = cs[:, TQ - 1:TQ]
        dpast_ref[...] = dpast_ref[...] - carry

    r = lax.broadcasted_iota(jnp.int32, (t, t), 0)
    c = lax.broadcasted_iota(jnp.int32, (t, t), 1)
    tri = jnp.where(r >= c, 1.0, 0.0).astype(F32)
    dq = _dot_exact(tri, logf_ref[0])
    er = lax.broadcasted_iota(jnp.int32, (LANES, LANES), 0)
    ec = lax.broadcasted_iota(jnp.int32, (LANES, LANES), 1)
    eye = jnp.where(er == ec, 1.0, 0.0).astype(F32)
    dq_t = _dot_nt_exact(eye, dq)
    dq_col = jnp.concatenate([_select_lane(dq, 2 * g), _select_lane(dq, 2 * g + 1)], axis=0)
    dpast = dpast_ref[...]
    top = lax.broadcasted_iota(jnp.int32, (2 * t, 1), 0) < t
    dk_c = jnp.where(top, _select_row(dpast, 2 * g), _select_row(dpast, 2 * g + 1))
    dk_n = jnp.where(top, _select_row(dq_t, 2 * g), _select_row(dq_t, 2 * g + 1))
    s_c, s_n = _sample_scores(q_ref, kn_ref, kc_ref)
    s_c = s_c + dq_col - dk_c
    s_n = s_n + dq_col - dk_n
    qrow = lax.broadcasted_iota(jnp.int32, s_n.shape, 0) % t
    kcol = lax.broadcasted_iota(jnp.int32, s_n.shape, 1)
    s_n = jnp.where(kcol <= qrow, s_n, NEG_INF)
    o0, o1 = _sample_softmax_pv(s_c, s_n, vn_ref, vc_ref)
    o_ref[0] = _pair_heads(o0, o1).astype(o_ref.dtype)


def _attn_sample(kernel_fn, name, qkv, cache_k, cache_v, groups, col0, extras, extra_specs,
                 scratch=()):
    b, t, _ = qkv.shape
    past = cache_k.shape[1]
    width = groups * LANES
    cache_k = cache_k.reshape(b, past, width)
    cache_v = cache_v.reshape(b, past, width)
    return pl.pallas_call(
        kernel_fn,
        grid=(b, groups),
        in_specs=[pl.BlockSpec((1, t, LANES), lambda b, g: (b, 0, col0 + g)),
                  pl.BlockSpec((1, t, LANES), lambda b, g: (b, 0, col0 + groups + g)),
                  pl.BlockSpec((1, t, LANES), lambda b, g: (b, 0, col0 + 2 * groups + g)),
                  pl.BlockSpec((1, past, LANES), lambda b, g: (b, 0, g)),
                  pl.BlockSpec((1, past, LANES), lambda b, g: (b, 0, g))] + list(extra_specs),
        out_specs=pl.BlockSpec((1, t, LANES), lambda b, g: (b, 0, g)),
        out_shape=jax.ShapeDtypeStruct((b, t, width), BF16),
        scratch_shapes=list(scratch),
        compiler_params=_cparams(2, VMEM_LIMIT),
        name=name,
    )(qkv, qkv, qkv, cache_k, cache_v, *extras)


def kernel(x_prompt, x_sample, c_prompt, c_sample, cache_a_k, cache_a_v, cache_b_k, cache_b_v, cache_c_k, cache_c_v, cache_c_logf, w_ada, b_ada, norm_pre, norm_post, ffn_w_gate, ffn_w_up, ffn_w_down, w_in_ab, w_out_ab, relpos_a, lambda_b, subln_b, w_in_c, b_f, w_out_c):
    n_prompt = x_prompt.shape[0]
    n_sample, t_sample = x_sample.shape[:2]
    s_prompt = x_prompt.shape[1]

    wg = ffn_w_gate.astype(BF16)
    wu = ffn_w_up.astype(BF16)
    wd = ffn_w_down.astype(BF16)
    w_in_ab16 = w_in_ab.astype(BF16)
    w_out_ab16 = w_out_ab.astype(BF16)
    w_qkv_c16 = w_in_c[:, :, :3 * D_C].astype(BF16)
    w_f16 = jnp.pad(w_in_c[:, :, 3 * D_C:], ((0, 0), (0, 0), (0, LANES - H_C))).astype(BF16)
    b_f_pad = jnp.pad(b_f, ((0, 0), (0, LANES - H_C)))[:, None, :]
    w_out_c16 = w_out_c.astype(BF16)
    clogf_t = jnp.swapaxes(cache_c_logf, -1, -2)

    rows = n_prompt + n_sample
    rows_pad = -(-rows // 8) * 8
    c_all = jnp.concatenate([c_prompt, c_sample, jnp.zeros((rows_pad - rows, D_MODEL), F32)], axis=0)
    mod_all = _ada(c_all, w_ada, b_ada).reshape(DEPTH, rows_pad, N_SUB, 3, D_MODEL)
    pbias, sbias = _relpos_bias(relpos_a, t_sample)

    def run(x, row0, nrows, sample):
        a_k, a_v, b_k, b_v, c_k, c_v, c_lf = [], [], [], [], [], [], []
        for li in range(DEPTH):
            mod = [mod_all[li, row0:row0 + nrows, i] for i in range(N_SUB)]
            gpre = [norm_pre[li, i][None, :] for i in range(N_SUB)]
            gpost = [norm_post[li, i][None, :] for i in range(N_SUB)]
            x = _ffn(x, mod[0], gpre[0], gpost[0], wg[li, 0], wu[li, 0], wd[li, 0])
            j = li // 2
            if li % 2 == 0:
                lam_init = _lam_init(li)
                qkv, ka, va, kb, vb = _inproj_ab(x, mod[1], gpre[1], w_in_ab16[j])
                subln = subln_b[j][None, :]
                if sample:
                    out_a = _attn_sample(
                        _attn_a_sample_kernel, "attn_a_sample", qkv, cache_a_k[j], cache_a_v[j],
                        D_A // LANES, 0, [sbias[j]],
                        [pl.BlockSpec((2, t_sample, SBIAS_W), lambda b, g: (g, 0, 0))])
                    out_b = _attn_sample(
                        functools.partial(_attn_b_sample_kernel, lam_init=lam_init), "attn_b_sample",
                        qkv, cache_b_k[j], cache_b_v[j], D_B // LANES, 3 * D_A // LANES,
                        [lambda_b[j], subln], [_full(lambda_b[j].shape), _full(subln.shape)])
                    a_k.append(ka)
                    a_v.append(va)
                else:
                    out_a = _attn_a_prompt(qkv, pbias[j])
                    out_b = _attn_b_prompt(qkv, lambda_b[j], subln, lam_init)
                    w = min(A_PAST, s_prompt)
                    a_k.append(ka[:, s_prompt - w:])
                    a_v.append(va[:, s_prompt - w:])
                b_k.append(kb)
                b_v.append(vb)
                x = _outproj(x, mod[1], gpost[1], [out_a, out_b],
                             [w_out_ab16[j, :D_A], w_out_ab16[j, D_A:]])
            else:
                qkv, k, v, logf = _inproj_c(x, mod[1], gpre[1], w_qkv_c16[j], w_f16[j], b_f_pad[j])
                if sample:
                    out = _attn_sample(
                        _attn_c_sample_kernel, "attn_c_sample", qkv, cache_c_k[j], cache_c_v[j],
                        D_C // LANES, 0, [logf, clogf_t[j]],
                        [pl.BlockSpec((1, t_sample, LANES), lambda b, g: (b, 0, 0)),
                         pl.BlockSpec((1, H_C, clogf_t.shape[-1]), lambda b, g: (b, 0, 0))],
                        scratch=[pltpu.VMEM((H_C, clogf_t.shape[-1]), F32)])
                else:
                    dcum, dcum_t = _logf_scan(logf)
                    out = _attn_c_prompt(qkv, dcum, dcum_t)
                c_k.append(k)
                c_v.append(v)
                c_lf.append(logf[:, :, :H_C])
                x = _outproj(x, mod[1], gpost[1], [out], [w_out_c16[j]])
            x = _ffn(x, mod[2], gpre[2], gpost[2], wg[li, 1], wu[li, 1], wd[li, 1])
        nb, sb = x.shape[:2]
        return (x,
                jnp.stack(a_k).reshape(-1, nb, a_k[0].shape[1], H_A, HEAD_DIM),
                jnp.stack(a_v).reshape(-1, nb, a_v[0].shape[1], H_A, HEAD_DIM),
                jnp.stack(b_k).reshape(-1, nb, sb, H_B, 2 * HEAD_DIM),
                jnp.stack(b_v).reshape(-1, nb, sb, H_B, 2 * HEAD_DIM),
                jnp.stack(c_k).reshape(-1, nb, sb, H_C, HEAD_DIM),
                jnp.stack(c_v).reshape(-1, nb, sb, H_C, HEAD_DIM),
                jnp.stack(c_lf))

    p = run(x_prompt, 0, n_prompt, False)
    s = run(x_sample, n_prompt, n_sample, True)
    return (p[0], s[0]) + tuple(p[1:]) + tuple(s[1:])
```

```python
import functools
import math

import jax
import jax.numpy as jnp
from jax import lax
from jax.experimental import pallas as pl
from jax.experimental.pallas import tpu as pltpu

D_MODEL = 1024
DEPTH = 4
CHUNK = 64
HEAD_DIM = 64
H_A = 8
H_B = 4
H_C = 16
D_A = H_A * HEAD_DIM
D_B = H_B * 2 * HEAD_DIM
D_C = H_C * HEAD_DIM
A_LEFT_CHUNKS = 8
A_PAST = A_LEFT_CHUNKS * CHUNK
REL_CLIP = 128
D_FF = 2816
N_SUB = 3
MACARON_W = 0.5
EPS = 1e-6
NEG_INF = -1e30
ATTN_SCALE = HEAD_DIM ** -0.5
LOG2E = 1.4426950408889634
QSCALE_LOG2 = ATTN_SCALE * LOG2E

LANES = 128
TOK_TILE = 256
SAMPLE_GROUP = 8
TP = 256
HP_A = 4
HP_B = 4
HP_C = 4
BAND_TILES = 3
BAND_A = BAND_TILES * TP
SCAN_T = 512
TAB_PAD = 384
TOEP = 1024
SBIAS_W = 640
ADA_TN = 1536
VMEM_LIMIT = 56 * 1024 * 1024

BF16 = jnp.bfloat16
F32 = jnp.float32


def _lam_init(li):
    return 0.8 - 0.6 * math.exp(-0.3 * li)


def _cparams(n_axes, vmem=None):
    return pltpu.CompilerParams(dimension_semantics=("arbitrary",) * n_axes,
                                vmem_limit_bytes=vmem)


def _dot(a, b):
    return jnp.dot(a, b, preferred_element_type=F32)


def _dot_nt(a, b):
    return lax.dot_general(a, b, (((1,), (1,)), ((), ())), preferred_element_type=F32)


def _dot_exact(a, b):
    return jnp.dot(a, b, preferred_element_type=F32, precision=lax.Precision.HIGHEST)


def _dot_nt_exact(a, b):
    return lax.dot_general(a, b, (((1,), (1,)), ((), ())), preferred_element_type=F32,
                           precision=lax.Precision.HIGHEST)


def _split3(x):
    hi = x.astype(BF16).astype(F32)
    r1 = x - hi
    mid = r1.astype(BF16).astype(F32)
    lo = (r1 - mid).astype(BF16).astype(F32)
    return hi, mid, lo


def _norm_mod(x, m, g_pre):
    y = x * lax.rsqrt(jnp.mean(x * x, axis=-1, keepdims=True) + EPS)
    return (y * g_pre) * (1.0 + m[:, 1:2, :]) + m[:, 0:1, :]


def _gated_residual(x, m, g_post, out, res_w):
    y = out * lax.rsqrt(jnp.mean(out * out, axis=-1, keepdims=True) + EPS)
    return x + (res_w * (1.0 + m[:, 2:3, :])) * (y * g_post)


def _tok_layout(x):
    nb_total, sb_total, _ = x.shape
    if sb_total >= TOK_TILE:
        nb, sb = 1, TOK_TILE
    else:
        nb, sb = SAMPLE_GROUP, sb_total
    tiles = sb_total // sb
    grid = (nb_total // nb) * tiles

    def spec(width):
        return pl.BlockSpec((nb, sb, width), lambda i: (i // tiles, i % tiles, 0))

    def spec_t(width):
        return pl.BlockSpec((nb, width, sb), lambda i: (i // tiles, 0, i % tiles))

    mod_spec = pl.BlockSpec((nb, 3, D_MODEL), lambda i: (i // tiles, 0, 0))
    return nb, sb, grid, spec, spec_t, mod_spec


def _full(shape):
    return pl.BlockSpec(shape, lambda *_: (0,) * len(shape))


def _ada_kernel(c_ref, w_ref, b_ref, o_ref):
    c = c_ref[...]
    a = (c * jax.nn.sigmoid(c)).astype(BF16)
    o_ref[0] = _dot(a, w_ref[0].astype(BF16)) + b_ref[0]


def _ada(c_all, w_ada, b_ada):
    rows = c_all.shape[0]
    n_out = w_ada.shape[-1]
    return pl.pallas_call(
        _ada_kernel,
        grid=(DEPTH, n_out // ADA_TN),
        in_specs=[pl.BlockSpec((rows, D_MODEL), lambda l, n: (0, 0)),
                  pl.BlockSpec((1, D_MODEL, ADA_TN), lambda l, n: (l, 0, n)),
                  pl.BlockSpec((1, 1, ADA_TN), lambda l, n: (l, 0, n))],
        out_specs=pl.BlockSpec((1, rows, ADA_TN), lambda l, n: (l, 0, n)),
        out_shape=jax.ShapeDtypeStruct((DEPTH, rows, n_out), F32),
        compiler_params=_cparams(2, VMEM_LIMIT),
        name="ada",
    )(c_all, w_ada, b_ada.reshape(DEPTH, 1, n_out))


def _ffn_kernel(x_ref, mod_ref, gpre_ref, gpost_ref, wg_ref, wu_ref, wd_ref, o_ref):
    x = x_ref[...]
    m = mod_ref[...]
    nb, sb, d = x.shape
    h = _norm_mod(x, m, gpre_ref[...]).reshape(nb * sb, d).astype(BF16)
    g = _dot(h, wg_ref[...])
    u = _dot(h, wu_ref[...])
    a = ((g * jax.nn.sigmoid(g)) * u).astype(BF16)
    out = _dot(a, wd_ref[...]).reshape(nb, sb, d)
    o_ref[...] = _gated_residual(x, m, gpost_ref[...], out, MACARON_W)


def _ffn(x, mod, g_pre, g_post, wg, wu, wd):
    nb, sb, grid, spec, _, mod_spec = _tok_layout(x)
    return pl.pallas_call(
        _ffn_kernel,
        grid=(grid,),
        in_specs=[spec(D_MODEL), mod_spec, _full((1, D_MODEL)), _full((1, D_MODEL)),
                  _full((D_MODEL, D_FF)), _full((D_MODEL, D_FF)), _full((D_FF, D_MODEL))],
        out_specs=spec(D_MODEL),
        out_shape=jax.ShapeDtypeStruct(x.shape, F32),
        compiler_params=_cparams(1, VMEM_LIMIT),
        name="ffn",
    )(x, mod, g_pre, g_post, wg, wu, wd)


def _inproj_ab_kernel(x_ref, mod_ref, gpre_ref, w_ref, qkv_ref, ka_ref, va_ref, kb_ref, vb_ref):
    x = x_ref[...]
    nb, sb, d = x.shape
    h = _norm_mod(x, mod_ref[...], gpre_ref[...]).reshape(nb * sb, d).astype(BF16)
    f32_outs = {1: ka_ref, 2: va_ref, 4: kb_ref, 5: vb_ref}
    for c in range(6):
        cols = slice(c * D_A, (c + 1) * D_A)
        p = _dot(h, w_ref[:, cols]).reshape(nb, sb, D_A)
        if c in f32_outs:
            f32_outs[c][...] = p
            qkv_ref[:, :, cols] = p.astype(BF16)
        else:
            qkv_ref[:, :, cols] = (p * ATTN_SCALE).astype(BF16)


def _inproj_ab(x, mod, g_pre, w_in):
    nb, sb, grid, spec, _, mod_spec = _tok_layout(x)
    nbt, sbt, _ = x.shape
    f32_out = jax.ShapeDtypeStruct((nbt, sbt, D_A), F32)
    return pl.pallas_call(
        _inproj_ab_kernel,
        grid=(grid,),
        in_specs=[spec(D_MODEL), mod_spec, _full((1, D_MODEL)), _full((D_MODEL, 6 * D_A))],
        out_specs=[spec(6 * D_A)] + [spec(D_A)] * 4,
        out_shape=[jax.ShapeDtypeStruct((nbt, sbt, 6 * D_A), BF16)] + [f32_out] * 4,
        compiler_params=_cparams(1, VMEM_LIMIT),
        name="inproj_ab",
    )(x, mod, g_pre, w_in)


def _log_sigmoid(z):
    return jnp.minimum(z, 0.0) - jnp.log1p(jnp.exp(-jnp.abs(z)))


def _inproj_c_kernel(x_ref, mod_ref, gpre_ref, w_ref, wf_ref, bf_ref,
                     qkv_ref, k_ref, v_ref, logf_ref):
    x = x_ref[...]
    nb, sb, d = x.shape
    h = _norm_mod(x, mod_ref[...], gpre_ref[...]).reshape(nb * sb, d).astype(BF16)
    f32_outs = {1: k_ref, 2: v_ref}
    for c in range(3):
        cols = slice(c * D_C, (c + 1) * D_C)
        p = _dot(h, w_ref[:, cols]).reshape(nb, sb, D_C)
        if c in f32_outs:
            f32_outs[c][...] = p
            qkv_ref[:, :, cols] = p.astype(BF16)
        else:
            qkv_ref[:, :, cols] = (p * ATTN_SCALE).astype(BF16)
    logf = _log_sigmoid(_dot(h, wf_ref[...]) + bf_ref[...])
    logf_ref[...] = logf.reshape(nb, sb, LANES)


def _inproj_c(x, mod, g_pre, w_qkv, w_f, b_f):
    nb, sb, grid, spec, _, mod_spec = _tok_layout(x)
    nbt, sbt, _ = x.shape
    f32_out = jax.ShapeDtypeStruct((nbt, sbt, D_C), F32)
    return pl.pallas_call(
        _inproj_c_kernel,
        grid=(grid,),
        in_specs=[spec(D_MODEL), mod_spec, _full((1, D_MODEL)), _full((D_MODEL, 3 * D_C)),
                  _full((D_MODEL, LANES)), _full((1, LANES))],
        out_specs=[spec(3 * D_C), spec(D_C), spec(D_C), spec(LANES)],
        out_shape=[jax.ShapeDtypeStruct((nbt, sbt, 3 * D_C), BF16), f32_out, f32_out,
                   jax.ShapeDtypeStruct((nbt, sbt, LANES), F32)],
        compiler_params=_cparams(1, VMEM_LIMIT),
        name="inproj_c",
    )(x, mod, g_pre, w_qkv, w_f, b_f)


def _outproj_kernel(*refs, n_in):
    x_ref, mod_ref, gpost_ref = refs[:3]
    a_refs = refs[3:3 + n_in]
    w_refs = refs[3 + n_in:3 + 2 * n_in]
    o_ref = refs[-1]
    x = x_ref[...]
    nb, sb, d = x.shape
    out = None
    for a_ref, w_ref in zip(a_refs, w_refs):
        a = a_ref[...]
        t = _dot(a.reshape(nb * sb, a.shape[-1]), w_ref[...])
        out = t if out is None else out + t
    o_ref[...] = _gated_residual(x, mod_ref[...], gpost_ref[...], out.reshape(nb, sb, d), 1.0)


def _outproj(x, mod, g_post, acts, weights):
    nb, sb, grid, spec, _, mod_spec = _tok_layout(x)
    n_in = len(acts)
    return pl.pallas_call(
        functools.partial(_outproj_kernel, n_in=n_in),
        grid=(grid,),
        in_specs=([spec(D_MODEL), mod_spec, _full((1, D_MODEL))]
                  + [spec(a.shape[-1]) for a in acts]
                  + [_full(w.shape) for w in weights]),
        out_specs=spec(D_MODEL),
        out_shape=jax.ShapeDtypeStruct(x.shape, F32),
        compiler_params=_cparams(1, VMEM_LIMIT),
        name="outproj",
    )(x, mod, g_post, *acts, *weights)


def _inproj_ab_prompt_kernel(x_ref, mod_ref, gpre_ref, w_ref, wt_ref,
                             qt_ref, k_ref, vt_ref, kat_ref, vat_ref, kb_ref, vb_ref):
    x = x_ref[...]
    _, sb, d = x.shape
    h = _norm_mod(x, mod_ref[...], gpre_ref[...]).reshape(sb, d).astype(BF16)

    def nat(c):
        return _dot(h, w_ref[:, c * D_A:(c + 1) * D_A])

    def tr(c):
        return _dot_nt(wt_ref[c * D_A:(c + 1) * D_A, :], h)

    qt_ref[0, :D_A, :] = (tr(0) * QSCALE_LOG2).astype(BF16)
    qt_ref[0, D_A:, :] = (tr(3) * QSCALE_LOG2).astype(BF16)
    kat_ref[0] = tr(1)
    k_ref[0, :, :D_A] = nat(1).astype(BF16)
    kb = nat(4)
    kb_ref[0] = kb
    k_ref[0, :, D_A:] = kb.astype(BF16)
    va_t = tr(2)
    vat_ref[0] = va_t
    vt_ref[0, :D_A, :] = va_t.astype(BF16)
    vt_ref[0, D_A:, :] = tr(5).astype(BF16)
    vb_ref[0] = nat(5)


def _inproj_ab_prompt(x, mod, g_pre, w_in, w_in_t):
    nb, sb, grid, spec, spec_t, mod_spec = _tok_layout(x)
    b, s, _ = x.shape
    width = D_A + D_B
    return pl.pallas_call(
        _inproj_ab_prompt_kernel,
        grid=(grid,),
        in_specs=[spec(D_MODEL), mod_spec, _full((1, D_MODEL)), _full(w_in.shape), _full(w_in_t.shape)],
        out_specs=[spec_t(width), spec(width), spec_t(width), spec_t(D_A), spec_t(D_A),
                   spec(D_B), spec(D_B)],
        out_shape=[jax.ShapeDtypeStruct((b, width, s), BF16),
                   jax.ShapeDtypeStruct((b, s, width), BF16),
                   jax.ShapeDtypeStruct((b, width, s), BF16),
                   jax.ShapeDtypeStruct((b, D_A, s), F32),
                   jax.ShapeDtypeStruct((b, D_A, s), F32),
                   jax.ShapeDtypeStruct((b, s, D_B), F32),
                   jax.ShapeDtypeStruct((b, s, D_B), F32)],
        compiler_params=_cparams(1, VMEM_LIMIT),
        name="inproj_ab_prompt",
    )(x, mod, g_pre, w_in, w_in_t)


def _inproj_c_prompt_kernel(x_ref, mod_ref, gpre_ref, wk_ref, wt_ref, wf_ref, wft_ref, bf_ref, bft_ref,
                            qt_ref, k_ref, vt_ref, kt32_ref, vt32_ref, logf_ref, logft_ref):
    x = x_ref[...]
    _, sb, d = x.shape
    h = _norm_mod(x, mod_ref[...], gpre_ref[...]).reshape(sb, d).astype(BF16)
    qt_ref[0] = (_dot_nt(wt_ref[:D_C, :], h) * QSCALE_LOG2).astype(BF16)
    k_ref[0] = _dot(h, wk_ref[...]).astype(BF16)
    kt32_ref[0] = _dot_nt(wt_ref[D_C:2 * D_C, :], h)
    v_t = _dot_nt(wt_ref[2 * D_C:, :], h)
    vt32_ref[0] = v_t
    vt_ref[0] = v_t.astype(BF16)
    logf_ref[0] = _log_sigmoid(_dot(h, wf_ref[...]) + bf_ref[...])
    logft_ref[0] = _log_sigmoid(_dot_nt(wft_ref[...], h) + bft_ref[...])[:H_C, :]


def _inproj_c_prompt(x, mod, g_pre, w_k, w_qkv_t, w_f, w_f_t, b_f, b_f_t):
    nb, sb, grid, spec, spec_t, mod_spec = _tok_layout(x)
    b, s, _ = x.shape
    return pl.pallas_call(
        _inproj_c_prompt_kernel,
        grid=(grid,),
        in_specs=[spec(D_MODEL), mod_spec, _full((1, D_MODEL)), _full(w_k.shape), _full(w_qkv_t.shape),
                  _full(w_f.shape), _full(w_f_t.shape), _full(b_f.shape), _full(b_f_t.shape)],
        out_specs=[spec_t(D_C), spec(D_C), spec_t(D_C), spec_t(D_C), spec_t(D_C), spec(LANES),
                   spec_t(H_C)],
        out_shape=[jax.ShapeDtypeStruct((b, D_C, s), BF16),
                   jax.ShapeDtypeStruct((b, s, D_C), BF16),
                   jax.ShapeDtypeStruct((b, D_C, s), BF16),
                   jax.ShapeDtypeStruct((b, D_C, s), F32),
                   jax.ShapeDtypeStruct((b, D_C, s), F32),
                   jax.ShapeDtypeStruct((b, s, LANES), F32),
                   jax.ShapeDtypeStruct((b, H_C, s), F32)],
        compiler_params=_cparams(1, VMEM_LIMIT),
        name="inproj_c_prompt",
    )(x, mod, g_pre, w_k, w_qkv_t, w_f, w_f_t, b_f, b_f_t)


def _outproj_prompt_kernel(*refs, n_in):
    x_ref, mod_ref, gpost_ref = refs[:3]
    a_refs = refs[3:3 + n_in]
    wt_refs = refs[3 + n_in:3 + 2 * n_in]
    o_ref = refs[-1]
    x = x_ref[...]
    out_t = None
    for a_ref, wt_ref in zip(a_refs, wt_refs):
        t = _dot(wt_ref[...], a_ref[0])
        out_t = t if out_t is None else out_t + t
    out = out_t.T.reshape(x.shape)
    o_ref[...] = _gated_residual(x, mod_ref[...], gpost_ref[...], out, 1.0)


def _outproj_prompt(x, mod, g_post, acts_t, weights_t):
    nb, sb, grid, spec, spec_t, mod_spec = _tok_layout(x)
    n_in = len(acts_t)
    return pl.pallas_call(
        functools.partial(_outproj_prompt_kernel, n_in=n_in),
        grid=(grid,),
        in_specs=([spec(D_MODEL), mod_spec, _full((1, D_MODEL))]
                  + [spec_t(a.shape[1]) for a in acts_t]
                  + [_full(w.shape) for w in weights_t]),
        out_specs=spec(D_MODEL),
        out_shape=jax.ShapeDtypeStruct(x.shape, F32),
        compiler_params=_cparams(1, VMEM_LIMIT),
        name="outproj_prompt",
    )(x, mod, g_post, *acts_t, *weights_t)


def _diff_lambda(lam_ref, lam_init):
    lv = lam_ref[...]
    a = jnp.sum(lv[0:1] * lv[1:2], axis=1, keepdims=True)
    b = jnp.sum(lv[2:3] * lv[3:4], axis=1, keepdims=True)
    return jnp.exp(a) - jnp.exp(b) + lam_init


def _alibi_slope(g):
    slope = jnp.float32(0.0)
    for h in range(H_B):
        slope = jnp.where(g == h, jnp.float32(2.0 ** (-8.0 * (h + 1) / H_B)), slope)
    return slope


def _select_lane(a, idx):
    lane = lax.broadcasted_iota(jnp.int32, a.shape, 1)
    return jnp.sum(jnp.where(lane == idx, a, 0.0), axis=1, keepdims=True)


def _select_row(a, idx):
    row = lax.broadcasted_iota(jnp.int32, a.shape, 0)
    return jnp.sum(jnp.where(row == idx, a, 0.0), axis=0, keepdims=True)


def _relpos_bias_kernel(tab_ref, pbias_ref, sbias_ref):
    t = tab_ref[0]
    t_hi = t.astype(BF16)
    r1 = t - t_hi.astype(F32)
    t_mid = r1.astype(BF16)
    t_lo = (r1 - t_mid.astype(F32)).astype(BF16)
    ent = lax.broadcasted_iota(jnp.int32, (TAB_PAD, TOEP), 0)
    n = lax.broadcasted_iota(jnp.int32, (TAB_PAD, TOEP), 1)

    def lookup(rel_pos):
        onehot = jnp.where(ent == jnp.clip(rel_pos, -REL_CLIP, REL_CLIP) + REL_CLIP, 1.0, 0.0)
        onehot = onehot.astype(BF16)
        return _dot(t_hi, onehot) + _dot(t_mid, onehot) + _dot(t_lo, onehot)

    gen_t = lookup(A_PAST + jnp.where(n < TP, n, n - TOEP)) * LOG2E
    d = (lax.broadcasted_iota(jnp.int32, (BAND_A, TP), 0) // CHUNK
         - lax.broadcasted_iota(jnp.int32, (BAND_A, TP), 1) // CHUNK)
    valid = (d >= 0) & (d <= A_LEFT_CHUNKS)
    gen = lookup(A_PAST + TP - 1 - n)
    t_sample = sbias_ref.shape[2]
    for h in range(H_A):
        rows_t = jnp.broadcast_to(gen_t[h:h + 1, :], (BAND_A, TOEP))
        toep_t = pltpu.roll(rows_t, 0, 1, stride=1, stride_axis=0)
        pbias_ref[0, h] = jnp.where(valid, toep_t[:, :TP], NEG_INF)
        rows = jnp.broadcast_to(gen[h:h + 1, :], (t_sample, TOEP))
        toep = pltpu.roll(rows, TOEP - TP + 1, 1, stride=1, stride_axis=0)
        sbias_ref[0, h] = toep[:, :SBIAS_W]


def _relpos_bias(relpos_a, sample_len):
    ne = relpos_a.shape[0]
    tab = jnp.pad(relpos_a, ((0, 0), (0, 0), (0, TAB_PAD - relpos_a.shape[-1])))
    return pl.pallas_call(
        _relpos_bias_kernel,
        grid=(ne,),
        in_specs=[pl.BlockSpec((1, H_A, TAB_PAD), lambda l: (l, 0, 0))],
        out_specs=[pl.BlockSpec((1, H_A, BAND_A, TP), lambda l: (l, 0, 0, 0)),
                   pl.BlockSpec((1, H_A, sample_len, SBIAS_W), lambda l: (l, 0, 0, 0))],
        out_shape=[jax.ShapeDtypeStruct((ne, H_A, BAND_A, TP), F32),
                   jax.ShapeDtypeStruct((ne, H_A, sample_len, SBIAS_W), F32)],
        compiler_params=_cparams(1, VMEM_LIMIT),
        name="relpos_bias",
    )(tab)


def _stats_init(m_ref, l_ref, acc_ref):
    m_ref[...] = jnp.full(m_ref.shape, NEG_INF, F32)
    l_ref[...] = jnp.zeros(l_ref.shape, F32)
    acc_ref[...] = jnp.zeros(acc_ref.shape, F32)


def _stats_update(m_ref, l_ref, acc_ref, mi, s_t, v_t):
    m_prev = m_ref[mi]
    m_new = jnp.maximum(m_prev, jnp.max(s_t, axis=0, keepdims=True))
    alpha = jnp.exp2(m_prev - m_new)
    p = jnp.exp2(s_t - m_new)
    l_ref[mi] = alpha * l_ref[mi] + jnp.sum(p, axis=0, keepdims=True)
    acc_ref[mi] = alpha * acc_ref[mi] + _dot(v_t, p.astype(BF16))
    m_ref[mi] = m_new


def _stats_result(l_ref, acc_ref, mi):
    return acc_ref[mi] * (1.0 / l_ref[mi])


def _stats_update_all(m_ref, l_ref, acc_ref, scores, values):
    n = len(scores)
    m_prev = [m_ref[i] for i in range(n)]
    m_new = [jnp.maximum(m_prev[i], jnp.max(scores[i], axis=0, keepdims=True)) for i in range(n)]
    alpha = [jnp.exp2(m_prev[i] - m_new[i]) for i in range(n)]
    p = [jnp.exp2(scores[i] - m_new[i]) for i in range(n)]
    pv = [_dot(values[i], p[i].astype(BF16)) for i in range(n)]
    for i in range(n):
        l_ref[i] = alpha[i] * l_ref[i] + jnp.sum(p[i], axis=0, keepdims=True)
        acc_ref[i] = alpha[i] * acc_ref[i] + pv[i]
        m_ref[i] = m_new[i]


def _stats_scratch(n, d):
    return [pltpu.VMEM((n, 1, TP), F32), pltpu.VMEM((n, 1, TP), F32), pltpu.VMEM((n, d, TP), F32)]


def _map_rows(q_t, mi):
    row = lax.broadcasted_iota(jnp.int32, q_t.shape, 0)
    keep = (row >= mi * HEAD_DIM) & (row < (mi + 1) * HEAD_DIM)
    return jnp.where(keep, q_t, jnp.zeros_like(q_t))


def _attn_a_prompt_kernel(qt_ref, k_ref, vt_ref, bias_ref, o_ref, m_ref, l_ref, acc_ref, *, hp):
    qi = pl.program_id(2)
    qm = [_map_rows(qt_ref[0, p * LANES:(p + 1) * LANES, :], mi) for p in range(hp) for mi in range(2)]
    _stats_init(m_ref, l_ref, acc_ref)
    for j in range(BAND_TILES):
        kt = qi - (BAND_TILES - 1) + j

        def tile(j=j, kt=kt):
            start = pl.multiple_of(kt * TP, TP)
            scores, values = [], []
            for p in range(hp):
                cols = slice(p * LANES, (p + 1) * LANES)
                k = k_ref[0, pl.ds(start, TP), cols]
                v_t = vt_ref[0, cols, pl.ds(start, TP)]
                for mi in range(2):
                    scores.append(_dot(k, qm[2 * p + mi]) + bias_ref[2 * p + mi, j * TP:(j + 1) * TP, :])
                    values.append(v_t[mi * HEAD_DIM:(mi + 1) * HEAD_DIM])
            _stats_update_all(m_ref, l_ref, acc_ref, scores, values)

        if j == BAND_TILES - 1:
            tile()
        else:
            pl.when(kt >= 0)(tile)
    o = jnp.concatenate([_stats_result(l_ref, acc_ref, i) for i in range(2 * hp)], axis=0)
    o_ref[0] = o.astype(o_ref.dtype)


def _attn_a_prompt(q_t, k, v_t, pbias):
    b, s, _ = k.shape
    hp = HP_A
    groups = D_A // (hp * LANES)
    return pl.pallas_call(
        functools.partial(_attn_a_prompt_kernel, hp=hp),
        grid=(b, groups, s // TP),
        in_specs=[pl.BlockSpec((1, hp * LANES, TP), lambda b, g, i: (b, g, i)),
                  pl.BlockSpec((1, s, hp * LANES), lambda b, g, i: (b, 0, g)),
                  pl.BlockSpec((1, hp * LANES, s), lambda b, g, i: (b, g, 0)),
                  pl.BlockSpec((2 * hp, BAND_A, TP), lambda b, g, i: (g, 0, 0))],
        out_specs=pl.BlockSpec((1, hp * LANES, TP), lambda b, g, i: (b, g, i)),
        out_shape=jax.ShapeDtypeStruct((b, D_A, s), BF16),
        scratch_shapes=_stats_scratch(2 * hp, HEAD_DIM),
        compiler_params=_cparams(3, VMEM_LIMIT),
        name="attn_a_prompt",
    )(q_t, k, v_t, pbias)


def _attn_b_prompt_kernel(qt_ref, k_ref, vt_ref, ke_ref, lam_ref, subln_ref, o_ref,
                          m_ref, l_ref, acc_ref, diag_ref, *, lam_init, hp):
    g = pl.program_id(1)
    qi = pl.program_id(2)
    coefs = [_alibi_slope(g * hp + p) * LOG2E for p in range(hp)]

    @pl.when(qi == 0)
    def _():
        kr = lax.broadcasted_iota(jnp.int32, (TP, TP), 0)
        qc = lax.broadcasted_iota(jnp.int32, (TP, TP), 1)
        dist = jnp.abs(qc - kr).astype(F32)
        for p in range(hp):
            diag_ref[p] = jnp.where(kr // CHUNK <= qc // CHUNK, -coefs[p] * dist, NEG_INF)

    row = lax.broadcasted_iota(jnp.int32, (LANES, TP), 0)
    qpos = (qi * TP + lax.broadcasted_iota(jnp.int32, (LANES, TP), 1)).astype(F32)
    part = row % 3
    qa_past, qa_diag = [], []
    for p in range(hp):
        c = coefs[p]
        q_t = qt_ref[0, p * LANES:(p + 1) * LANES, :]
        hi, mid, lo = _split3(jnp.where(row < 3, CHUNK * c, jnp.where(row < 6, c, -c * qpos)))
        qe = jnp.where(row < 9, jnp.where(part == 0, hi, jnp.where(part == 1, mid, lo)), 0.0).astype(BF16)
        for mi in range(2):
            qm = _map_rows(q_t, mi)
            qa_past.append(jnp.concatenate([qm, qe], axis=0))
            qa_diag.append(jnp.concatenate([qm, jnp.zeros_like(qe)], axis=0))
    _stats_init(m_ref, l_ref, acc_ref)

    def tile(j, qa, diag):
        start = pl.multiple_of(j * TP, TP)
        ke = ke_ref[pl.ds(start, TP), :]
        scores, values = [], []
        for p in range(hp):
            cols = slice(p * LANES, (p + 1) * LANES)
            kx = jnp.concatenate([k_ref[0, pl.ds(start, TP), cols], ke], axis=1)
            v_t = vt_ref[0, cols, pl.ds(start, TP)]
            for mi in range(2):
                s_t = _dot(kx, qa[2 * p + mi])
                scores.append(s_t + diag_ref[p] if diag else s_t)
                values.append(v_t)
        _stats_update_all(m_ref, l_ref, acc_ref, scores, values)

    def past_tile(j, carry):
        tile(j, qa_past, False)
        return carry

    lax.fori_loop(0, qi, past_tile, 0)
    tile(qi, qa_diag, True)
    lam = _diff_lambda(lam_ref, lam_init)
    outs = []
    for p in range(hp):
        o = _stats_result(l_ref, acc_ref, 2 * p) - lam * _stats_result(l_ref, acc_ref, 2 * p + 1)
        y = o * lax.rsqrt(jnp.mean(o * o, axis=0, keepdims=True) + EPS)
        outs.append((y * subln_ref[...]) * (1.0 - lam_init))
    o_ref[0] = jnp.concatenate(outs, axis=0).astype(o_ref.dtype)


def _attn_b_prompt(q_t, k, v_t, lam_vec, subln_col, lam_init):
    b, s, _ = k.shape
    hp = HP_B
    groups = D_B // (hp * LANES)
    blk0 = D_A // (hp * LANES)
    pos = jnp.arange(s, dtype=jnp.int32)[:, None]
    lane = jnp.arange(LANES, dtype=jnp.int32)[None, :]
    key_extras = jnp.where(lane < 3, pos // CHUNK,
                           jnp.where(lane < 6, pos % CHUNK, jnp.where(lane < 9, 1, 0))).astype(BF16)
    return pl.pallas_call(
        functools.partial(_attn_b_prompt_kernel, lam_init=lam_init, hp=hp),
        grid=(b, groups, s // TP),
        in_specs=[pl.BlockSpec((1, hp * LANES, TP), lambda b, g, i: (b, blk0 + g, i)),
                  pl.BlockSpec((1, s, hp * LANES), lambda b, g, i: (b, 0, blk0 + g)),
                  pl.BlockSpec((1, hp * LANES, s), lambda b, g, i: (b, blk0 + g, 0)),
                  _full(key_extras.shape), _full(lam_vec.shape), _full(subln_col.shape)],
        out_specs=pl.BlockSpec((1, hp * LANES, TP), lambda b, g, i: (b, g, i)),
        out_shape=jax.ShapeDtypeStruct((b, D_B, s), BF16),
        scratch_shapes=_stats_scratch(2 * hp, 2 * HEAD_DIM) + [pltpu.VMEM((hp, TP, TP), F32)],
        compiler_params=_cparams(3, VMEM_LIMIT),
        name="attn_b_prompt",
    )(q_t, k, v_t, key_extras, lam_vec, subln_col)


def _logf_scan_kernel(logf_ref, dqt_ref, e_ref):
    r = lax.broadcasted_iota(jnp.int32, (SCAN_T, SCAN_T), 0)
    c = lax.broadcasted_iota(jnp.int32, (SCAN_T, SCAN_T), 1)
    tri = jnp.where(r >= c, 1.0, 0.0).astype(F32)
    i_sel = lax.broadcasted_iota(jnp.int32, (3 * LANES, LANES), 0)
    j_sel = lax.broadcasted_iota(jnp.int32, (3 * LANES, LANES), 1)
    in_dk = (j_sel >= 3) & (j_sel < 9)
    sels = []
    for g in range(H_C // 2):
        src = ((j_sel - 3) % 3) * LANES + 2 * g + (j_sel - 3) // 3
        sels.append(jnp.where(in_dk & (i_sel == src), 1.0, 0.0).astype(BF16))
    ones_cols = jnp.where(lax.broadcasted_iota(jnp.int32, (1, LANES), 1) < 3, 1.0, 0.0)
    carry = jnp.zeros((1, LANES), F32)
    for t in range(logf_ref.shape[1] // SCAN_T):
        rows = slice(t * SCAN_T, (t + 1) * SCAN_T)
        cs = _dot_exact(tri, logf_ref[0, rows, :]) + carry
        carry = cs[SCAN_T - 1:SCAN_T, :]
        x = cs * LOG2E
        dqt_ref[0, :, rows] = x.T[:H_C, :]
        hi, mid, lo = _split3(x)
        parts = jnp.concatenate([hi.astype(BF16), mid.astype(BF16), lo.astype(BF16)], axis=1)
        for g in range(H_C // 2):
            e_ref[0, g, rows, :] = (_dot(parts, sels[g]) + ones_cols).astype(BF16)


def _logf_scan(logf):
    b, s, _ = logf.shape
    return pl.pallas_call(
        _logf_scan_kernel,
        grid=(b,),
        in_specs=[pl.BlockSpec((1, s, LANES), lambda b: (b, 0, 0))],
        out_specs=[pl.BlockSpec((1, H_C, s), lambda b: (b, 0, 0)),
                   pl.BlockSpec((1, H_C // 2, s, LANES), lambda b: (b, 0, 0, 0))],
        out_shape=[jax.ShapeDtypeStruct((b, H_C, s), F32),
                   jax.ShapeDtypeStruct((b, H_C // 2, s, LANES), BF16)],
        compiler_params=_cparams(1, VMEM_LIMIT),
        name="logf_scan",
    )(logf)


def _attn_c_prompt_kernel(qt_ref, k_ref, vt_ref, dqt_ref, e_ref, o_ref, m_ref, l_ref, acc_ref, *, hp):
    qi = pl.program_id(2)
    row = lax.broadcasted_iota(jnp.int32, (LANES, TP), 0)
    qa = []
    for p in range(hp):
        q_t = qt_ref[0, p * LANES:(p + 1) * LANES, :]
        dq = dqt_ref[0, p]
        for mi in range(2):
            hi, mid, lo = _split3(dq[mi:mi + 1, :])
            dk_rows = (row >= 3 + 3 * mi) & (row < 6 + 3 * mi)
            qe = jnp.where(row == 0, hi, jnp.where(row == 1, mid, jnp.where(row == 2, lo,
                           jnp.where(dk_rows, -1.0, 0.0))))
            qa.append(jnp.concatenate([_map_rows(q_t, mi), qe.astype(BF16)], axis=0))
    _stats_init(m_ref, l_ref, acc_ref)

    def tile(j, masked):
        start = pl.multiple_of(j * TP, TP)
        scores, values = [], []
        for p in range(hp):
            cols = slice(p * LANES, (p + 1) * LANES)
            kx = jnp.concatenate([k_ref[0, pl.ds(start, TP), cols], e_ref[0, p, pl.ds(start, TP), :]],
                                 axis=1)
            v_t = vt_ref[0, cols, pl.ds(start, TP)]
            for mi in range(2):
                s_t = _dot(kx, qa[2 * p + mi])
                if masked:
                    kr = lax.broadcasted_iota(jnp.int32, (TP, TP), 0)
                    qc = lax.broadcasted_iota(jnp.int32, (TP, TP), 1)
                    s_t = jnp.where(kr <= qc, s_t, NEG_INF)
                scores.append(s_t)
                values.append(v_t[mi * HEAD_DIM:(mi + 1) * HEAD_DIM])
        _stats_update_all(m_ref, l_ref, acc_ref, scores, values)

    def past_tile(j, carry):
        tile(j, False)
        return carry

    lax.fori_loop(0, qi, past_tile, 0)
    tile(qi, True)
    o = jnp.concatenate([_stats_result(l_ref, acc_ref, i) for i in range(2 * hp)], axis=0)
    o_ref[0] = o.astype(o_ref.dtype)


def _attn_c_prompt(q_t, k, v_t, dq_t, key_extras):
    b, s, _ = k.shape
    hp = HP_C
    groups = D_C // (hp * LANES)
    dq_t = dq_t.reshape(b, D_C // LANES, 2, s)
    return pl.pallas_call(
        functools.partial(_attn_c_prompt_kernel, hp=hp),
        grid=(b, groups, s // TP),
        in_specs=[pl.BlockSpec((1, hp * LANES, TP), lambda b, g, i: (b, g, i)),
                  pl.BlockSpec((1, s, hp * LANES), lambda b, g, i: (b, 0, g)),
                  pl.BlockSpec((1, hp * LANES, s), lambda b, g, i: (b, g, 0)),
                  pl.BlockSpec((1, hp, 2, TP), lambda b, g, i: (b, g, 0, i)),
                  pl.BlockSpec((1, hp, s, LANES), lambda b, g, i: (b, g, 0, 0))],
        out_specs=pl.BlockSpec((1, hp * LANES, TP), lambda b, g, i: (b, g, i)),
        out_shape=jax.ShapeDtypeStruct((b, D_C, s), BF16),
        scratch_shapes=_stats_scratch(2 * hp, HEAD_DIM),
        compiler_params=_cparams(3, VMEM_LIMIT),
        name="attn_c_prompt",
    )(q_t, k, v_t, dq_t, key_extras)


def _split_maps(q):
    lane = lax.broadcasted_iota(jnp.int32, q.shape, q.ndim - 1)
    zero = jnp.zeros_like(q)
    return jnp.where(lane < HEAD_DIM, q, zero), jnp.where(lane >= HEAD_DIM, q, zero)


def _pair_heads(o0, o1):
    lane = lax.broadcasted_iota(jnp.int32, o0.shape, 1)
    return jnp.where(lane < HEAD_DIM, o0, o1)


def _diff_out(o0, o1, lam, subln, lam_init):
    o = o0 - lam * o1
    y = o * lax.rsqrt(jnp.mean(o * o, axis=-1, keepdims=True) + EPS)
    return (y * subln) * (1.0 - lam_init)


def _sample_scores(q, kn, kc, cache_t):
    q0, q1 = _split_maps(q)
    qs = jnp.concatenate([q0, q1], axis=0)
    s_c = _dot(qs, kc) if cache_t else _dot_nt(qs, kc)
    return s_c, _dot_nt(qs, kn)


def _sample_softmax_pv(s_c, s_n, vn, vc, cache_t):
    m = jnp.maximum(jnp.max(s_c, axis=-1, keepdims=True), jnp.max(s_n, axis=-1, keepdims=True))
    p_c = jnp.exp(s_c - m)
    p_n = jnp.exp(s_n - m)
    l = jnp.sum(p_c, axis=-1, keepdims=True) + jnp.sum(p_n, axis=-1, keepdims=True)
    p_c = p_c.astype(BF16)
    o_c = _dot_nt(p_c, vc) if cache_t else _dot(p_c, vc)
    o = (o_c + _dot(p_n.astype(BF16), vn)) * (1.0 / l)
    t = o.shape[0] // 2
    return o[:t], o[t:]


def _cache_pair_t(ref):
    blk = ref[0, 0]
    return blk.reshape(2 * HEAD_DIM, blk.shape[-1]).astype(BF16)


def _attn_a_sample_kernel(q_ref, kn_ref, vn_ref, kc_ref, vc_ref, bias_ref, o_ref):
    kc = _cache_pair_t(kc_ref)
    s_c, s_n = _sample_scores(q_ref[0], kn_ref[0], kc, True)
    t = q_ref.shape[1]
    w = kc.shape[1]
    bias = bias_ref[...].reshape(2 * t, SBIAS_W)
    o0, o1 = _sample_softmax_pv(s_c + bias[:, :w], s_n + bias[:, w:w + t], vn_ref[0],
                                _cache_pair_t(vc_ref), True)
    o_ref[0] = _pair_heads(o0, o1).astype(o_ref.dtype)


def _attn_c_sample_kernel(q_ref, kn_ref, vn_ref, kc_ref, vc_ref, logf_ref, clogf_ref, o_ref,
                          dpast_ref):
    g = pl.program_id(1)
    t = q_ref.shape[1]
    past = clogf_ref.shape[-1]

    @pl.when(g == 0)
    def _():
        r = lax.broadcasted_iota(jnp.int32, (SCAN_T, SCAN_T), 0)
        c = lax.broadcasted_iota(jnp.int32, (SCAN_T, SCAN_T), 1)
        triu = jnp.where(r <= c, 1.0, 0.0).astype(F32)
        carry = jnp.zeros((H_C, 1), F32)
        for i in range(past // SCAN_T):
            cols = slice(i * SCAN_T, (i + 1) * SCAN_T)
            cs = _dot_exact(clogf_ref[0, 0, :, cols], triu) + carry
            dpast_ref[:, cols] = cs
            carry = cs[:, SCAN_T - 1:SCAN_T]
        dpast_ref[...] = dpast_ref[...] - carry

    r = lax.broadcasted_iota(jnp.int32, (t, t), 0)
    c = lax.broadcasted_iota(jnp.int32, (t, t), 1)
    tri = jnp.where(r >= c, 1.0, 0.0).astype(F32)
    dq = _dot_exact(tri, logf_ref[0])
    er = lax.broadcasted_iota(jnp.int32, (LANES, LANES), 0)
    ec = lax.broadcasted_iota(jnp.int32, (LANES, LANES), 1)
    eye = jnp.where(er == ec, 1.0, 0.0).astype(F32)
    dq_t = _dot_nt_exact(eye, dq)
    dq_col = jnp.concatenate([_select_lane(dq, 2 * g), _select_lane(dq, 2 * g + 1)], axis=0)
    dpast = dpast_ref[...]
    top = lax.broadcasted_iota(jnp.int32, (2 * t, 1), 0) < t
    dk_c = jnp.where(top, _select_row(dpast, 2 * g), _select_row(dpast, 2 * g + 1))
    dk_n = jnp.where(top, _select_row(dq_t, 2 * g), _select_row(dq_t, 2 * g + 1))
    s_c, s_n = _sample_scores(q_ref[0], kn_ref[0], _cache_pair_t(kc_ref), True)
    s_c = s_c + dq_col - dk_c
    s_n = s_n + dq_col - dk_n
    qrow = lax.broadcasted_iota(jnp.int32, s_n.shape, 0) % t
    kcol = lax.broadcasted_iota(jnp.int32, s_n.shape, 1)
    s_n = jnp.where(kcol <= qrow, s_n, NEG_INF)
    o0, o1 = _sample_softmax_pv(s_c, s_n, vn_ref[0], _cache_pair_t(vc_ref), True)
    o_ref[0] = _pair_heads(o0, o1).astype(o_ref.dtype)


def _attn_pair_sample(kernel_fn, name, layer, qkv, cache_k_t, cache_v_t, extras, extra_specs,
                      scratch=()):
    b, t, _ = qkv.shape
    heads, _, past = cache_k_t.shape[2:]
    groups = heads // 2
    cache_spec = pl.BlockSpec((1, 1, 2, HEAD_DIM, past), lambda b, g: (layer, b, g, 0, 0))
    return pl.pallas_call(
        kernel_fn,
        grid=(b, groups),
        in_specs=[pl.BlockSpec((1, t, LANES), lambda b, g: (b, 0, g)),
                  pl.BlockSpec((1, t, LANES), lambda b, g: (b, 0, groups + g)),
                  pl.BlockSpec((1, t, LANES), lambda b, g: (b, 0, 2 * groups + g)),
                  cache_spec, cache_spec] + list(extra_specs),
        out_specs=pl.BlockSpec((1, t, LANES), lambda b, g: (b, 0, g)),
        out_shape=jax.ShapeDtypeStruct((b, t, groups * LANES), BF16),
        scratch_shapes=list(scratch),
        compiler_params=_cparams(2, VMEM_LIMIT),
        name=name,
    )(qkv, qkv, qkv, cache_k_t, cache_v_t, *extras)


def _attn_b_sample_kernel(q_ref, kn_ref, vn_ref, kc_ref, vc_ref, lam_ref, subln_ref, o_ref,
                          *, lam_init):
    t = q_ref.shape[1]
    past = kc_ref.shape[2] // H_B
    shape_c = (2 * t, past)
    shape_n = (2 * t, t)
    qpos_c = past + lax.broadcasted_iota(jnp.int32, shape_c, 0) % t
    dist_c = jnp.abs(qpos_c - lax.broadcasted_iota(jnp.int32, shape_c, 1)).astype(F32)
    dist_n = jnp.abs(lax.broadcasted_iota(jnp.int32, shape_n, 0) % t
                     - lax.broadcasted_iota(jnp.int32, shape_n, 1)).astype(F32)
    lam = _diff_lambda(lam_ref, lam_init)
    for h in range(H_B):
        cols = slice(h * LANES, (h + 1) * LANES)
        slope = 2.0 ** (-8.0 * (h + 1) / H_B)
        kc = kc_ref[0, 0, pl.ds(h, past, stride=H_B), :].astype(BF16)
        vc = vc_ref[0, 0, pl.ds(h, past, stride=H_B), :].astype(BF16)
        s_c, s_n = _sample_scores(q_ref[0, :, cols], kn_ref[0, :, cols], kc, False)
        o0, o1 = _sample_softmax_pv(s_c - slope * dist_c, s_n - slope * dist_n,
                                    vn_ref[0, :, cols], vc, False)
        o_ref[0, :, cols] = _diff_out(o0, o1, lam, subln_ref[...], lam_init).astype(o_ref.dtype)


def _attn_b_sample(layer, qkv, cache_k, cache_v, lam_vec, subln, lam_init):
    b, t, _ = qkv.shape
    rows = cache_k.shape[2]
    col0 = 3 * D_A // D_B
    cache_spec = pl.BlockSpec((1, 1, rows, LANES), lambda b: (layer, b, 0, 0))
    return pl.pallas_call(
        functools.partial(_attn_b_sample_kernel, lam_init=lam_init),
        grid=(b,),
        in_specs=[pl.BlockSpec((1, t, D_B), lambda b: (b, 0, col0)),
                  pl.BlockSpec((1, t, D_B), lambda b: (b, 0, col0 + 1)),
                  pl.BlockSpec((1, t, D_B), lambda b: (b, 0, col0 + 2)),
                  cache_spec, cache_spec, _full(lam_vec.shape), _full(subln.shape)],
        out_specs=pl.BlockSpec((1, t, D_B), lambda b: (b, 0, 0)),
        out_shape=jax.ShapeDtypeStruct((b, t, D_B), BF16),
        compiler_params=_cparams(1, VMEM_LIMIT),
        name="attn_b_sample",
    )(qkv, qkv, qkv, cache_k, cache_v, lam_vec, subln)


def _heads_last(a_t, heads):
    lead = a_t.shape[:-2]
    s = a_t.shape[-1]
    a = a_t.reshape(lead + (heads, HEAD_DIM, s))
    n = len(lead)
    return jnp.transpose(a, tuple(range(n)) + (n + 2, n, n + 1))


def kernel(x_prompt, x_sample, c_prompt, c_sample, cache_a_k, cache_a_v, cache_b_k, cache_b_v, cache_c_k, cache_c_v, cache_c_logf, w_ada, b_ada, norm_pre, norm_post, ffn_w_gate, ffn_w_up, ffn_w_down, w_in_ab, w_out_ab, relpos_a, lambda_b, subln_b, w_in_c, b_f, w_out_c):
    n_prompt, s_prompt = x_prompt.shape[:2]
    n_sample, t_sample = x_sample.shape[:2]

    wg = ffn_w_gate.astype(BF16)
    wu = ffn_w_up.astype(BF16)
    wd = ffn_w_down.astype(BF16)
    w_in_ab16 = w_in_ab.astype(BF16)
    w_in_ab16_t = jnp.swapaxes(w_in_ab, 1, 2).astype(BF16)
    w_out_ab16 = w_out_ab.astype(BF16)
    w_out_ab16_t = jnp.swapaxes(w_out_ab, 1, 2).astype(BF16)
    w_qkv_c16 = w_in_c[:, :, :3 * D_C].astype(BF16)
    w_qkv_c16_t = jnp.swapaxes(w_in_c[:, :, :3 * D_C], 1, 2).astype(BF16)
    w_f16 = jnp.pad(w_in_c[:, :, 3 * D_C:], ((0, 0), (0, 0), (0, LANES - H_C))).astype(BF16)
    w_f16_t = jnp.swapaxes(w_f16, 1, 2)
    b_f_row = jnp.pad(b_f, ((0, 0), (0, LANES - H_C)))[:, None, :]
    b_f_col = jnp.swapaxes(b_f_row, 1, 2)
    w_out_c16 = w_out_c.astype(BF16)
    w_out_c16_t = jnp.swapaxes(w_out_c, 1, 2).astype(BF16)
    cache_a_k_t = jnp.transpose(cache_a_k, (0, 1, 3, 4, 2))
    cache_a_v_t = jnp.transpose(cache_a_v, (0, 1, 3, 4, 2))
    cache_c_k_t = jnp.transpose(cache_c_k, (0, 1, 3, 4, 2))
    cache_c_v_t = jnp.transpose(cache_c_v, (0, 1, 3, 4, 2))
    clogf_t = jnp.swapaxes(cache_c_logf, -1, -2)
    past_b = cache_b_k.shape[2]
    cache_b_k_rows = cache_b_k.reshape(cache_b_k.shape[:2] + (past_b * H_B, 2 * HEAD_DIM))
    cache_b_v_rows = cache_b_v.reshape(cache_b_v.shape[:2] + (past_b * H_B, 2 * HEAD_DIM))

    rows = n_prompt + n_sample
    rows_pad = -(-rows // 8) * 8
    c_all = jnp.concatenate([c_prompt, c_sample, jnp.zeros((rows_pad - rows, D_MODEL), F32)], axis=0)
    mod_all = _ada(c_all, w_ada, b_ada).reshape(DEPTH, rows_pad, N_SUB, 3, D_MODEL)
    pbias, sbias = _relpos_bias(relpos_a, t_sample)

    def sublayer_params(li, row0, nrows):
        mod = [mod_all[li, row0:row0 + nrows, i] for i in range(N_SUB)]
        gpre = [norm_pre[li, i][None, :] for i in range(N_SUB)]
        gpost = [norm_post[li, i][None, :] for i in range(N_SUB)]
        return mod, gpre, gpost

    def run_prompt(x):
        a_k, a_v, b_k, b_v, c_k, c_v, c_lf = [], [], [], [], [], [], []
        for li in range(DEPTH):
            mod, gpre, gpost = sublayer_params(li, 0, n_prompt)
            x = _ffn(x, mod[0], gpre[0], gpost[0], wg[li, 0], wu[li, 0], wd[li, 0])
            j = li // 2
            if li % 2 == 0:
                lam_init = _lam_init(li)
                q_t, k, v_t, ka_t, va_t, kb, vb = _inproj_ab_prompt(
                    x, mod[1], gpre[1], w_in_ab16[j], w_in_ab16_t[j])
                out_a = _attn_a_prompt(q_t, k, v_t, pbias[j])
                out_b = _attn_b_prompt(q_t, k, v_t, lambda_b[j], subln_b[j][:, None], lam_init)
                w = min(A_PAST, s_prompt)
                a_k.append(ka_t[:, :, s_prompt - w:])
                a_v.append(va_t[:, :, s_prompt - w:])
                b_k.append(kb)
                b_v.append(vb)
                x = _outproj_prompt(x, mod[1], gpost[1], [out_a, out_b],
                                    [w_out_ab16_t[j, :, :D_A], w_out_ab16_t[j, :, D_A:]])
            else:
                q_t, k, v_t, k_t32, v_t32, logf, logf_t = _inproj_c_prompt(
                    x, mod[1], gpre[1], w_qkv_c16[j, :, D_C:2 * D_C], w_qkv_c16_t[j],
                    w_f16[j], w_f16_t[j], b_f_row[j], b_f_col[j])
                dq_t, key_extras = _logf_scan(logf)
                out = _attn_c_prompt(q_t, k, v_t, dq_t, key_extras)
                c_k.append(k_t32)
                c_v.append(v_t32)
                c_lf.append(logf_t)
                x = _outproj_prompt(x, mod[1], gpost[1], [out], [w_out_c16_t[j]])
            x = _ffn(x, mod[2], gpre[2], gpost[2], wg[li, 1], wu[li, 1], wd[li, 1])
        nb, sb = x.shape[:2]
        return (x,
                _heads_last(jnp.stack(a_k), H_A), _heads_last(jnp.stack(a_v), H_A),
                jnp.stack(b_k).reshape(-1, nb, sb, H_B, 2 * HEAD_DIM),
                jnp.stack(b_v).reshape(-1, nb, sb, H_B, 2 * HEAD_DIM),
                _heads_last(jnp.stack(c_k), H_C), _heads_last(jnp.stack(c_v), H_C),
                jnp.swapaxes(jnp.stack(c_lf), -1, -2))

    def run_sample(x):
        a_k, a_v, b_k, b_v, c_k, c_v, c_lf = [], [], [], [], [], [], []
        for li in range(DEPTH):
            mod, gpre, gpost = sublayer_params(li, n_prompt, n_sample)
            x = _ffn(x, mod[0], gpre[0], gpost[0], wg[li, 0], wu[li, 0], wd[li, 0])
            j = li // 2
            if li % 2 == 0:
                lam_init = _lam_init(li)
                qkv, ka, va, kb, vb = _inproj_ab(x, mod[1], gpre[1], w_in_ab16[j])
                subln = subln_b[j][None, :]
                out_a = _attn_pair_sample(
                    _attn_a_sample_kernel, "attn_a_sample", j, qkv, cache_a_k_t, cache_a_v_t,
                    [sbias[j]], [pl.BlockSpec((2, t_sample, SBIAS_W), lambda b, g: (g, 0, 0))])
                out_b = _attn_b_sample(j, qkv, cache_b_k_rows, cache_b_v_rows, lambda_b[j], subln,
                                       lam_init)
                a_k.append(ka)
                a_v.append(va)
                b_k.append(kb)
                b_v.append(vb)
                x = _outproj(x, mod[1], gpost[1], [out_a, out_b],
                             [w_out_ab16[j, :D_A], w_out_ab16[j, D_A:]])
            else:
                qkv, k, v, logf = _inproj_c(x, mod[1], gpre[1], w_qkv_c16[j], w_f16[j], b_f_row[j])
                past = clogf_t.shape[-1]
                out = _attn_pair_sample(
                    _attn_c_sample_kernel, "attn_c_sample", j, qkv, cache_c_k_t, cache_c_v_t,
                    [logf, clogf_t],
                    [pl.BlockSpec((1, t_sample, LANES), lambda b, g: (b, 0, 0)),
                     pl.BlockSpec((1, 1, H_C, past), lambda b, g, j=j: (j, b, 0, 0))],
                    scratch=[pltpu.VMEM((H_C, past), F32)])
                c_k.append(k)
                c_v.append(v)
                c_lf.append(logf[:, :, :H_C])
                x = _outproj(x, mod[1], gpost[1], [out], [w_out_c16[j]])
            x = _ffn(x, mod[2], gpre[2], gpost[2], wg[li, 1], wu[li, 1], wd[li, 1])
        nb, sb = x.shape[:2]
        return (x,
                jnp.stack(a_k).reshape(-1, nb, sb, H_A, HEAD_DIM),
                jnp.stack(a_v).reshape(-1, nb, sb, H_A, HEAD_DIM),
                jnp.stack(b_k).reshape(-1, nb, sb, H_B, 2 * HEAD_DIM),
                jnp.stack(b_v).reshape(-1, nb, sb, H_B, 2 * HEAD_DIM),
                jnp.stack(c_k).reshape(-1, nb, sb, H_C, HEAD_DIM),
                jnp.stack(c_v).reshape(-1, nb, sb, H_C, HEAD_DIM),
                jnp.stack(c_lf))

    p = run_prompt(x_prompt)
    s = run_sample(x_sample)
    return (p[0], s[0]) + tuple(p[1:]) + tuple(s[1:])
```

```python
import functools
import math

import jax
import jax.numpy as jnp
from jax import lax
from jax.experimental import pallas as pl
from jax.experimental.pallas import tpu as pltpu

D_MODEL = 1024
DEPTH = 4
CHUNK = 64
HEAD_DIM = 64
H_A = 8
H_B = 4
H_C = 16
D_A = H_A * HEAD_DIM
D_B = H_B * 2 * HEAD_DIM
D_C = H_C * HEAD_DIM
A_LEFT_CHUNKS = 8
A_PAST = A_LEFT_CHUNKS * CHUNK
REL_CLIP = 128
D_FF = 2816
N_SUB = 3
MACARON_W = 0.5
EPS = 1e-6
NEG_INF = -1e30
ATTN_SCALE = HEAD_DIM ** -0.5
LOG2E = 1.4426950408889634
QSCALE_LOG2 = ATTN_SCALE * LOG2E

LANES = 128
TOK_TILE = 256
SAMPLE_GROUP = 8
TP = 256
SUM_ROWS = 16
DIAG_TILES = 2
TQW = DIAG_TILES * TP
HP_A = 4
HP_B = 4
HP_C = 4
HP_SAMPLE = 4
BAND_TILES = 3
BAND_A = BAND_TILES * TP
SCAN_T = 512
TAB_PAD = 384
TOEP = 1024
SBIAS_W = 640
ADA_TN = 1536
VMEM_LIMIT = 56 * 1024 * 1024

BF16 = jnp.bfloat16
F32 = jnp.float32


def _lam_init(li):
    return 0.8 - 0.6 * math.exp(-0.3 * li)


def _cparams(n_axes, vmem=None):
    return pltpu.CompilerParams(dimension_semantics=("arbitrary",) * n_axes,
                                vmem_limit_bytes=vmem)


def _dot(a, b):
    return jnp.dot(a, b, preferred_element_type=F32)


def _dot_nt(a, b):
    return lax.dot_general(a, b, (((1,), (1,)), ((), ())), preferred_element_type=F32)


def _dot_exact(a, b):
    return jnp.dot(a, b, preferred_element_type=F32, precision=lax.Precision.HIGHEST)


def _dot_nt_exact(a, b):
    return lax.dot_general(a, b, (((1,), (1,)), ((), ())), preferred_element_type=F32,
                           precision=lax.Precision.HIGHEST)


def _split3(x):
    hi = x.astype(BF16).astype(F32)
    r1 = x - hi
    mid = r1.astype(BF16).astype(F32)
    lo = (r1 - mid).astype(BF16).astype(F32)
    return hi, mid, lo


def _norm_mod(x, m, g_pre):
    y = x * lax.rsqrt(jnp.mean(x * x, axis=-1, keepdims=True) + EPS)
    return (y * g_pre) * (1.0 + m[:, 1:2, :]) + m[:, 0:1, :]


def _gated_residual(x, m, g_post, out, res_w):
    y = out * lax.rsqrt(jnp.mean(out * out, axis=-1, keepdims=True) + EPS)
    return x + (res_w * (1.0 + m[:, 2:3, :])) * (y * g_post)


def _tok_layout(x):
    nb_total, sb_total, _ = x.shape
    if sb_total >= TOK_TILE:
        nb, sb = 1, TOK_TILE
    else:
        nb, sb = SAMPLE_GROUP, sb_total
    tiles = sb_total // sb
    grid = (nb_total // nb) * tiles

    def spec(width, rows_per_token=1):
        return pl.BlockSpec((nb, sb * rows_per_token, width), lambda i: (i // tiles, i % tiles, 0))

    def spec_t(width):
        return pl.BlockSpec((nb, width, sb), lambda i: (i // tiles, 0, i % tiles))

    mod_spec = pl.BlockSpec((nb, 3, D_MODEL), lambda i: (i // tiles, 0, 0))
    return nb, sb, grid, spec, spec_t, mod_spec


def _full(shape):
    return pl.BlockSpec(shape, lambda *_: (0,) * len(shape))


def _ada_kernel(c_ref, w_ref, b_ref, o_ref):
    c = c_ref[...]
    a = (c * jax.nn.sigmoid(c)).astype(BF16)
    o_ref[0] = _dot(a, w_ref[0].astype(BF16)) + b_ref[0]


def _ada(c_all, w_ada, b_ada):
    rows = c_all.shape[0]
    n_out = w_ada.shape[-1]
    return pl.pallas_call(
        _ada_kernel,
        grid=(DEPTH, n_out // ADA_TN),
        in_specs=[pl.BlockSpec((rows, D_MODEL), lambda l, n: (0, 0)),
                  pl.BlockSpec((1, D_MODEL, ADA_TN), lambda l, n: (l, 0, n)),
                  pl.BlockSpec((1, 1, ADA_TN), lambda l, n: (l, 0, n))],
        out_specs=pl.BlockSpec((1, rows, ADA_TN), lambda l, n: (l, 0, n)),
        out_shape=jax.ShapeDtypeStruct((DEPTH, rows, n_out), F32),
        compiler_params=_cparams(2, VMEM_LIMIT),
        name="ada",
    )(c_all, w_ada, b_ada.reshape(DEPTH, 1, n_out))


def _ffn_kernel(x_ref, mod_ref, gpre_ref, gpost_ref, wg_ref, wu_ref, wd_ref, o_ref):
    x = x_ref[...]
    m = mod_ref[...]
    nb, sb, d = x.shape
    h = _norm_mod(x, m, gpre_ref[...]).reshape(nb * sb, d).astype(BF16)
    g = _dot(h, wg_ref[...])
    u = _dot(h, wu_ref[...])
    a = ((g * jax.nn.sigmoid(g)) * u).astype(BF16)
    out = _dot(a, wd_ref[...]).reshape(nb, sb, d)
    o_ref[...] = _gated_residual(x, m, gpost_ref[...], out, MACARON_W)


def _ffn(x, mod, g_pre, g_post, wg, wu, wd):
    nb, sb, grid, spec, _, mod_spec = _tok_layout(x)
    return pl.pallas_call(
        _ffn_kernel,
        grid=(grid,),
        in_specs=[spec(D_MODEL), mod_spec, _full((1, D_MODEL)), _full((1, D_MODEL)),
                  _full((D_MODEL, D_FF)), _full((D_MODEL, D_FF)), _full((D_FF, D_MODEL))],
        out_specs=spec(D_MODEL),
        out_shape=jax.ShapeDtypeStruct(x.shape, F32),
        compiler_params=_cparams(1, VMEM_LIMIT),
        name="ffn",
    )(x, mod, g_pre, g_post, wg, wu, wd)


def _inproj_ab_kernel(x_ref, mod_ref, gpre_ref, w_ref, qkv_ref, ka_ref, va_ref, kb_ref, vb_ref):
    x = x_ref[...]
    nb, sb, d = x.shape
    h = _norm_mod(x, mod_ref[...], gpre_ref[...]).reshape(nb * sb, d).astype(BF16)
    f32_outs = {1: ka_ref, 2: va_ref, 4: kb_ref, 5: vb_ref}
    for c in range(6):
        cols = slice(c * D_A, (c + 1) * D_A)
        p = _dot(h, w_ref[:, cols]).reshape(nb, sb, D_A)
        if c in f32_outs:
            f32_outs[c][...] = p
            qkv_ref[:, :, cols] = p.astype(BF16)
        else:
            qkv_ref[:, :, cols] = (p * ATTN_SCALE).astype(BF16)


def _inproj_ab(x, mod, g_pre, w_in):
    nb, sb, grid, spec, _, mod_spec = _tok_layout(x)
    nbt, sbt, _ = x.shape
    f32_out = jax.ShapeDtypeStruct((nbt, sbt, D_A), F32)
    return pl.pallas_call(
        _inproj_ab_kernel,
        grid=(grid,),
        in_specs=[spec(D_MODEL), mod_spec, _full((1, D_MODEL)), _full((D_MODEL, 6 * D_A))],
        out_specs=[spec(6 * D_A)] + [spec(D_A)] * 4,
        out_shape=[jax.ShapeDtypeStruct((nbt, sbt, 6 * D_A), BF16)] + [f32_out] * 4,
        compiler_params=_cparams(1, VMEM_LIMIT),
        name="inproj_ab",
    )(x, mod, g_pre, w_in)


def _log_sigmoid(z):
    return jnp.minimum(z, 0.0) - jnp.log1p(jnp.exp(-jnp.abs(z)))


def _inproj_c_kernel(x_ref, mod_ref, gpre_ref, w_ref, wf_ref, bf_ref,
                     qkv_ref, k_ref, v_ref, logf_ref):
    x = x_ref[...]
    nb, sb, d = x.shape
    h = _norm_mod(x, mod_ref[...], gpre_ref[...]).reshape(nb * sb, d).astype(BF16)
    f32_outs = {1: k_ref, 2: v_ref}
    for c in range(3):
        cols = slice(c * D_C, (c + 1) * D_C)
        p = _dot(h, w_ref[:, cols]).reshape(nb, sb, D_C)
        if c in f32_outs:
            f32_outs[c][...] = p
            qkv_ref[:, :, cols] = p.astype(BF16)
        else:
            qkv_ref[:, :, cols] = (p * ATTN_SCALE).astype(BF16)
    logf = _log_sigmoid(_dot(h, wf_ref[...]) + bf_ref[...])
    logf_ref[...] = logf.reshape(nb, sb, LANES)


def _inproj_c(x, mod, g_pre, w_qkv, w_f, b_f):
    nb, sb, grid, spec, _, mod_spec = _tok_layout(x)
    nbt, sbt, _ = x.shape
    f32_out = jax.ShapeDtypeStruct((nbt, sbt, D_C), F32)
    return pl.pallas_call(
        _inproj_c_kernel,
        grid=(grid,),
        in_specs=[spec(D_MODEL), mod_spec, _full((1, D_MODEL)), _full((D_MODEL, 3 * D_C)),
                  _full((D_MODEL, LANES)), _full((1, LANES))],
        out_specs=[spec(3 * D_C), spec(D_C), spec(D_C), spec(LANES)],
        out_shape=[jax.ShapeDtypeStruct((nbt, sbt, 3 * D_C), BF16), f32_out, f32_out,
                   jax.ShapeDtypeStruct((nbt, sbt, LANES), F32)],
        compiler_params=_cparams(1, VMEM_LIMIT),
        name="inproj_c",
    )(x, mod, g_pre, w_qkv, w_f, b_f)


def _outproj_kernel(*refs, n_in):
    x_ref, mod_ref, gpost_ref = refs[:3]
    a_refs = refs[3:3 + n_in]
    w_refs = refs[3 + n_in:3 + 2 * n_in]
    o_ref = refs[-1]
    x = x_ref[...]
    nb, sb, d = x.shape
    out = None
    for a_ref, w_ref in zip(a_refs, w_refs):
        a = a_ref[...]
        t = _dot(a.reshape(nb * sb, a.shape[-1]), w_ref[...])
        out = t if out is None else out + t
    o_ref[...] = _gated_residual(x, mod_ref[...], gpost_ref[...], out.reshape(nb, sb, d), 1.0)


def _outproj(x, mod, g_post, acts, weights):
    nb, sb, grid, spec, _, mod_spec = _tok_layout(x)
    n_in = len(acts)
    return pl.pallas_call(
        functools.partial(_outproj_kernel, n_in=n_in),
        grid=(grid,),
        in_specs=([spec(D_MODEL), mod_spec, _full((1, D_MODEL))]
                  + [spec(a.shape[-1]) for a in acts]
                  + [_full(w.shape) for w in weights]),
        out_specs=spec(D_MODEL),
        out_shape=jax.ShapeDtypeStruct(x.shape, F32),
        compiler_params=_cparams(1, VMEM_LIMIT),
        name="outproj",
    )(x, mod, g_post, *acts, *weights)


def _inproj_ab_prompt_kernel(x_ref, mod_ref, gpre_ref, w_ref, wt_ref,
                             qt_ref, k_ref, vt_ref, kat_ref, vat_ref, kb_ref, vb_ref):
    x = x_ref[...]
    _, sb, d = x.shape
    h = _norm_mod(x, mod_ref[...], gpre_ref[...]).reshape(sb, d).astype(BF16)

    def nat(c):
        return _dot(h, w_ref[:, c * D_A:(c + 1) * D_A])

    def tr(c):
        return _dot_nt(wt_ref[c * D_A:(c + 1) * D_A, :], h)

    qt_ref[0, :D_A, :] = (tr(0) * QSCALE_LOG2).astype(BF16)
    qt_ref[0, D_A:, :] = (tr(3) * QSCALE_LOG2).astype(BF16)
    kat_ref[0] = tr(1)
    k_ref[0, :, :D_A] = nat(1).astype(BF16)
    kb = nat(4)
    k_ref[0, :, D_A:] = kb.astype(BF16)
    va_t = tr(2)
    vat_ref[0] = va_t
    vt_ref[0, :D_A, :] = va_t.astype(BF16)
    vt_ref[0, D_A:, :] = tr(5).astype(BF16)
    vb = nat(5)
    for hd in range(H_B):
        kb_ref[0, pl.ds(hd, sb, stride=H_B), :] = kb[:, hd * LANES:(hd + 1) * LANES]
        vb_ref[0, pl.ds(hd, sb, stride=H_B), :] = vb[:, hd * LANES:(hd + 1) * LANES]


def _inproj_ab_prompt(x, mod, g_pre, w_in, w_in_t):
    nb, sb, grid, spec, spec_t, mod_spec = _tok_layout(x)
    b, s, _ = x.shape
    width = D_A + D_B
    return pl.pallas_call(
        _inproj_ab_prompt_kernel,
        grid=(grid,),
        in_specs=[spec(D_MODEL), mod_spec, _full((1, D_MODEL)), _full(w_in.shape), _full(w_in_t.shape)],
        out_specs=[spec_t(width), spec(width), spec_t(width), spec_t(D_A), spec_t(D_A),
                   spec(LANES, H_B), spec(LANES, H_B)],
        out_shape=[jax.ShapeDtypeStruct((b, width, s), BF16),
                   jax.ShapeDtypeStruct((b, s, width), BF16),
                   jax.ShapeDtypeStruct((b, width, s), BF16),
                   jax.ShapeDtypeStruct((b, D_A, s), F32),
                   jax.ShapeDtypeStruct((b, D_A, s), F32),
                   jax.ShapeDtypeStruct((b, s * H_B, LANES), F32),
                   jax.ShapeDtypeStruct((b, s * H_B, LANES), F32)],
        compiler_params=_cparams(1, VMEM_LIMIT),
        name="inproj_ab_prompt",
    )(x, mod, g_pre, w_in, w_in_t)


def _inproj_c_prompt_kernel(x_ref, mod_ref, gpre_ref, wk_ref, wt_ref, wf_ref, wft_ref, bf_ref, bft_ref,
                            qt_ref, k_ref, vt_ref, kt32_ref, vt32_ref, logf_ref, logft_ref):
    x = x_ref[...]
    _, sb, d = x.shape
    h = _norm_mod(x, mod_ref[...], gpre_ref[...]).reshape(sb, d).astype(BF16)
    qt_ref[0] = (_dot_nt(wt_ref[:D_C, :], h) * QSCALE_LOG2).astype(BF16)
    k_ref[0] = _dot(h, wk_ref[...]).astype(BF16)
    kt32_ref[0] = _dot_nt(wt_ref[D_C:2 * D_C, :], h)
    v_t = _dot_nt(wt_ref[2 * D_C:, :], h)
    vt32_ref[0] = v_t
    vt_ref[0] = v_t.astype(BF16)
    logf_ref[0] = _log_sigmoid(_dot(h, wf_ref[...]) + bf_ref[...])
    logft_ref[0] = _log_sigmoid(_dot_nt(wft_ref[...], h) + bft_ref[...])[:H_C, :]


def _inproj_c_prompt(x, mod, g_pre, w_k, w_qkv_t, w_f, w_f_t, b_f, b_f_t):
    nb, sb, grid, spec, spec_t, mod_spec = _tok_layout(x)
    b, s, _ = x.shape
    return pl.pallas_call(
        _inproj_c_prompt_kernel,
        grid=(grid,),
        in_specs=[spec(D_MODEL), mod_spec, _full((1, D_MODEL)), _full(w_k.shape), _full(w_qkv_t.shape),
                  _full(w_f.shape), _full(w_f_t.shape), _full(b_f.shape), _full(b_f_t.shape)],
        out_specs=[spec_t(D_C), spec(D_C), spec_t(D_C), spec_t(D_C), spec_t(D_C), spec(LANES),
                   spec_t(H_C)],
        out_shape=[jax.ShapeDtypeStruct((b, D_C, s), BF16),
                   jax.ShapeDtypeStruct((b, s, D_C), BF16),
                   jax.ShapeDtypeStruct((b, D_C, s), BF16),
                   jax.ShapeDtypeStruct((b, D_C, s), F32),
                   jax.ShapeDtypeStruct((b, D_C, s), F32),
                   jax.ShapeDtypeStruct((b, s, LANES), F32),
                   jax.ShapeDtypeStruct((b, H_C, s), F32)],
        compiler_params=_cparams(1, VMEM_LIMIT),
        name="inproj_c_prompt",
    )(x, mod, g_pre, w_k, w_qkv_t, w_f, w_f_t, b_f, b_f_t)


def _outproj_prompt_kernel(*refs, n_in):
    x_ref, mod_ref, gpost_ref = refs[:3]
    a_refs = refs[3:3 + n_in]
    wt_refs = refs[3 + n_in:3 + 2 * n_in]
    o_ref = refs[-1]
    x = x_ref[...]
    out_t = None
    for a_ref, wt_ref in zip(a_refs, wt_refs):
        t = _dot(wt_ref[...], a_ref[0])
        out_t = t if out_t is None else out_t + t
    out = out_t.T.reshape(x.shape)
    o_ref[...] = _gated_residual(x, mod_ref[...], gpost_ref[...], out, 1.0)


def _outproj_prompt(x, mod, g_post, acts_t, weights_t):
    nb, sb, grid, spec, spec_t, mod_spec = _tok_layout(x)
    n_in = len(acts_t)
    return pl.pallas_call(
        functools.partial(_outproj_prompt_kernel, n_in=n_in),
        grid=(grid,),
        in_specs=([spec(D_MODEL), mod_spec, _full((1, D_MODEL))]
                  + [spec_t(a.shape[1]) for a in acts_t]
                  + [_full(w.shape) for w in weights_t]),
        out_specs=spec(D_MODEL),
        out_shape=jax.ShapeDtypeStruct(x.shape, F32),
        compiler_params=_cparams(1, VMEM_LIMIT),
        name="outproj_prompt",
    )(x, mod, g_post, *acts_t, *weights_t)


def _diff_lambda(lam_ref, lam_init):
    lv = lam_ref[...]
    a = jnp.sum(lv[0:1] * lv[1:2], axis=1, keepdims=True)
    b = jnp.sum(lv[2:3] * lv[3:4], axis=1, keepdims=True)
    return jnp.exp(a) - jnp.exp(b) + lam_init


def _alibi_slope(g):
    slope = jnp.float32(0.0)
    for h in range(H_B):
        slope = jnp.where(g == h, jnp.float32(2.0 ** (-8.0 * (h + 1) / H_B)), slope)
    return slope


def _select_lane(a, idx):
    lane = lax.broadcasted_iota(jnp.int32, a.shape, 1)
    return jnp.sum(jnp.where(lane == idx, a, 0.0), axis=1, keepdims=True)


def _select_row(a, idx):
    row = lax.broadcasted_iota(jnp.int32, a.shape, 0)
    return jnp.sum(jnp.where(row == idx, a, 0.0), axis=0, keepdims=True)


def _relpos_bias_kernel(tab_ref, pbias_ref, sbias_ref):
    t = tab_ref[0]
    t_hi = t.astype(BF16)
    r1 = t - t_hi.astype(F32)
    t_mid = r1.astype(BF16)
    t_lo = (r1 - t_mid.astype(F32)).astype(BF16)
    ent = lax.broadcasted_iota(jnp.int32, (TAB_PAD, TOEP), 0)
    n = lax.broadcasted_iota(jnp.int32, (TAB_PAD, TOEP), 1)

    def lookup(rel_pos):
        onehot = jnp.where(ent == jnp.clip(rel_pos, -REL_CLIP, REL_CLIP) + REL_CLIP, 1.0, 0.0)
        onehot = onehot.astype(BF16)
        return _dot(t_hi, onehot) + _dot(t_mid, onehot) + _dot(t_lo, onehot)

    gen_t = lookup(A_PAST + jnp.where(n < TP, n, n - TOEP)) * LOG2E
    d = (lax.broadcasted_iota(jnp.int32, (BAND_A, TP), 0) // CHUNK
         - lax.broadcasted_iota(jnp.int32, (BAND_A, TP), 1) // CHUNK)
    valid = (d >= 0) & (d <= A_LEFT_CHUNKS)
    gen = lookup(A_PAST + TP - 1 - n)
    t_sample = sbias_ref.shape[2]
    for h in range(H_A):
        rows_t = jnp.broadcast_to(gen_t[h:h + 1, :], (BAND_A, TOEP))
        toep_t = pltpu.roll(rows_t, 0, 1, stride=1, stride_axis=0)
        pbias_ref[0, h] = jnp.where(valid, toep_t[:, :TP], NEG_INF)
        rows = jnp.broadcast_to(gen[h:h + 1, :], (t_sample, TOEP))
        toep = pltpu.roll(rows, TOEP - TP + 1, 1, stride=1, stride_axis=0)
        sbias_ref[0, h] = toep[:, :SBIAS_W]


def _relpos_bias(relpos_a, sample_len):
    ne = relpos_a.shape[0]
    tab = jnp.pad(relpos_a, ((0, 0), (0, 0), (0, TAB_PAD - relpos_a.shape[-1])))
    return pl.pallas_call(
        _relpos_bias_kernel,
        grid=(ne,),
        in_specs=[pl.BlockSpec((1, H_A, TAB_PAD), lambda l: (l, 0, 0))],
        out_specs=[pl.BlockSpec((1, H_A, BAND_A, TP), lambda l: (l, 0, 0, 0)),
                   pl.BlockSpec((1, H_A, sample_len, SBIAS_W), lambda l: (l, 0, 0, 0))],
        out_shape=[jax.ShapeDtypeStruct((ne, H_A, BAND_A, TP), F32),
                   jax.ShapeDtypeStruct((ne, H_A, sample_len, SBIAS_W), F32)],
        compiler_params=_cparams(1, VMEM_LIMIT),
        name="relpos_bias",
    )(tab)


def _stats_init(m_ref, acc_ref):
    m_ref[...] = jnp.full(m_ref.shape, NEG_INF, F32)
    acc_ref[...] = jnp.zeros(acc_ref.shape, F32)


def _with_ones(v_t):
    return jnp.concatenate([v_t, jnp.ones((SUM_ROWS, v_t.shape[1]), v_t.dtype)], axis=0)


def _stats_update_all(m_ref, acc_ref, scores, values):
    n = len(scores)
    m_prev = [m_ref[i] for i in range(n)]
    m_new = [jnp.maximum(m_prev[i], jnp.max(scores[i], axis=0, keepdims=True)) for i in range(n)]
    alpha = [jnp.exp2(m_prev[i] - m_new[i]) for i in range(n)]
    p = [jnp.exp2(scores[i] - m_new[i]).astype(BF16) for i in range(n)]
    pv = [_dot(values[i], p[i]) for i in range(n)]
    for i in range(n):
        acc_ref[i] = alpha[i] * acc_ref[i] + pv[i]
        m_ref[i] = m_new[i]


def _stats_result(acc_ref, mi):
    acc = acc_ref[mi]
    d = acc.shape[0] - SUM_ROWS
    return acc[:d] * (1.0 / acc[d:d + 1])


def _stats_scratch(n, d, tq):
    return [pltpu.VMEM((n, 1, tq), F32), pltpu.VMEM((n, d + SUM_ROWS, tq), F32)]


def _map_rows(q_t, mi):
    row = lax.broadcasted_iota(jnp.int32, q_t.shape, 0)
    keep = (row >= mi * HEAD_DIM) & (row < (mi + 1) * HEAD_DIM)
    return jnp.where(keep, q_t, jnp.zeros_like(q_t))


def _attn_a_prompt_kernel(qt_ref, k_ref, vt_ref, bias_ref, o_ref, m_ref, acc_ref, *, hp):
    qi = pl.program_id(2)
    qm = [_map_rows(qt_ref[0, p * LANES:(p + 1) * LANES, :], mi) for p in range(hp) for mi in range(2)]
    _stats_init(m_ref, acc_ref)
    for j in range(BAND_TILES):
        kt = qi - (BAND_TILES - 1) + j

        def tile(j=j, kt=kt):
            start = pl.multiple_of(kt * TP, TP)
            scores, values = [], []
            for p in range(hp):
                cols = slice(p * LANES, (p + 1) * LANES)
                k = k_ref[0, pl.ds(start, TP), cols]
                v_t = vt_ref[0, cols, pl.ds(start, TP)]
                for mi in range(2):
                    scores.append(_dot(k, qm[2 * p + mi]) + bias_ref[2 * p + mi, j * TP:(j + 1) * TP, :])
                    values.append(_with_ones(v_t[mi * HEAD_DIM:(mi + 1) * HEAD_DIM]))
            _stats_update_all(m_ref, acc_ref, scores, values)

        if j == BAND_TILES - 1:
            tile()
        else:
            pl.when(kt >= 0)(tile)
    o = jnp.concatenate([_stats_result(acc_ref, i) for i in range(2 * hp)], axis=0)
    o_ref[0] = o.astype(o_ref.dtype)


def _attn_a_prompt(q_t, k, v_t, pbias):
    b, s, _ = k.shape
    hp = HP_A
    groups = D_A // (hp * LANES)
    return pl.pallas_call(
        functools.partial(_attn_a_prompt_kernel, hp=hp),
        grid=(b, groups, s // TP),
        in_specs=[pl.BlockSpec((1, hp * LANES, TP), lambda b, g, i: (b, g, i)),
                  pl.BlockSpec((1, s, hp * LANES), lambda b, g, i: (b, 0, g)),
                  pl.BlockSpec((1, hp * LANES, s), lambda b, g, i: (b, g, 0)),
                  pl.BlockSpec((2 * hp, BAND_A, TP), lambda b, g, i: (g, 0, 0))],
        out_specs=pl.BlockSpec((1, hp * LANES, TP), lambda b, g, i: (b, g, i)),
        out_shape=jax.ShapeDtypeStruct((b, D_A, s), BF16),
        scratch_shapes=_stats_scratch(2 * hp, HEAD_DIM, TP),
        compiler_params=_cparams(3, VMEM_LIMIT),
        name="attn_a_prompt",
    )(q_t, k, v_t, pbias)


def _attn_b_prompt_kernel(qt_ref, k_ref, vt_ref, ke_ref, lam_ref, subln_ref, o_ref,
                          m_ref, acc_ref, diag_ref, *, lam_init, hp):
    g = pl.program_id(1)
    qi = pl.program_id(2)
    coefs = [_alibi_slope(g * hp + p) * LOG2E for p in range(hp)]

    @pl.when(qi == 0)
    def _():
        for d in range(DIAG_TILES):
            kr = lax.broadcasted_iota(jnp.int32, (TP, TQW), 0) + d * TP
            qc = lax.broadcasted_iota(jnp.int32, (TP, TQW), 1)
            dist = jnp.abs(qc - kr).astype(F32)
            for p in range(hp):
                diag_ref[p, d] = jnp.where(kr // CHUNK <= qc // CHUNK, -coefs[p] * dist, NEG_INF)

    row = lax.broadcasted_iota(jnp.int32, (LANES, TQW), 0)
    qpos = (qi * TQW + lax.broadcasted_iota(jnp.int32, (LANES, TQW), 1)).astype(F32)
    part = row % 3
    qa_past, qa_diag = [], []
    for p in range(hp):
        c = coefs[p]
        q_t = qt_ref[0, p * LANES:(p + 1) * LANES, :]
        hi, mid, lo = _split3(jnp.where(row < 3, CHUNK * c, jnp.where(row < 6, c, -c * qpos)))
        qe = jnp.where(row < 9, jnp.where(part == 0, hi, jnp.where(part == 1, mid, lo)), 0.0).astype(BF16)
        for mi in range(2):
            qm = _map_rows(q_t, mi)
            qa_past.append(jnp.concatenate([qm, qe], axis=0))
            qa_diag.append(jnp.concatenate([qm, jnp.zeros_like(qe)], axis=0))
    _stats_init(m_ref, acc_ref)

    def tile(j, qa, diag):
        start = pl.multiple_of(j * TP, TP)
        ke = ke_ref[pl.ds(start, TP), :]
        scores, values = [], []
        for p in range(hp):
            cols = slice(p * LANES, (p + 1) * LANES)
            kx = jnp.concatenate([k_ref[0, pl.ds(start, TP), cols], ke], axis=1)
            v_t = _with_ones(vt_ref[0, cols, pl.ds(start, TP)])
            for mi in range(2):
                s_t = _dot(kx, qa[2 * p + mi])
                scores.append(s_t if diag is None else s_t + diag_ref[p, diag])
                values.append(v_t)
        _stats_update_all(m_ref, acc_ref, scores, values)

    def past_tile(j, carry):
        tile(j, qa_past, None)
        return carry

    lax.fori_loop(0, DIAG_TILES * qi, past_tile, 0)
    for d in range(DIAG_TILES):
        tile(DIAG_TILES * qi + d, qa_diag, d)
    lam = _diff_lambda(lam_ref, lam_init)
    outs = []
    for p in range(hp):
        o = _stats_result(acc_ref, 2 * p) - lam * _stats_result(acc_ref, 2 * p + 1)
        y = o * lax.rsqrt(jnp.mean(o * o, axis=0, keepdims=True) + EPS)
        outs.append((y * subln_ref[...]) * (1.0 - lam_init))
    o_ref[0] = jnp.concatenate(outs, axis=0).astype(o_ref.dtype)


def _attn_b_prompt(q_t, k, v_t, lam_vec, subln_col, lam_init):
    b, s, _ = k.shape
    hp = HP_B
    groups = D_B // (hp * LANES)
    blk0 = D_A // (hp * LANES)
    pos = jnp.arange(s, dtype=jnp.int32)[:, None]
    lane = jnp.arange(LANES, dtype=jnp.int32)[None, :]
    key_extras = jnp.where(lane < 3, pos // CHUNK,
                           jnp.where(lane < 6, pos % CHUNK, jnp.where(lane < 9, 1, 0))).astype(BF16)
    return pl.pallas_call(
        functools.partial(_attn_b_prompt_kernel, lam_init=lam_init, hp=hp),
        grid=(b, groups, s // TQW),
        in_specs=[pl.BlockSpec((1, hp * LANES, TQW), lambda b, g, i: (b, blk0 + g, i)),
                  pl.BlockSpec((1, s, hp * LANES), lambda b, g, i: (b, 0, blk0 + g)),
                  pl.BlockSpec((1, hp * LANES, s), lambda b, g, i: (b, blk0 + g, 0)),
                  _full(key_extras.shape), _full(lam_vec.shape), _full(subln_col.shape)],
        out_specs=pl.BlockSpec((1, hp * LANES, TQW), lambda b, g, i: (b, g, i)),
        out_shape=jax.ShapeDtypeStruct((b, D_B, s), BF16),
        scratch_shapes=(_stats_scratch(2 * hp, 2 * HEAD_DIM, TQW)
                        + [pltpu.VMEM((hp, DIAG_TILES, TP, TQW), F32)]),
        compiler_params=_cparams(3, VMEM_LIMIT),
        name="attn_b_prompt",
    )(q_t, k, v_t, key_extras, lam_vec, subln_col)


def _logf_scan_kernel(logf_ref, dqt_ref, e_ref):
    r = lax.broadcasted_iota(jnp.int32, (SCAN_T, SCAN_T), 0)
    c = lax.broadcasted_iota(jnp.int32, (SCAN_T, SCAN_T), 1)
    tri = jnp.where(r >= c, 1.0, 0.0).astype(F32)
    i_sel = lax.broadcasted_iota(jnp.int32, (3 * LANES, LANES), 0)
    j_sel = lax.broadcasted_iota(jnp.int32, (3 * LANES, LANES), 1)
    in_dk = (j_sel >= 3) & (j_sel < 9)
    sels = []
    for g in range(H_C // 2):
        src = ((j_sel - 3) % 3) * LANES + 2 * g + (j_sel - 3) // 3
        sels.append(jnp.where(in_dk & (i_sel == src), 1.0, 0.0).astype(BF16))
    ones_cols = jnp.where(lax.broadcasted_iota(jnp.int32, (1, LANES), 1) < 3, 1.0, 0.0)
    carry = jnp.zeros((1, LANES), F32)
    for t in range(logf_ref.shape[1] // SCAN_T):
        rows = slice(t * SCAN_T, (t + 1) * SCAN_T)
        cs = _dot_exact(tri, logf_ref[0, rows, :]) + carry
        carry = cs[SCAN_T - 1:SCAN_T, :]
        x = cs * LOG2E
        dqt_ref[0, :, rows] = x.T[:H_C, :]
        hi, mid, lo = _split3(x)
        parts = jnp.concatenate([hi.astype(BF16), mid.astype(BF16), lo.astype(BF16)], axis=1)
        for g in range(H_C // 2):
            e_ref[0, g, rows, :] = (_dot(parts, sels[g]) + ones_cols).astype(BF16)


def _logf_scan(logf):
    b, s, _ = logf.shape
    return pl.pallas_call(
        _logf_scan_kernel,
        grid=(b,),
        in_specs=[pl.BlockSpec((1, s, LANES), lambda b: (b, 0, 0))],
        out_specs=[pl.BlockSpec((1, H_C, s), lambda b: (b, 0, 0)),
                   pl.BlockSpec((1, H_C // 2, s, LANES), lambda b: (b, 0, 0, 0))],
        out_shape=[jax.ShapeDtypeStruct((b, H_C, s), F32),
                   jax.ShapeDtypeStruct((b, H_C // 2, s, LANES), BF16)],
        compiler_params=_cparams(1, VMEM_LIMIT),
        name="logf_scan",
    )(logf)


def _attn_c_prompt_kernel(qt_ref, k_ref, vt_ref, dqt_ref, e_ref, o_ref, m_ref, acc_ref, *, hp):
    qi = pl.program_id(2)
    row = lax.broadcasted_iota(jnp.int32, (LANES, TQW), 0)
    qa = []
    for p in range(hp):
        q_t = qt_ref[0, p * LANES:(p + 1) * LANES, :]
        dq = dqt_ref[0, p]
        for mi in range(2):
            hi, mid, lo = _split3(dq[mi:mi + 1, :])
            dk_rows = (row >= 3 + 3 * mi) & (row < 6 + 3 * mi)
            qe = jnp.where(row == 0, hi, jnp.where(row == 1, mid, jnp.where(row == 2, lo,
                           jnp.where(dk_rows, -1.0, 0.0))))
            qa.append(jnp.concatenate([_map_rows(q_t, mi), qe.astype(BF16)], axis=0))
    _stats_init(m_ref, acc_ref)

    def tile(j, diag):
        start = pl.multiple_of(j * TP, TP)
        scores, values = [], []
        for p in range(hp):
            cols = slice(p * LANES, (p + 1) * LANES)
            kx = jnp.concatenate([k_ref[0, pl.ds(start, TP), cols], e_ref[0, p, pl.ds(start, TP), :]],
                                 axis=1)
            v_t = vt_ref[0, cols, pl.ds(start, TP)]
            for mi in range(2):
                s_t = _dot(kx, qa[2 * p + mi])
                if diag is not None:
                    kr = lax.broadcasted_iota(jnp.int32, (TP, TQW), 0) + diag * TP
                    qc = lax.broadcasted_iota(jnp.int32, (TP, TQW), 1)
                    s_t = jnp.where(kr <= qc, s_t, NEG_INF)
                scores.append(s_t)
                values.append(_with_ones(v_t[mi * HEAD_DIM:(mi + 1) * HEAD_DIM]))
        _stats_update_all(m_ref, acc_ref, scores, values)

    def past_tile(j, carry):
        tile(j, None)
        return carry

    lax.fori_loop(0, DIAG_TILES * qi, past_tile, 0)
    for d in range(DIAG_TILES):
        tile(DIAG_TILES * qi + d, d)
    o = jnp.concatenate([_stats_result(acc_ref, i) for i in range(2 * hp)], axis=0)
    o_ref[0] = o.astype(o_ref.dtype)


def _attn_c_prompt(q_t, k, v_t, dq_t, key_extras):
    b, s, _ = k.shape
    hp = HP_C
    groups = D_C // (hp * LANES)
    dq_t = dq_t.reshape(b, D_C // LANES, 2, s)
    return pl.pallas_call(
        functools.partial(_attn_c_prompt_kernel, hp=hp),
        grid=(b, groups, s // TQW),
        in_specs=[pl.BlockSpec((1, hp * LANES, TQW), lambda b, g, i: (b, g, i)),
                  pl.BlockSpec((1, s, hp * LANES), lambda b, g, i: (b, 0, g)),
                  pl.BlockSpec((1, hp * LANES, s), lambda b, g, i: (b, g, 0)),
                  pl.BlockSpec((1, hp, 2, TQW), lambda b, g, i: (b, g, 0, i)),
                  pl.BlockSpec((1, hp, s, LANES), lambda b, g, i: (b, g, 0, 0))],
        out_specs=pl.BlockSpec((1, hp * LANES, TQW), lambda b, g, i: (b, g, i)),
        out_shape=jax.ShapeDtypeStruct((b, D_C, s), BF16),
        scratch_shapes=_stats_scratch(2 * hp, HEAD_DIM, TQW),
        compiler_params=_cparams(3, VMEM_LIMIT),
        name="attn_c_prompt",
    )(q_t, k, v_t, dq_t, key_extras)


def _split_maps(q):
    lane = lax.broadcasted_iota(jnp.int32, q.shape, q.ndim - 1)
    zero = jnp.zeros_like(q)
    return jnp.where(lane < HEAD_DIM, q, zero), jnp.where(lane >= HEAD_DIM, q, zero)


def _pair_heads(o0, o1):
    lane = lax.broadcasted_iota(jnp.int32, o0.shape, 1)
    return jnp.where(lane < HEAD_DIM, o0, o1)


def _diff_out(o0, o1, lam, subln, lam_init):
    o = o0 - lam * o1
    y = o * lax.rsqrt(jnp.mean(o * o, axis=-1, keepdims=True) + EPS)
    return (y * subln) * (1.0 - lam_init)


def _sample_scores(q, kn, kc, cache_t):
    q0, q1 = _split_maps(q)
    qs = jnp.concatenate([q0, q1], axis=0)
    s_c = _dot(qs, kc) if cache_t else _dot_nt(qs, kc)
    return s_c, _dot_nt(qs, kn)


def _sample_softmax_pv(s_c, s_n, vn, vc, cache_t):
    n = len(s_c)
    m = [jnp.maximum(jnp.max(s_c[i], axis=-1, keepdims=True), jnp.max(s_n[i], axis=-1, keepdims=True))
         for i in range(n)]
    p_c = [jnp.exp(s_c[i] - m[i]) for i in range(n)]
    p_n = [jnp.exp(s_n[i] - m[i]) for i in range(n)]
    l = [jnp.sum(p_c[i], axis=-1, keepdims=True) + jnp.sum(p_n[i], axis=-1, keepdims=True)
         for i in range(n)]
    outs = []
    for i in range(n):
        pc = p_c[i].astype(BF16)
        o_c = _dot_nt(pc, vc[i]) if cache_t else _dot(pc, vc[i])
        o = (o_c + _dot(p_n[i].astype(BF16), vn[i])) * (1.0 / l[i])
        t = o.shape[0] // 2
        outs.append((o[:t], o[t:]))
    return outs


def _cache_pair_t(ref, p):
    blk = ref[0, 0, 2 * p:2 * p + 2]
    return blk.reshape(2 * HEAD_DIM, blk.shape[-1]).astype(BF16)


def _attn_a_sample_kernel(q_ref, kn_ref, vn_ref, kc_ref, vc_ref, bias_ref, o_ref, *, hp):
    t = q_ref.shape[1]
    w = kc_ref.shape[-1]
    s_c, s_n, vn, vc = [], [], [], []
    for p in range(hp):
        cols = slice(p * LANES, (p + 1) * LANES)
        sc, sn = _sample_scores(q_ref[0, :, cols], kn_ref[0, :, cols], _cache_pair_t(kc_ref, p), True)
        bias = bias_ref[2 * p:2 * p + 2].reshape(2 * t, SBIAS_W)
        s_c.append(sc + bias[:, :w])
        s_n.append(sn + bias[:, w:w + t])
        vn.append(vn_ref[0, :, cols])
        vc.append(_cache_pair_t(vc_ref, p))
    outs = _sample_softmax_pv(s_c, s_n, vn, vc, True)
    for p in range(hp):
        o_ref[0, :, p * LANES:(p + 1) * LANES] = _pair_heads(*outs[p]).astype(o_ref.dtype)


def _attn_c_sample_kernel(q_ref, kn_ref, vn_ref, kc_ref, vc_ref, logf_ref, clogf_ref, o_ref,
                          dpast_ref, *, hp):
    g = pl.program_id(1)
    t = q_ref.shape[1]
    past = clogf_ref.shape[-1]

    @pl.when(g == 0)
    def _():
        r = lax.broadcasted_iota(jnp.int32, (SCAN_T, SCAN_T), 0)
        c = lax.broadcasted_iota(jnp.int32, (SCAN_T, SCAN_T), 1)
        triu = jnp.where(r <= c, 1.0, 0.0).astype(F32)
        carry = jnp.zeros((H_C, 1), F32)
        for i in range(past // SCAN_T):
            cols = slice(i * SCAN_T, (i + 1) * SCAN_T)
            cs = _dot_exact(clogf_ref[0, 0, :, cols], triu) + carry
            dpast_ref[:, cols] = cs
            carry = cs[:, SCAN_T - 1:SCAN_T]
        dpast_ref[...] = dpast_ref[...] - carry

    r = lax.broadcasted_iota(jnp.int32, (t, t), 0)
    c = lax.broadcasted_iota(jnp.int32, (t, t), 1)
    tri = jnp.where(r >= c, 1.0, 0.0).astype(F32)
    dq = _dot_exact(tri, logf_ref[0])
    er = lax.broadcasted_iota(jnp.int32, (LANES, LANES), 0)
    ec = lax.broadcasted_iota(jnp.int32, (LANES, LANES), 1)
    eye = jnp.where(er == ec, 1.0, 0.0).astype(F32)
    dq_t = _dot_nt_exact(eye, dq)
    dpast = dpast_ref[...]
    top = lax.broadcasted_iota(jnp.int32, (2 * t, 1), 0) < t
    qrow = lax.broadcasted_iota(jnp.int32, (2 * t, t), 0) % t
    kcol = lax.broadcasted_iota(jnp.int32, (2 * t, t), 1)
    s_c, s_n, vn, vc = [], [], [], []
    for p in range(hp):
        cols = slice(p * LANES, (p + 1) * LANES)
        h0 = 2 * (g * hp + p)
        dq_col = jnp.concatenate([_select_lane(dq, h0), _select_lane(dq, h0 + 1)], axis=0)
        dk_c = jnp.where(top, _select_row(dpast, h0), _select_row(dpast, h0 + 1))
        dk_n = jnp.where(top, _select_row(dq_t, h0), _select_row(dq_t, h0 + 1))
        sc, sn = _sample_scores(q_ref[0, :, cols], kn_ref[0, :, cols], _cache_pair_t(kc_ref, p), True)
        s_c.append(sc + dq_col - dk_c)
        s_n.append(jnp.where(kcol <= qrow, sn + dq_col - dk_n, NEG_INF))
        vn.append(vn_ref[0, :, cols])
        vc.append(_cache_pair_t(vc_ref, p))
    outs = _sample_softmax_pv(s_c, s_n, vn, vc, True)
    for p in range(hp):
        o_ref[0, :, p * LANES:(p + 1) * LANES] = _pair_heads(*outs[p]).astype(o_ref.dtype)


def _attn_pair_sample(kernel_fn, name, layer, qkv, cache_k_t, cache_v_t, extras, extra_specs,
                      scratch=()):
    b, t, _ = qkv.shape
    heads, _, past = cache_k_t.shape[2:]
    hp = HP_SAMPLE
    groups = heads // (2 * hp)
    width = hp * LANES
    cache_spec = pl.BlockSpec((1, 1, 2 * hp, HEAD_DIM, past), lambda b, g: (layer, b, g, 0, 0))
    return pl.pallas_call(
        functools.partial(kernel_fn, hp=hp),
        grid=(b, groups),
        in_specs=[pl.BlockSpec((1, t, width), lambda b, g: (b, 0, g)),
                  pl.BlockSpec((1, t, width), lambda b, g: (b, 0, groups + g)),
                  pl.BlockSpec((1, t, width), lambda b, g: (b, 0, 2 * groups + g)),
                  cache_spec, cache_spec] + list(extra_specs),
        out_specs=pl.BlockSpec((1, t, width), lambda b, g: (b, 0, g)),
        out_shape=jax.ShapeDtypeStruct((b, t, heads * HEAD_DIM), BF16),
        scratch_shapes=list(scratch),
        compiler_params=_cparams(2, VMEM_LIMIT),
        name=name,
    )(qkv, qkv, qkv, cache_k_t, cache_v_t, *extras)


def _attn_b_sample_kernel(q_ref, kn_ref, vn_ref, kc_ref, vc_ref, lam_ref, subln_ref, o_ref,
                          *, lam_init):
    t = q_ref.shape[1]
    past = kc_ref.shape[2] // H_B
    shape_c = (2 * t, past)
    shape_n = (2 * t, t)
    qpos_c = past + lax.broadcasted_iota(jnp.int32, shape_c, 0) % t
    dist_c = jnp.abs(qpos_c - lax.broadcasted_iota(jnp.int32, shape_c, 1)).astype(F32)
    dist_n = jnp.abs(lax.broadcasted_iota(jnp.int32, shape_n, 0) % t
                     - lax.broadcasted_iota(jnp.int32, shape_n, 1)).astype(F32)
    lam = _diff_lambda(lam_ref, lam_init)
    s_c, s_n, vn, vc = [], [], [], []
    for h in range(H_B):
        cols = slice(h * LANES, (h + 1) * LANES)
        slope = 2.0 ** (-8.0 * (h + 1) / H_B)
        kc = kc_ref[0, 0, pl.ds(h, past, stride=H_B), :].astype(BF16)
        sc, sn = _sample_scores(q_ref[0, :, cols], kn_ref[0, :, cols], kc, False)
        s_c.append(sc - slope * dist_c)
        s_n.append(sn - slope * dist_n)
        vn.append(vn_ref[0, :, cols])
        vc.append(vc_ref[0, 0, pl.ds(h, past, stride=H_B), :].astype(BF16))
    outs = _sample_softmax_pv(s_c, s_n, vn, vc, False)
    for h in range(H_B):
        o_ref[0, :, h * LANES:(h + 1) * LANES] = _diff_out(
            outs[h][0], outs[h][1], lam, subln_ref[...], lam_init).astype(o_ref.dtype)


def _attn_b_sample(layer, qkv, cache_k, cache_v, lam_vec, subln, lam_init):
    b, t, _ = qkv.shape
    rows = cache_k.shape[2]
    col0 = 3 * D_A // D_B
    cache_spec = pl.BlockSpec((1, 1, rows, LANES), lambda b: (layer, b, 0, 0))
    return pl.pallas_call(
        functools.partial(_attn_b_sample_kernel, lam_init=lam_init),
        grid=(b,),
        in_specs=[pl.BlockSpec((1, t, D_B), lambda b: (b, 0, col0)),
                  pl.BlockSpec((1, t, D_B), lambda b: (b, 0, col0 + 1)),
                  pl.BlockSpec((1, t, D_B), lambda b: (b, 0, col0 + 2)),
                  cache_spec, cache_spec, _full(lam_vec.shape), _full(subln.shape)],
        out_specs=pl.BlockSpec((1, t, D_B), lambda b: (b, 0, 0)),
        out_shape=jax.ShapeDtypeStruct((b, t, D_B), BF16),
        compiler_params=_cparams(1, VMEM_LIMIT),
        name="attn_b_sample",
    )(qkv, qkv, qkv, cache_k, cache_v, lam_vec, subln)


def _heads_last(a_t, heads):
    lead = a_t.shape[:-2]
    s = a_t.shape[-1]
    a = a_t.reshape(lead + (heads, HEAD_DIM, s))
    n = len(lead)
    return jnp.transpose(a, tuple(range(n)) + (n + 2, n, n + 1))


def kernel(x_prompt, x_sample, c_prompt, c_sample, cache_a_k, cache_a_v, cache_b_k, cache_b_v, cache_c_k, cache_c_v, cache_c_logf, w_ada, b_ada, norm_pre, norm_post, ffn_w_gate, ffn_w_up, ffn_w_down, w_in_ab, w_out_ab, relpos_a, lambda_b, subln_b, w_in_c, b_f, w_out_c):
    n_prompt, s_prompt = x_prompt.shape[:2]
    n_sample, t_sample = x_sample.shape[:2]

    wg = ffn_w_gate.astype(BF16)
    wu = ffn_w_up.astype(BF16)
    wd = ffn_w_down.astype(BF16)
    w_in_ab16 = w_in_ab.astype(BF16)
    w_in_ab16_t = jnp.swapaxes(w_in_ab, 1, 2).astype(BF16)
    w_out_ab16 = w_out_ab.astype(BF16)
    w_out_ab16_t = jnp.swapaxes(w_out_ab, 1, 2).astype(BF16)
    w_qkv_c16 = w_in_c[:, :, :3 * D_C].astype(BF16)
    w_qkv_c16_t = jnp.swapaxes(w_in_c[:, :, :3 * D_C], 1, 2).astype(BF16)
    w_f16 = jnp.pad(w_in_c[:, :, 3 * D_C:], ((0, 0), (0, 0), (0, LANES - H_C))).astype(BF16)
    w_f16_t = jnp.swapaxes(w_f16, 1, 2)
    b_f_row = jnp.pad(b_f, ((0, 0), (0, LANES - H_C)))[:, None, :]
    b_f_col = jnp.swapaxes(b_f_row, 1, 2)
    w_out_c16 = w_out_c.astype(BF16)
    w_out_c16_t = jnp.swapaxes(w_out_c, 1, 2).astype(BF16)
    cache_a_k_t = jnp.transpose(cache_a_k, (0, 1, 3, 4, 2))
    cache_a_v_t = jnp.transpose(cache_a_v, (0, 1, 3, 4, 2))
    cache_c_k_t = jnp.transpose(cache_c_k, (0, 1, 3, 4, 2))
    cache_c_v_t = jnp.transpose(cache_c_v, (0, 1, 3, 4, 2))
    clogf_t = jnp.swapaxes(cache_c_logf, -1, -2)
    past_b = cache_b_k.shape[2]
    cache_b_k_rows = cache_b_k.reshape(cache_b_k.shape[:2] + (past_b * H_B, 2 * HEAD_DIM))
    cache_b_v_rows = cache_b_v.reshape(cache_b_v.shape[:2] + (past_b * H_B, 2 * HEAD_DIM))

    rows = n_prompt + n_sample
    rows_pad = -(-rows // 8) * 8
    c_all = jnp.concatenate([c_prompt, c_sample, jnp.zeros((rows_pad - rows, D_MODEL), F32)], axis=0)
    mod_all = _ada(c_all, w_ada, b_ada).reshape(DEPTH, rows_pad, N_SUB, 3, D_MODEL)
    pbias, sbias = _relpos_bias(relpos_a, t_sample)

    def sublayer_params(li, row0, nrows):
        mod = [mod_all[li, row0:row0 + nrows, i] for i in range(N_SUB)]
        gpre = [norm_pre[li, i][None, :] for i in range(N_SUB)]
        gpost = [norm_post[li, i][None, :] for i in range(N_SUB)]
        return mod, gpre, gpost

    def run_prompt(x):
        a_k, a_v, b_k, b_v, c_k, c_v, c_lf = [], [], [], [], [], [], []
        for li in range(DEPTH):
            mod, gpre, gpost = sublayer_params(li, 0, n_prompt)
            x = _ffn(x, mod[0], gpre[0], gpost[0], wg[li, 0], wu[li, 0], wd[li, 0])
            j = li // 2
            if li % 2 == 0:
                lam_init = _lam_init(li)
                q_t, k, v_t, ka_t, va_t, kb, vb = _inproj_ab_prompt(
                    x, mod[1], gpre[1], w_in_ab16[j], w_in_ab16_t[j])
                out_a = _attn_a_prompt(q_t, k, v_t, pbias[j])
                out_b = _attn_b_prompt(q_t, k, v_t, lambda_b[j], subln_b[j][:, None], lam_init)
                w = min(A_PAST, s_prompt)
                a_k.append(ka_t[:, :, s_prompt - w:])
                a_v.append(va_t[:, :, s_prompt - w:])
                b_k.append(kb)
                b_v.append(vb)
                x = _outproj_prompt(x, mod[1], gpost[1], [out_a, out_b],
                                    [w_out_ab16_t[j, :, :D_A], w_out_ab16_t[j, :, D_A:]])
            else:
                q_t, k, v_t, k_t32, v_t32, logf, logf_t = _inproj_c_prompt(
                    x, mod[1], gpre[1], w_qkv_c16[j, :, D_C:2 * D_C], w_qkv_c16_t[j],
                    w_f16[j], w_f16_t[j], b_f_row[j], b_f_col[j])
                dq_t, key_extras = _logf_scan(logf)
                out = _attn_c_prompt(q_t, k, v_t, dq_t, key_extras)
                c_k.append(k_t32)
                c_v.append(v_t32)
                c_lf.append(logf_t)
                x = _outproj_prompt(x, mod[1], gpost[1], [out], [w_out_c16_t[j]])
            x = _ffn(x, mod[2], gpre[2], gpost[2], wg[li, 1], wu[li, 1], wd[li, 1])
        nb, sb = x.shape[:2]
        return (x,
                _heads_last(jnp.stack(a_k), H_A), _heads_last(jnp.stack(a_v), H_A),
                jnp.stack(b_k).reshape(-1, nb, sb, H_B, 2 * HEAD_DIM),
                jnp.stack(b_v).reshape(-1, nb, sb, H_B, 2 * HEAD_DIM),
                _heads_last(jnp.stack(c_k), H_C), _heads_last(jnp.stack(c_v), H_C),
                jnp.swapaxes(jnp.stack(c_lf), -1, -2))

    def run_sample(x):
        a_k, a_v, b_k, b_v, c_k, c_v, c_lf = [], [], [], [], [], [], []
        for li in range(DEPTH):
            mod, gpre, gpost = sublayer_params(li, n_prompt, n_sample)
            x = _ffn(x, mod[0], gpre[0], gpost[0], wg[li, 0], wu[li, 0], wd[li, 0])
            j = li // 2
            if li % 2 == 0:
                lam_init = _lam_init(li)
                qkv, ka, va, kb, vb = _inproj_ab(x, mod[1], gpre[1], w_in_ab16[j])
                subln = subln_b[j][None, :]
                out_a = _attn_pair_sample(
                    _attn_a_sample_kernel, "attn_a_sample", j, qkv, cache_a_k_t, cache_a_v_t,
                    [sbias[j]],
                    [pl.BlockSpec((2 * HP_SAMPLE, t_sample, SBIAS_W), lambda b, g: (g, 0, 0))])
                out_b = _attn_b_sample(j, qkv, cache_b_k_rows, cache_b_v_rows, lambda_b[j], subln,
                                       lam_init)
                a_k.append(ka)
                a_v.append(va)
                b_k.append(kb)
                b_v.append(vb)
                x = _outproj(x, mod[1], gpost[1], [out_a, out_b],
                             [w_out_ab16[j, :D_A], w_out_ab16[j, D_A:]])
            else:
                qkv, k, v, logf = _inproj_c(x, mod[1], gpre[1], w_qkv_c16[j], w_f16[j], b_f_row[j])
                past = clogf_t.shape[-1]
                out = _attn_pair_sample(
                    _attn_c_sample_kernel, "attn_c_sample", j, qkv, cache_c_k_t, cache_c_v_t,
                    [logf, clogf_t],
                    [pl.BlockSpec((1, t_sample, LANES), lambda b, g: (b, 0, 0)),
                     pl.BlockSpec((1, 1, H_C, past), lambda b, g, j=j: (j, b, 0, 0))],
                    scratch=[pltpu.VMEM((H_C, past), F32)])
                c_k.append(k)
                c_v.append(v)
                c_lf.append(logf[:, :, :H_C])
                x = _outproj(x, mod[1], gpost[1], [out], [w_out_c16[j]])
            x = _ffn(x, mod[2], gpre[2], gpost[2], wg[li, 1], wu[li, 1], wd[li, 1])
        nb, sb = x.shape[:2]
        return (x,
                jnp.stack(a_k).reshape(-1, nb, sb, H_A, HEAD_DIM),
                jnp.stack(a_v).reshape(-1, nb, sb, H_A, HEAD_DIM),
                jnp.stack(b_k).reshape(-1, nb, sb, H_B, 2 * HEAD_DIM),
                jnp.stack(b_v).reshape(-1, nb, sb, H_B, 2 * HEAD_DIM),
                jnp.stack(c_k).reshape(-1, nb, sb, H_C, HEAD_DIM),
                jnp.stack(c_v).reshape(-1, nb, sb, H_C, HEAD_DIM),
                jnp.stack(c_lf))

    p = run_prompt(x_prompt)
    s = run_sample(x_sample)
    return (p[0], s[0]) + tuple(p[1:]) + tuple(s[1:])
```

```python
import functools
import math

import jax
import jax.numpy as jnp
from jax import lax
from jax.experimental import pallas as pl
from jax.experimental.pallas import tpu as pltpu

D_MODEL = 1024
DEPTH = 4
CHUNK = 64
HEAD_DIM = 64
H_A = 8
H_B = 4
H_C = 16
D_A = H_A * HEAD_DIM
D_B = H_B * 2 * HEAD_DIM
D_C = H_C * HEAD_DIM
A_LEFT_CHUNKS = 8
A_PAST = A_LEFT_CHUNKS * CHUNK
REL_CLIP = 128
D_FF = 2816
N_SUB = 3
MACARON_W = 0.5
EPS = 1e-6
NEG_INF = -1e30
ATTN_SCALE = HEAD_DIM ** -0.5
LOG2E = 1.4426950408889634
QSCALE_LOG2 = ATTN_SCALE * LOG2E

LANES = 128
TOK_TILE = 256
SAMPLE_GROUP = 8
TP = 256
SUM_ROWS = 16
DIAG_TILES = 2
TQW = DIAG_TILES * TP
HP_A = 4
HP_B = 4
HP_C = 4
HP_SAMPLE = 4
BAND_TILES = 3
BAND_A = BAND_TILES * TP
SCAN_T = 512
TAB_PAD = 384
TOEP = 1024
SBIAS_W = 640
ADA_TN = 1536
VMEM_LIMIT = 56 * 1024 * 1024

BF16 = jnp.bfloat16
F32 = jnp.float32


def _lam_init(li):
    return 0.8 - 0.6 * math.exp(-0.3 * li)


def _cparams(n_axes, vmem=None):
    return pltpu.CompilerParams(dimension_semantics=("arbitrary",) * n_axes,
                                vmem_limit_bytes=vmem)


def _dot(a, b):
    return jnp.dot(a, b, preferred_element_type=F32)


def _dot_nt(a, b):
    return lax.dot_general(a, b, (((1,), (1,)), ((), ())), preferred_element_type=F32)


def _dot_exact(a, b):
    return jnp.dot(a, b, preferred_element_type=F32, precision=lax.Precision.HIGHEST)


def _dot_nt_exact(a, b):
    return lax.dot_general(a, b, (((1,), (1,)), ((), ())), preferred_element_type=F32,
                           precision=lax.Precision.HIGHEST)


def _split3(x):
    hi = x.astype(BF16).astype(F32)
    r1 = x - hi
    mid = r1.astype(BF16).astype(F32)
    lo = (r1 - mid).astype(BF16).astype(F32)
    return hi, mid, lo


def _norm_mod(x, m, g_pre):
    y = x * lax.rsqrt(jnp.mean(x * x, axis=-1, keepdims=True) + EPS)
    return (y * g_pre) * (1.0 + m[:, 1:2, :]) + m[:, 0:1, :]


def _gated_residual(x, m, g_post, out, res_w):
    y = out * lax.rsqrt(jnp.mean(out * out, axis=-1, keepdims=True) + EPS)
    return x + (res_w * (1.0 + m[:, 2:3, :])) * (y * g_post)


def _tok_layout(x):
    nb_total, sb_total, _ = x.shape
    if sb_total >= TOK_TILE:
        nb, sb = 1, TOK_TILE
    else:
        nb, sb = SAMPLE_GROUP, sb_total
    tiles = sb_total // sb
    grid = (nb_total // nb) * tiles

    def spec(width, rows_per_token=1):
        return pl.BlockSpec((nb, sb * rows_per_token, width), lambda i: (i // tiles, i % tiles, 0))

    def spec_t(width):
        return pl.BlockSpec((nb, width, sb), lambda i: (i // tiles, 0, i % tiles))

    mod_spec = pl.BlockSpec((nb, 3, D_MODEL), lambda i: (i // tiles, 0, 0))
    return nb, sb, grid, spec, spec_t, mod_spec


def _full(shape):
    return pl.BlockSpec(shape, lambda *_: (0,) * len(shape), pipeline_mode=pl.Buffered(1))


def _ada_kernel(c_ref, w_ref, b_ref, o_ref):
    c = c_ref[...]
    a = (c * jax.nn.sigmoid(c)).astype(BF16)
    o_ref[0] = _dot(a, w_ref[0].astype(BF16)) + b_ref[0]


def _ada(c_all, w_ada, b_ada):
    rows = c_all.shape[0]
    n_out = w_ada.shape[-1]
    return pl.pallas_call(
        _ada_kernel,
        grid=(DEPTH, n_out // ADA_TN),
        in_specs=[pl.BlockSpec((rows, D_MODEL), lambda l, n: (0, 0)),
                  pl.BlockSpec((1, D_MODEL, ADA_TN), lambda l, n: (l, 0, n)),
                  pl.BlockSpec((1, 1, ADA_TN), lambda l, n: (l, 0, n))],
        out_specs=pl.BlockSpec((1, rows, ADA_TN), lambda l, n: (l, 0, n)),
        out_shape=jax.ShapeDtypeStruct((DEPTH, rows, n_out), F32),
        compiler_params=_cparams(2, VMEM_LIMIT),
        name="ada",
    )(c_all, w_ada, b_ada.reshape(DEPTH, 1, n_out))


def _ffn_body(x, m, g_pre, g_post, wg_ref, wu_ref, wd_ref):
    nb, sb, d = x.shape
    h = _norm_mod(x, m, g_pre).reshape(nb * sb, d).astype(BF16)
    g = _dot(h, wg_ref[0, 0])
    u = _dot(h, wu_ref[0, 0])
    a = ((g * jax.nn.sigmoid(g)) * u).astype(BF16)
    out = _dot(a, wd_ref[0, 0]).reshape(nb, sb, d)
    return _gated_residual(x, m, g_post, out, MACARON_W)


def _ffn_weight_specs(li, k):
    def one(rows, cols):
        return pl.BlockSpec((1, 1, rows, cols), lambda *_: (li, k, 0, 0), pipeline_mode=pl.Buffered(1))
    return [one(D_MODEL, D_FF), one(D_MODEL, D_FF), one(D_FF, D_MODEL)]


def _ffn_kernel(x_ref, mod_ref, gpre_ref, gpost_ref, wg_ref, wu_ref, wd_ref, o_ref):
    o_ref[...] = _ffn_body(x_ref[...], mod_ref[...], gpre_ref[...], gpost_ref[...],
                           wg_ref, wu_ref, wd_ref)


def _ffn(x, mod, g_pre, g_post, wg, wu, wd, li, k):
    nb, sb, grid, spec, _, mod_spec = _tok_layout(x)
    return pl.pallas_call(
        _ffn_kernel,
        grid=(grid,),
        in_specs=[spec(D_MODEL), mod_spec, _full((1, D_MODEL)), _full((1, D_MODEL))]
        + _ffn_weight_specs(li, k),
        out_specs=spec(D_MODEL),
        out_shape=jax.ShapeDtypeStruct(x.shape, F32),
        compiler_params=_cparams(1, VMEM_LIMIT),
        name="ffn",
    )(x, mod, g_pre, g_post, wg, wu, wd)


def _inproj_ab_kernel(x_ref, mod_ref, gpre_ref, w_ref, qkv_ref, ka_ref, va_ref, kb_ref, vb_ref):
    x = x_ref[...]
    nb, sb, d = x.shape
    h = _norm_mod(x, mod_ref[...], gpre_ref[...]).reshape(nb * sb, d).astype(BF16)
    f32_outs = {1: ka_ref, 2: va_ref, 4: kb_ref, 5: vb_ref}
    for c in range(6):
        cols = slice(c * D_A, (c + 1) * D_A)
        p = _dot(h, w_ref[:, cols]).reshape(nb, sb, D_A)
        if c in f32_outs:
            f32_outs[c][...] = p
            qkv_ref[:, :, cols] = p.astype(BF16)
        else:
            qkv_ref[:, :, cols] = (p * ATTN_SCALE).astype(BF16)


def _inproj_ab(x, mod, g_pre, w_in):
    nb, sb, grid, spec, _, mod_spec = _tok_layout(x)
    nbt, sbt, _ = x.shape
    f32_out = jax.ShapeDtypeStruct((nbt, sbt, D_A), F32)
    return pl.pallas_call(
        _inproj_ab_kernel,
        grid=(grid,),
        in_specs=[spec(D_MODEL), mod_spec, _full((1, D_MODEL)), _full((D_MODEL, 6 * D_A))],
        out_specs=[spec(6 * D_A)] + [spec(D_A)] * 4,
        out_shape=[jax.ShapeDtypeStruct((nbt, sbt, 6 * D_A), BF16)] + [f32_out] * 4,
        compiler_params=_cparams(1, VMEM_LIMIT),
        name="inproj_ab",
    )(x, mod, g_pre, w_in)


def _log_sigmoid(z):
    return jnp.minimum(z, 0.0) - jnp.log1p(jnp.exp(-jnp.abs(z)))


def _inproj_c_kernel(x_ref, mod_ref, gpre_ref, w_ref, wf_ref, bf_ref,
                     qkv_ref, k_ref, v_ref, logf_ref):
    x = x_ref[...]
    nb, sb, d = x.shape
    h = _norm_mod(x, mod_ref[...], gpre_ref[...]).reshape(nb * sb, d).astype(BF16)
    f32_outs = {1: k_ref, 2: v_ref}
    for c in range(3):
        cols = slice(c * D_C, (c + 1) * D_C)
        p = _dot(h, w_ref[:, cols]).reshape(nb, sb, D_C)
        if c in f32_outs:
            f32_outs[c][...] = p
            qkv_ref[:, :, cols] = p.astype(BF16)
        else:
            qkv_ref[:, :, cols] = (p * ATTN_SCALE).astype(BF16)
    logf = _log_sigmoid(_dot(h, wf_ref[...]) + bf_ref[...])
    logf_ref[...] = logf.reshape(nb, sb, LANES)


def _inproj_c(x, mod, g_pre, w_qkv, w_f, b_f):
    nb, sb, grid, spec, _, mod_spec = _tok_layout(x)
    nbt, sbt, _ = x.shape
    f32_out = jax.ShapeDtypeStruct((nbt, sbt, D_C), F32)
    return pl.pallas_call(
        _inproj_c_kernel,
        grid=(grid,),
        in_specs=[spec(D_MODEL), mod_spec, _full((1, D_MODEL)), _full((D_MODEL, 3 * D_C)),
                  _full((D_MODEL, LANES)), _full((1, LANES))],
        out_specs=[spec(3 * D_C), spec(D_C), spec(D_C), spec(LANES)],
        out_shape=[jax.ShapeDtypeStruct((nbt, sbt, 3 * D_C), BF16), f32_out, f32_out,
                   jax.ShapeDtypeStruct((nbt, sbt, LANES), F32)],
        compiler_params=_cparams(1, VMEM_LIMIT),
        name="inproj_c",
    )(x, mod, g_pre, w_qkv, w_f, b_f)


def _outproj_kernel(*refs, n_in):
    x_ref, mod_ref, gpost_ref = refs[:3]
    a_refs = refs[3:3 + n_in]
    w_refs = refs[3 + n_in:3 + 2 * n_in]
    o_ref = refs[-1]
    x = x_ref[...]
    nb, sb, d = x.shape
    out = None
    for a_ref, w_ref in zip(a_refs, w_refs):
        a = a_ref[...]
        t = _dot(a.reshape(nb * sb, a.shape[-1]), w_ref[...])
        out = t if out is None else out + t
    o_ref[...] = _gated_residual(x, mod_ref[...], gpost_ref[...], out.reshape(nb, sb, d), 1.0)


def _outproj(x, mod, g_post, acts, weights):
    nb, sb, grid, spec, _, mod_spec = _tok_layout(x)
    n_in = len(acts)
    return pl.pallas_call(
        functools.partial(_outproj_kernel, n_in=n_in),
        grid=(grid,),
        in_specs=([spec(D_MODEL), mod_spec, _full((1, D_MODEL))]
                  + [spec(a.shape[-1]) for a in acts]
                  + [_full(w.shape) for w in weights]),
        out_specs=spec(D_MODEL),
        out_shape=jax.ShapeDtypeStruct(x.shape, F32),
        compiler_params=_cparams(1, VMEM_LIMIT),
        name="outproj",
    )(x, mod, g_post, *acts, *weights)


def _inproj_ab_prompt_kernel(x_ref, mod0_ref, mod_ref, gpre0_ref, gpost0_ref, wg_ref, wu_ref, wd_ref,
                             gpre_ref, w_ref, wt_ref,
                             xo_ref, qt_ref, k_ref, vt_ref, kat_ref, vat_ref, kb_ref, vb_ref):
    x = _ffn_body(x_ref[...], mod0_ref[...], gpre0_ref[...], gpost0_ref[...], wg_ref, wu_ref, wd_ref)
    xo_ref[...] = x
    _, sb, d = x.shape
    h = _norm_mod(x, mod_ref[...], gpre_ref[...]).reshape(sb, d).astype(BF16)

    def nat(c):
        return _dot(h, w_ref[:, c * D_A:(c + 1) * D_A])

    def tr(c):
        return _dot_nt(wt_ref[c * D_A:(c + 1) * D_A, :], h)

    qt_ref[0, :D_A, :] = (tr(0) * QSCALE_LOG2).astype(BF16)
    qt_ref[0, D_A:, :] = (tr(3) * QSCALE_LOG2).astype(BF16)
    ka_t = tr(1)
    kat_ref[0] = ka_t
    k_ref[0, :, :D_A] = ka_t.T.astype(BF16)
    kb = nat(4)
    k_ref[0, :, D_A:] = kb.astype(BF16)
    va_t = tr(2)
    vat_ref[0] = va_t
    vt_ref[0, :D_A, :] = va_t.astype(BF16)
    vb = nat(5)
    vt_ref[0, D_A:, :] = vb.T.astype(BF16)
    for hd in range(H_B):
        kb_ref[0, pl.ds(hd, sb, stride=H_B), :] = kb[:, hd * LANES:(hd + 1) * LANES]
        vb_ref[0, pl.ds(hd, sb, stride=H_B), :] = vb[:, hd * LANES:(hd + 1) * LANES]


def _ffn_inproj_ab_prompt(x, mod0, mod1, g_pre0, g_post0, wg, wu, wd, li, g_pre1, w_in, w_in_t):
    nb, sb, grid, spec, spec_t, mod_spec = _tok_layout(x)
    b, s, _ = x.shape
    width = D_A + D_B
    vec = _full((1, D_MODEL))
    return pl.pallas_call(
        _inproj_ab_prompt_kernel,
        grid=(grid,),
        in_specs=[spec(D_MODEL), mod_spec, mod_spec, vec, vec] + _ffn_weight_specs(li, 0)
        + [vec, _full(w_in.shape), _full(w_in_t.shape)],
        out_specs=[spec(D_MODEL), spec_t(width), spec(width), spec_t(width), spec_t(D_A), spec_t(D_A),
                   spec(LANES, H_B), spec(LANES, H_B)],
        out_shape=[jax.ShapeDtypeStruct(x.shape, F32),
                   jax.ShapeDtypeStruct((b, width, s), BF16),
                   jax.ShapeDtypeStruct((b, s, width), BF16),
                   jax.ShapeDtypeStruct((b, width, s), BF16),
                   jax.ShapeDtypeStruct((b, D_A, s), F32),
                   jax.ShapeDtypeStruct((b, D_A, s), F32),
                   jax.ShapeDtypeStruct((b, s * H_B, LANES), F32),
                   jax.ShapeDtypeStruct((b, s * H_B, LANES), F32)],
        compiler_params=_cparams(1, VMEM_LIMIT),
        name="ffn_inproj_ab_prompt",
    )(x, mod0, mod1, g_pre0, g_post0, wg, wu, wd, g_pre1, w_in, w_in_t)


def _inproj_c_prompt_kernel(x_ref, mod0_ref, mod_ref, gpre0_ref, gpost0_ref, wg_ref, wu_ref, wd_ref,
                            gpre_ref, wk_ref, wt_ref, wf_ref, wft_ref, bf_ref, bft_ref,
                            xo_ref, qt_ref, k_ref, vt_ref, kt32_ref, vt32_ref, logf_ref, logft_ref):
    x = _ffn_body(x_ref[...], mod0_ref[...], gpre0_ref[...], gpost0_ref[...], wg_ref, wu_ref, wd_ref)
    xo_ref[...] = x
    _, sb, d = x.shape
    h = _norm_mod(x, mod_ref[...], gpre_ref[...]).reshape(sb, d).astype(BF16)
    qt_ref[0] = (_dot_nt(wt_ref[:D_C, :], h) * QSCALE_LOG2).astype(BF16)
    k_t = _dot_nt(wt_ref[D_C:2 * D_C, :], h)
    kt32_ref[0] = k_t
    k_ref[0] = k_t.T.astype(BF16)
    v_t = _dot_nt(wt_ref[2 * D_C:, :], h)
    vt32_ref[0] = v_t
    vt_ref[0] = v_t.astype(BF16)
    logf_ref[0] = _log_sigmoid(_dot(h, wf_ref[...]) + bf_ref[...])
    logft_ref[0] = _log_sigmoid(_dot_nt(wft_ref[...], h) + bft_ref[...])[:H_C, :]


def _ffn_inproj_c_prompt(x, mod0, mod1, g_pre0, g_post0, wg, wu, wd, li, g_pre1,
                         w_k, w_qkv_t, w_f, w_f_t, b_f, b_f_t):
    nb, sb, grid, spec, spec_t, mod_spec = _tok_layout(x)
    b, s, _ = x.shape
    vec = _full((1, D_MODEL))
    return pl.pallas_call(
        _inproj_c_prompt_kernel,
        grid=(grid,),
        in_specs=[spec(D_MODEL), mod_spec, mod_spec, vec, vec] + _ffn_weight_specs(li, 0)
        + [vec, _full(w_k.shape), _full(w_qkv_t.shape), _full(w_f.shape), _full(w_f_t.shape),
           _full(b_f.shape), _full(b_f_t.shape)],
        out_specs=[spec(D_MODEL), spec_t(D_C), spec(D_C), spec_t(D_C), spec_t(D_C), spec_t(D_C),
                   spec(LANES), spec_t(H_C)],
        out_shape=[jax.ShapeDtypeStruct(x.shape, F32),
                   jax.ShapeDtypeStruct((b, D_C, s), BF16),
                   jax.ShapeDtypeStruct((b, s, D_C), BF16),
                   jax.ShapeDtypeStruct((b, D_C, s), BF16),
                   jax.ShapeDtypeStruct((b, D_C, s), F32),
                   jax.ShapeDtypeStruct((b, D_C, s), F32),
                   jax.ShapeDtypeStruct((b, s, LANES), F32),
                   jax.ShapeDtypeStruct((b, H_C, s), F32)],
        compiler_params=_cparams(1, VMEM_LIMIT),
        name="ffn_inproj_c_prompt",
    )(x, mod0, mod1, g_pre0, g_post0, wg, wu, wd, g_pre1, w_k, w_qkv_t, w_f, w_f_t, b_f, b_f_t)


def _outproj_prompt_kernel(*refs, n_in):
    x_ref, mod_ref, gpost_ref = refs[:3]
    a_refs = refs[3:3 + n_in]
    wt_refs = refs[3 + n_in:3 + 2 * n_in]
    o_ref = refs[-1]
    x = x_ref[...]
    out_t = None
    for a_ref, wt_ref in zip(a_refs, wt_refs):
        t = _dot(wt_ref[...], a_ref[0])
        out_t = t if out_t is None else out_t + t
    out = out_t.T.reshape(x.shape)
    o_ref[...] = _gated_residual(x, mod_ref[...], gpost_ref[...], out, 1.0)


def _outproj_prompt(x, mod, g_post, acts_t, weights_t):
    nb, sb, grid, spec, spec_t, mod_spec = _tok_layout(x)
    n_in = len(acts_t)
    return pl.pallas_call(
        functools.partial(_outproj_prompt_kernel, n_in=n_in),
        grid=(grid,),
        in_specs=([spec(D_MODEL), mod_spec, _full((1, D_MODEL))]
                  + [spec_t(a.shape[1]) for a in acts_t]
                  + [_full(w.shape) for w in weights_t]),
        out_specs=spec(D_MODEL),
        out_shape=jax.ShapeDtypeStruct(x.shape, F32),
        compiler_params=_cparams(1, VMEM_LIMIT),
        name="outproj_prompt",
    )(x, mod, g_post, *acts_t, *weights_t)


def _diff_lambda(lam_ref, lam_init):
    lv = lam_ref[...]
    a = jnp.sum(lv[0:1] * lv[1:2], axis=1, keepdims=True)
    b = jnp.sum(lv[2:3] * lv[3:4], axis=1, keepdims=True)
    return jnp.exp(a) - jnp.exp(b) + lam_init


def _alibi_slope(g):
    slope = jnp.float32(0.0)
    for h in range(H_B):
        slope = jnp.where(g == h, jnp.float32(2.0 ** (-8.0 * (h + 1) / H_B)), slope)
    return slope


def _select_lane(a, idx):
    lane = lax.broadcasted_iota(jnp.int32, a.shape, 1)
    return jnp.sum(jnp.where(lane == idx, a, 0.0), axis=1, keepdims=True)


def _select_row(a, idx):
    row = lax.broadcasted_iota(jnp.int32, a.shape, 0)
    return jnp.sum(jnp.where(row == idx, a, 0.0), axis=0, keepdims=True)


def _relpos_bias_kernel(tab_ref, pbias_ref, sbias_ref):
    t = tab_ref[0]
    t_hi = t.astype(BF16)
    r1 = t - t_hi.astype(F32)
    t_mid = r1.astype(BF16)
    t_lo = (r1 - t_mid.astype(F32)).astype(BF16)
    ent = lax.broadcasted_iota(jnp.int32, (TAB_PAD, TOEP), 0)
    n = lax.broadcasted_iota(jnp.int32, (TAB_PAD, TOEP), 1)

    def lookup(rel_pos):
        onehot = jnp.where(ent == jnp.clip(rel_pos, -REL_CLIP, REL_CLIP) + REL_CLIP, 1.0, 0.0)
        onehot = onehot.astype(BF16)
        return _dot(t_hi, onehot) + _dot(t_mid, onehot) + _dot(t_lo, onehot)

    gen_t = lookup(A_PAST + jnp.where(n < TP, n, n - TOEP)) * LOG2E
    d = (lax.broadcasted_iota(jnp.int32, (BAND_A, TP), 0) // CHUNK
         - lax.broadcasted_iota(jnp.int32, (BAND_A, TP), 1) // CHUNK)
    valid = (d >= 0) & (d <= A_LEFT_CHUNKS)
    gen = lookup(A_PAST + TP - 1 - n)
    t_sample = sbias_ref.shape[2]
    for h in range(H_A):
        rows_t = jnp.broadcast_to(gen_t[h:h + 1, :], (BAND_A, TOEP))
        toep_t = pltpu.roll(rows_t, 0, 1, stride=1, stride_axis=0)
        pbias_ref[0, h] = jnp.where(valid, toep_t[:, :TP], NEG_INF)
        rows = jnp.broadcast_to(gen[h:h + 1, :], (t_sample, TOEP))
        toep = pltpu.roll(rows, TOEP - TP + 1, 1, stride=1, stride_axis=0)
        sbias_ref[0, h] = toep[:, :SBIAS_W]


def _relpos_bias(relpos_a, sample_len):
    ne = relpos_a.shape[0]
    tab = jnp.pad(relpos_a, ((0, 0), (0, 0), (0, TAB_PAD - relpos_a.shape[-1])))
    return pl.pallas_call(
        _relpos_bias_kernel,
        grid=(ne,),
        in_specs=[pl.BlockSpec((1, H_A, TAB_PAD), lambda l: (l, 0, 0))],
        out_specs=[pl.BlockSpec((1, H_A, BAND_A, TP), lambda l: (l, 0, 0, 0)),
                   pl.BlockSpec((1, H_A, sample_len, SBIAS_W), lambda l: (l, 0, 0, 0))],
        out_shape=[jax.ShapeDtypeStruct((ne, H_A, BAND_A, TP), F32),
                   jax.ShapeDtypeStruct((ne, H_A, sample_len, SBIAS_W), F32)],
        compiler_params=_cparams(1, VMEM_LIMIT),
        name="relpos_bias",
    )(tab)


def _stats_init(m_ref, acc_ref):
    m_ref[...] = jnp.full(m_ref.shape, NEG_INF, F32)
    acc_ref[...] = jnp.zeros(acc_ref.shape, F32)


def _with_ones(v_t):
    return jnp.concatenate([v_t, jnp.ones((SUM_ROWS, v_t.shape[1]), v_t.dtype)], axis=0)


def _stats_update_all(m_ref, acc_ref, scores, values):
    n = len(scores)
    m_prev = [m_ref[i] for i in range(n)]
    m_new = [jnp.maximum(m_prev[i], jnp.max(scores[i], axis=0, keepdims=True)) for i in range(n)]
    alpha = [jnp.exp2(m_prev[i] - m_new[i]) for i in range(n)]
    p = [jnp.exp2(scores[i] - m_new[i]).astype(BF16) for i in range(n)]
    pv = [_dot(values[i], p[i]) for i in range(n)]
    for i in range(n):
        acc_ref[i] = alpha[i] * acc_ref[i] + pv[i]
        m_ref[i] = m_new[i]


def _stats_result(acc_ref, mi):
    acc = acc_ref[mi]
    d = acc.shape[0] - SUM_ROWS
    return acc[:d] * (1.0 / acc[d:d + 1])


def _stats_scratch(n, d, tq):
    return [pltpu.VMEM((n, 1, tq), F32), pltpu.VMEM((n, d + SUM_ROWS, tq), F32)]


def _map_rows(q_t, mi):
    row = lax.broadcasted_iota(jnp.int32, q_t.shape, 0)
    keep = (row >= mi * HEAD_DIM) & (row < (mi + 1) * HEAD_DIM)
    return jnp.where(keep, q_t, jnp.zeros_like(q_t))


def _attn_a_prompt_kernel(qt_ref, k_ref, vt_ref, bias_ref, o_ref, m_ref, acc_ref, *, hp):
    qi = pl.program_id(2)
    qm = [_map_rows(qt_ref[0, p * LANES:(p + 1) * LANES, :], mi) for p in range(hp) for mi in range(2)]
    _stats_init(m_ref, acc_ref)
    for j in range(BAND_TILES):
        kt = qi - (BAND_TILES - 1) + j

        def tile(j=j, kt=kt):
            start = pl.multiple_of(kt * TP, TP)
            scores, values = [], []
            for p in range(hp):
                cols = slice(p * LANES, (p + 1) * LANES)
                k = k_ref[0, pl.ds(start, TP), cols]
                v_t = vt_ref[0, cols, pl.ds(start, TP)]
                for mi in range(2):
                    scores.append(_dot(k, qm[2 * p + mi]) + bias_ref[2 * p + mi, j * TP:(j + 1) * TP, :])
                    values.append(_with_ones(v_t[mi * HEAD_DIM:(mi + 1) * HEAD_DIM]))
            _stats_update_all(m_ref, acc_ref, scores, values)

        if j == BAND_TILES - 1:
            tile()
        else:
            pl.when(kt >= 0)(tile)
    o = jnp.concatenate([_stats_result(acc_ref, i) for i in range(2 * hp)], axis=0)
    o_ref[0] = o.astype(o_ref.dtype)


def _attn_a_prompt(q_t, k, v_t, pbias):
    b, s, _ = k.shape
    hp = HP_A
    groups = D_A // (hp * LANES)
    return pl.pallas_call(
        functools.partial(_attn_a_prompt_kernel, hp=hp),
        grid=(b, groups, s // TP),
        in_specs=[pl.BlockSpec((1, hp * LANES, TP), lambda b, g, i: (b, g, i)),
                  pl.BlockSpec((1, s, hp * LANES), lambda b, g, i: (b, 0, g)),
                  pl.BlockSpec((1, hp * LANES, s), lambda b, g, i: (b, g, 0)),
                  pl.BlockSpec((2 * hp, BAND_A, TP), lambda b, g, i: (g, 0, 0))],
        out_specs=pl.BlockSpec((1, hp * LANES, TP), lambda b, g, i: (b, g, i)),
        out_shape=jax.ShapeDtypeStruct((b, D_A, s), BF16),
        scratch_shapes=_stats_scratch(2 * hp, HEAD_DIM, TP),
        compiler_params=_cparams(3, VMEM_LIMIT),
        name="attn_a_prompt",
    )(q_t, k, v_t, pbias)


def _attn_b_prompt_kernel(qt_ref, k_ref, vt_ref, ke_ref, lam_ref, subln_ref, o_ref,
                          m_ref, acc_ref, diag_ref, *, lam_init, hp):
    g = pl.program_id(1)
    qi = pl.program_id(2)
    coefs = [_alibi_slope(g * hp + p) * LOG2E for p in range(hp)]

    @pl.when(qi == 0)
    def _():
        for d in range(DIAG_TILES):
            kr = lax.broadcasted_iota(jnp.int32, (TP, TQW), 0) + d * TP
            qc = lax.broadcasted_iota(jnp.int32, (TP, TQW), 1)
            dist = jnp.abs(qc - kr).astype(F32)
            for p in range(hp):
                diag_ref[p, d] = jnp.where(kr // CHUNK <= qc // CHUNK, -coefs[p] * dist, NEG_INF)

    row = lax.broadcasted_iota(jnp.int32, (LANES, TQW), 0)
    qpos = (qi * TQW + lax.broadcasted_iota(jnp.int32, (LANES, TQW), 1)).astype(F32)
    part = row % 3
    qa_past, qa_diag = [], []
    for p in range(hp):
        c = coefs[p]
        q_t = qt_ref[0, p * LANES:(p + 1) * LANES, :]
        hi, mid, lo = _split3(jnp.where(row < 3, CHUNK * c, jnp.where(row < 6, c, -c * qpos)))
        qe = jnp.where(row < 9, jnp.where(part == 0, hi, jnp.where(part == 1, mid, lo)), 0.0).astype(BF16)
        for mi in range(2):
            qm = _map_rows(q_t, mi)
            qa_past.append(jnp.concatenate([qm, qe], axis=0))
            qa_diag.append(jnp.concatenate([qm, jnp.zeros_like(qe)], axis=0))
    _stats_init(m_ref, acc_ref)

    def tile(j, qa, diag):
        start = pl.multiple_of(j * TP, TP)
        ke = ke_ref[pl.ds(start, TP), :]
        scores, values = [], []
        for p in range(hp):
            cols = slice(p * LANES, (p + 1) * LANES)
            kx = jnp.concatenate([k_ref[0, pl.ds(start, TP), cols], ke], axis=1)
            v_t = _with_ones(vt_ref[0, cols, pl.ds(start, TP)])
            for mi in range(2):
                s_t = _dot(kx, qa[2 * p + mi])
                scores.append(s_t if diag is None else s_t + diag_ref[p, diag])
                values.append(v_t)
        _stats_update_all(m_ref, acc_ref, scores, values)

    def past_tile(j, carry):
        tile(j, qa_past, None)
        return carry

    lax.fori_loop(0, DIAG_TILES * qi, past_tile, 0)
    for d in range(DIAG_TILES):
        tile(DIAG_TILES * qi + d, qa_diag, d)
    lam = _diff_lambda(lam_ref, lam_init)
    outs = []
    for p in range(hp):
        o = _stats_result(acc_ref, 2 * p) - lam * _stats_result(acc_ref, 2 * p + 1)
        y = o * lax.rsqrt(jnp.mean(o * o, axis=0, keepdims=True) + EPS)
        outs.append((y * subln_ref[...]) * (1.0 - lam_init))
    o_ref[0] = jnp.concatenate(outs, axis=0).astype(o_ref.dtype)


def _attn_b_prompt(q_t, k, v_t, lam_vec, subln_col, lam_init):
    b, s, _ = k.shape
    hp = HP_B
    groups = D_B // (hp * LANES)
    blk0 = D_A // (hp * LANES)
    pos = jnp.arange(s, dtype=jnp.int32)[:, None]
    lane = jnp.arange(LANES, dtype=jnp.int32)[None, :]
    key_extras = jnp.where(lane < 3, pos // CHUNK,
                           jnp.where(lane < 6, pos % CHUNK, jnp.where(lane < 9, 1, 0))).astype(BF16)
    return pl.pallas_call(
        functools.partial(_attn_b_prompt_kernel, lam_init=lam_init, hp=hp),
        grid=(b, groups, s // TQW),
        in_specs=[pl.BlockSpec((1, hp * LANES, TQW), lambda b, g, i: (b, blk0 + g, i)),
                  pl.BlockSpec((1, s, hp * LANES), lambda b, g, i: (b, 0, blk0 + g)),
                  pl.BlockSpec((1, hp * LANES, s), lambda b, g, i: (b, blk0 + g, 0)),
                  _full(key_extras.shape), _full(lam_vec.shape), _full(subln_col.shape)],
        out_specs=pl.BlockSpec((1, hp * LANES, TQW), lambda b, g, i: (b, g, i)),
        out_shape=jax.ShapeDtypeStruct((b, D_B, s), BF16),
        scratch_shapes=(_stats_scratch(2 * hp, 2 * HEAD_DIM, TQW)
                        + [pltpu.VMEM((hp, DIAG_TILES, TP, TQW), F32)]),
        compiler_params=_cparams(3, VMEM_LIMIT),
        name="attn_b_prompt",
    )(q_t, k, v_t, key_extras, lam_vec, subln_col)


def _logf_scan_kernel(logf_ref, dqt_ref, e_ref):
    r = lax.broadcasted_iota(jnp.int32, (SCAN_T, SCAN_T), 0)
    c = lax.broadcasted_iota(jnp.int32, (SCAN_T, SCAN_T), 1)
    tri = jnp.where(r >= c, 1.0, 0.0).astype(BF16)
    pairs = H_C // 2
    i_sel = lax.broadcasted_iota(jnp.int32, (3 * LANES, pairs * LANES), 0)
    col = lax.broadcasted_iota(jnp.int32, (3 * LANES, pairs * LANES), 1)
    j_sel = col % LANES
    src = ((j_sel - 3) % 3) * LANES + 2 * (col // LANES) + (j_sel - 3) // 3
    sel = jnp.where((j_sel >= 3) & (j_sel < 9) & (i_sel == src), 1.0, 0.0).astype(BF16)
    ones_cols = jnp.where(lax.broadcasted_iota(jnp.int32, (1, pairs * LANES), 1) % LANES < 3, 1.0, 0.0)
    carry = jnp.zeros((1, LANES), F32)
    for t in range(logf_ref.shape[1] // SCAN_T):
        rows = slice(t * SCAN_T, (t + 1) * SCAN_T)
        hi, mid, lo = _split3(logf_ref[0, rows, :])
        cs = (_dot(tri, hi.astype(BF16)) + _dot(tri, mid.astype(BF16)) + _dot(tri, lo.astype(BF16))) + carry
        carry = cs[SCAN_T - 1:SCAN_T, :]
        x = cs * LOG2E
        dqt_ref[0, :, rows] = x.T[:H_C, :]
        hi, mid, lo = _split3(x)
        parts = jnp.concatenate([hi.astype(BF16), mid.astype(BF16), lo.astype(BF16)], axis=1)
        e_all = (_dot(parts, sel) + ones_cols).astype(BF16)
        for g in range(pairs):
            e_ref[0, g, rows, :] = e_all[:, g * LANES:(g + 1) * LANES]


def _logf_scan(logf):
    b, s, _ = logf.shape
    return pl.pallas_call(
        _logf_scan_kernel,
        grid=(b,),
        in_specs=[pl.BlockSpec((1, s, LANES), lambda b: (b, 0, 0))],
        out_specs=[pl.BlockSpec((1, H_C, s), lambda b: (b, 0, 0)),
                   pl.BlockSpec((1, H_C // 2, s, LANES), lambda b: (b, 0, 0, 0))],
        out_shape=[jax.ShapeDtypeStruct((b, H_C, s), F32),
                   jax.ShapeDtypeStruct((b, H_C // 2, s, LANES), BF16)],
        compiler_params=_cparams(1, VMEM_LIMIT),
        name="logf_scan",
    )(logf)


def _attn_c_prompt_kernel(qt_ref, k_ref, vt_ref, dqt_ref, e_ref, o_ref, m_ref, acc_ref, *, hp):
    qi = pl.program_id(2)
    row = lax.broadcasted_iota(jnp.int32, (LANES, TQW), 0)
    qa = []
    for p in range(hp):
        q_t = qt_ref[0, p * LANES:(p + 1) * LANES, :]
        dq = dqt_ref[0, p]
        for mi in range(2):
            hi, mid, lo = _split3(dq[mi:mi + 1, :])
            dk_rows = (row >= 3 + 3 * mi) & (row < 6 + 3 * mi)
            qe = jnp.where(row == 0, hi, jnp.where(row == 1, mid, jnp.where(row == 2, lo,
                           jnp.where(dk_rows, -1.0, 0.0))))
            qa.append(jnp.concatenate([_map_rows(q_t, mi), qe.astype(BF16)], axis=0))
    _stats_init(m_ref, acc_ref)

    def tile(j, diag):
        start = pl.multiple_of(j * TP, TP)
        scores, values = [], []
        for p in range(hp):
            cols = slice(p * LANES, (p + 1) * LANES)
            kx = jnp.concatenate([k_ref[0, pl.ds(start, TP), cols], e_ref[0, p, pl.ds(start, TP), :]],
                                 axis=1)
            v_t = vt_ref[0, cols, pl.ds(start, TP)]
            for mi in range(2):
                s_t = _dot(kx, qa[2 * p + mi])
                if diag is not None:
                    kr = lax.broadcasted_iota(jnp.int32, (TP, TQW), 0) + diag * TP
                    qc = lax.broadcasted_iota(jnp.int32, (TP, TQW), 1)
                    s_t = jnp.where(kr <= qc, s_t, NEG_INF)
                scores.append(s_t)
                values.append(_with_ones(v_t[mi * HEAD_DIM:(mi + 1) * HEAD_DIM]))
        _stats_update_all(m_ref, acc_ref, scores, values)

    def past_tile(j, carry):
        tile(j, None)
        return carry

    lax.fori_loop(0, DIAG_TILES * qi, past_tile, 0)
    for d in range(DIAG_TILES):
        tile(DIAG_TILES * qi + d, d)
    o = jnp.concatenate([_stats_result(acc_ref, i) for i in range(2 * hp)], axis=0)
    o_ref[0] = o.astype(o_ref.dtype)


def _attn_c_prompt(q_t, k, v_t, dq_t, key_extras):
    b, s, _ = k.shape
    hp = HP_C
    groups = D_C // (hp * LANES)
    dq_t = dq_t.reshape(b, D_C // LANES, 2, s)
    return pl.pallas_call(
        functools.partial(_attn_c_prompt_kernel, hp=hp),
        grid=(b, groups, s // TQW),
        in_specs=[pl.BlockSpec((1, hp * LANES, TQW), lambda b, g, i: (b, g, i)),
                  pl.BlockSpec((1, s, hp * LANES), lambda b, g, i: (b, 0, g)),
                  pl.BlockSpec((1, hp * LANES, s), lambda b, g, i: (b, g, 0)),
                  pl.BlockSpec((1, hp, 2, TQW), lambda b, g, i: (b, g, 0, i)),
                  pl.BlockSpec((1, hp, s, LANES), lambda b, g, i: (b, g, 0, 0))],
        out_specs=pl.BlockSpec((1, hp * LANES, TQW), lambda b, g, i: (b, g, i)),
        out_shape=jax.ShapeDtypeStruct((b, D_C, s), BF16),
        scratch_shapes=_stats_scratch(2 * hp, HEAD_DIM, TQW),
        compiler_params=_cparams(3, VMEM_LIMIT),
        name="attn_c_prompt",
    )(q_t, k, v_t, dq_t, key_extras)


def _split_maps(q):
    lane = lax.broadcasted_iota(jnp.int32, q.shape, q.ndim - 1)
    zero = jnp.zeros_like(q)
    return jnp.where(lane < HEAD_DIM, q, zero), jnp.where(lane >= HEAD_DIM, q, zero)


def _pair_heads(o0, o1):
    lane = lax.broadcasted_iota(jnp.int32, o0.shape, 1)
    return jnp.where(lane < HEAD_DIM, o0, o1)


def _diff_out(o0, o1, lam, subln, lam_init):
    o = o0 - lam * o1
    y = o * lax.rsqrt(jnp.mean(o * o, axis=-1, keepdims=True) + EPS)
    return (y * subln) * (1.0 - lam_init)


def _sample_scores(q, kn, kc, cache_t):
    q0, q1 = _split_maps(q)
    qs = jnp.concatenate([q0, q1], axis=0)
    s_c = _dot(qs, kc) if cache_t else _dot_nt(qs, kc)
    return s_c, _dot_nt(qs, kn)


def _sample_softmax_pv(s_c, s_n, vn, vc, cache_t):
    n = len(s_c)
    m = [jnp.maximum(jnp.max(s_c[i], axis=-1, keepdims=True), jnp.max(s_n[i], axis=-1, keepdims=True))
         for i in range(n)]
    p_c = [jnp.exp(s_c[i] - m[i]) for i in range(n)]
    p_n = [jnp.exp(s_n[i] - m[i]) for i in range(n)]
    l = [jnp.sum(p_c[i], axis=-1, keepdims=True) + jnp.sum(p_n[i], axis=-1, keepdims=True)
         for i in range(n)]
    outs = []
    for i in range(n):
        pc = p_c[i].astype(BF16)
        o_c = _dot_nt(pc, vc[i]) if cache_t else _dot(pc, vc[i])
        o = (o_c + _dot(p_n[i].astype(BF16), vn[i])) * (1.0 / l[i])
        t = o.shape[0] // 2
        outs.append((o[:t], o[t:]))
    return outs


def _cache_pair_t(ref, p):
    blk = ref[0, 0, 2 * p:2 * p + 2]
    return blk.reshape(2 * HEAD_DIM, blk.shape[-1]).astype(BF16)


def _attn_a_sample_kernel(q_ref, kn_ref, vn_ref, kc_ref, vc_ref, bias_ref, o_ref, *, hp):
    t = q_ref.shape[1]
    w = kc_ref.shape[-1]
    s_c, s_n, vn, vc = [], [], [], []
    for p in range(hp):
        cols = slice(p * LANES, (p + 1) * LANES)
        sc, sn = _sample_scores(q_ref[0, :, cols], kn_ref[0, :, cols], _cache_pair_t(kc_ref, p), True)
        bias = bias_ref[2 * p:2 * p + 2].reshape(2 * t, SBIAS_W)
        s_c.append(sc + bias[:, :w])
        s_n.append(sn + bias[:, w:w + t])
        vn.append(vn_ref[0, :, cols])
        vc.append(_cache_pair_t(vc_ref, p))
    outs = _sample_softmax_pv(s_c, s_n, vn, vc, True)
    for p in range(hp):
        o_ref[0, :, p * LANES:(p + 1) * LANES] = _pair_heads(*outs[p]).astype(o_ref.dtype)


def _attn_c_sample_kernel(q_ref, kn_ref, vn_ref, kc_ref, vc_ref, logf_ref, clogf_ref, o_ref,
                          dpast_ref, *, hp):
    g = pl.program_id(1)
    t = q_ref.shape[1]
    past = clogf_ref.shape[-1]

    @pl.when(g == 0)
    def _():
        r = lax.broadcasted_iota(jnp.int32, (SCAN_T, SCAN_T), 0)
        c = lax.broadcasted_iota(jnp.int32, (SCAN_T, SCAN_T), 1)
        triu = jnp.where(r <= c, 1.0, 0.0).astype(F32)
        carry = jnp.zeros((H_C, 1), F32)
        for i in range(past // SCAN_T):
            cols = slice(i * SCAN_T, (i + 1) * SCAN_T)
            cs = _dot_exact(clogf_ref[0, 0, :, cols], triu) + carry
            dpast_ref[:, cols] = cs
            carry = cs[:, SCAN_T - 1:SCAN_T]
        dpast_ref[...] = dpast_ref[...] - carry

    r = lax.broadcasted_iota(jnp.int32, (t, t), 0)
    c = lax.broadcasted_iota(jnp.int32, (t, t), 1)
    tri = jnp.where(r >= c, 1.0, 0.0).astype(F32)
    dq = _dot_exact(tri, logf_ref[0])
    er = lax.broadcasted_iota(jnp.int32, (LANES, LANES), 0)
    ec = lax.broadcasted_iota(jnp.int32, (LANES, LANES), 1)
    eye = jnp.where(er == ec, 1.0, 0.0).astype(F32)
    dq_t = _dot_nt_exact(eye, dq)
    dpast = dpast_ref[...]
    top = lax.broadcasted_iota(jnp.int32, (2 * t, 1), 0) < t
    qrow = lax.broadcasted_iota(jnp.int32, (2 * t, t), 0) % t
    kcol = lax.broadcasted_iota(jnp.int32, (2 * t, t), 1)
    s_c, s_n, vn, vc = [], [], [], []
    for p in range(hp):
        cols = slice(p * LANES, (p + 1) * LANES)
        h0 = 2 * (g * hp + p)
        dq_col = jnp.concatenate([_select_lane(dq, h0), _select_lane(dq, h0 + 1)], axis=0)
        dk_c = jnp.where(top, _select_row(dpast, h0), _select_row(dpast, h0 + 1))
        dk_n = jnp.where(top, _select_row(dq_t, h0), _select_row(dq_t, h0 + 1))
        sc, sn = _sample_scores(q_ref[0, :, cols], kn_ref[0, :, cols], _cache_pair_t(kc_ref, p), True)
        s_c.append(sc + dq_col - dk_c)
        s_n.append(jnp.where(kcol <= qrow, sn + dq_col - dk_n, NEG_INF))
        vn.append(vn_ref[0, :, cols])
        vc.append(_cache_pair_t(vc_ref, p))
    outs = _sample_softmax_pv(s_c, s_n, vn, vc, True)
    for p in range(hp):
        o_ref[0, :, p * LANES:(p + 1) * LANES] = _pair_heads(*outs[p]).astype(o_ref.dtype)


def _attn_pair_sample(kernel_fn, name, layer, qkv, cache_k_t, cache_v_t, extras, extra_specs,
                      scratch=()):
    b, t, _ = qkv.shape
    heads, _, past = cache_k_t.shape[2:]
    hp = HP_SAMPLE
    groups = heads // (2 * hp)
    width = hp * LANES
    cache_spec = pl.BlockSpec((1, 1, 2 * hp, HEAD_DIM, past), lambda b, g: (layer, b, g, 0, 0))
    return pl.pallas_call(
        functools.partial(kernel_fn, hp=hp),
        grid=(b, groups),
        in_specs=[pl.BlockSpec((1, t, width), lambda b, g: (b, 0, g)),
                  pl.BlockSpec((1, t, width), lambda b, g: (b, 0, groups + g)),
                  pl.BlockSpec((1, t, width), lambda b, g: (b, 0, 2 * groups + g)),
                  cache_spec, cache_spec] + list(extra_specs),
        out_specs=pl.BlockSpec((1, t, width), lambda b, g: (b, 0, g)),
        out_shape=jax.ShapeDtypeStruct((b, t, heads * HEAD_DIM), BF16),
        scratch_shapes=list(scratch),
        compiler_params=_cparams(2, VMEM_LIMIT),
        name=name,
    )(qkv, qkv, qkv, cache_k_t, cache_v_t, *extras)


def _attn_b_sample_kernel(q_ref, kn_ref, vn_ref, kc_ref, vc_ref, lam_ref, subln_ref, o_ref,
                          *, lam_init):
    t = q_ref.shape[1]
    past = kc_ref.shape[2] // H_B
    shape_c = (2 * t, past)
    shape_n = (2 * t, t)
    qpos_c = past + lax.broadcasted_iota(jnp.int32, shape_c, 0) % t
    dist_c = jnp.abs(qpos_c - lax.broadcasted_iota(jnp.int32, shape_c, 1)).astype(F32)
    dist_n = jnp.abs(lax.broadcasted_iota(jnp.int32, shape_n, 0) % t
                     - lax.broadcasted_iota(jnp.int32, shape_n, 1)).astype(F32)
    lam = _diff_lambda(lam_ref, lam_init)
    s_c, s_n, vn, vc = [], [], [], []
    for h in range(H_B):
        cols = slice(h * LANES, (h + 1) * LANES)
        slope = 2.0 ** (-8.0 * (h + 1) / H_B)
        kc = kc_ref[0, 0, pl.ds(h, past, stride=H_B), :].astype(BF16)
        sc, sn = _sample_scores(q_ref[0, :, cols], kn_ref[0, :, cols], kc, False)
        s_c.append(sc - slope * dist_c)
        s_n.append(sn - slope * dist_n)
        vn.append(vn_ref[0, :, cols])
        vc.append(vc_ref[0, 0, pl.ds(h, past, stride=H_B), :].astype(BF16))
    outs = _sample_softmax_pv(s_c, s_n, vn, vc, False)
    for h in range(H_B):
        o_ref[0, :, h * LANES:(h + 1) * LANES] = _diff_out(
            outs[h][0], outs[h][1], lam, subln_ref[...], lam_init).astype(o_ref.dtype)


def _attn_b_sample(layer, qkv, cache_k, cache_v, lam_vec, subln, lam_init):
    b, t, _ = qkv.shape
    rows = cache_k.shape[2]
    col0 = 3 * D_A // D_B
    cache_spec = pl.BlockSpec((1, 1, rows, LANES), lambda b: (layer, b, 0, 0))
    return pl.pallas_call(
        functools.partial(_attn_b_sample_kernel, lam_init=lam_init),
        grid=(b,),
        in_specs=[pl.BlockSpec((1, t, D_B), lambda b: (b, 0, col0)),
                  pl.BlockSpec((1, t, D_B), lambda b: (b, 0, col0 + 1)),
                  pl.BlockSpec((1, t, D_B), lambda b: (b, 0, col0 + 2)),
                  cache_spec, cache_spec, _full(lam_vec.shape), _full(subln.shape)],
        out_specs=pl.BlockSpec((1, t, D_B), lambda b: (b, 0, 0)),
        out_shape=jax.ShapeDtypeStruct((b, t, D_B), BF16),
        compiler_params=_cparams(1, VMEM_LIMIT),
        name="attn_b_sample",
    )(qkv, qkv, qkv, cache_k, cache_v, lam_vec, subln)


def _heads_last(a_t, heads):
    lead = a_t.shape[:-2]
    s = a_t.shape[-1]
    a = a_t.reshape(lead + (heads, HEAD_DIM, s))
    n = len(lead)
    return jnp.transpose(a, tuple(range(n)) + (n + 2, n, n + 1))


def kernel(x_prompt, x_sample, c_prompt, c_sample, cache_a_k, cache_a_v, cache_b_k, cache_b_v, cache_c_k, cache_c_v, cache_c_logf, w_ada, b_ada, norm_pre, norm_post, ffn_w_gate, ffn_w_up, ffn_w_down, w_in_ab, w_out_ab, relpos_a, lambda_b, subln_b, w_in_c, b_f, w_out_c):
    n_prompt, s_prompt = x_prompt.shape[:2]
    n_sample, t_sample = x_sample.shape[:2]

    wg = ffn_w_gate.astype(BF16)
    wu = ffn_w_up.astype(BF16)
    wd = ffn_w_down.astype(BF16)
    w_in_ab16 = w_in_ab.astype(BF16)
    w_in_ab16_t = jnp.swapaxes(w_in_ab, 1, 2).astype(BF16)
    w_out_ab16 = w_out_ab.astype(BF16)
    w_out_ab16_t = jnp.swapaxes(w_out_ab, 1, 2).astype(BF16)
    w_qkv_c16 = w_in_c[:, :, :3 * D_C].astype(BF16)
    w_qkv_c16_t = jnp.swapaxes(w_in_c[:, :, :3 * D_C], 1, 2).astype(BF16)
    w_f16 = jnp.pad(w_in_c[:, :, 3 * D_C:], ((0, 0), (0, 0), (0, LANES - H_C))).astype(BF16)
    w_f16_t = jnp.swapaxes(w_f16, 1, 2)
    b_f_row = jnp.pad(b_f, ((0, 0), (0, LANES - H_C)))[:, None, :]
    b_f_col = jnp.swapaxes(b_f_row, 1, 2)
    w_out_c16 = w_out_c.astype(BF16)
    w_out_c16_t = jnp.swapaxes(w_out_c, 1, 2).astype(BF16)
    cache_a_k_t = jnp.transpose(cache_a_k, (0, 1, 3, 4, 2))
    cache_a_v_t = jnp.transpose(cache_a_v, (0, 1, 3, 4, 2))
    cache_c_k_t = jnp.transpose(cache_c_k, (0, 1, 3, 4, 2))
    cache_c_v_t = jnp.transpose(cache_c_v, (0, 1, 3, 4, 2))
    clogf_t = jnp.swapaxes(cache_c_logf, -1, -2)
    past_b = cache_b_k.shape[2]
    cache_b_k_rows = cache_b_k.reshape(cache_b_k.shape[:2] + (past_b * H_B, 2 * HEAD_DIM))
    cache_b_v_rows = cache_b_v.reshape(cache_b_v.shape[:2] + (past_b * H_B, 2 * HEAD_DIM))

    rows = n_prompt + n_sample
    rows_pad = -(-rows // 8) * 8
    c_all = jnp.concatenate([c_prompt, c_sample, jnp.zeros((rows_pad - rows, D_MODEL), F32)], axis=0)
    mod_all = _ada(c_all, w_ada, b_ada).reshape(DEPTH, rows_pad, N_SUB, 3, D_MODEL)
    pbias, sbias = _relpos_bias(relpos_a, t_sample)

    def sublayer_params(li, row0, nrows):
        mod = [mod_all[li, row0:row0 + nrows, i] for i in range(N_SUB)]
        gpre = [norm_pre[li, i][None, :] for i in range(N_SUB)]
        gpost = [norm_post[li, i][None, :] for i in range(N_SUB)]
        return mod, gpre, gpost

    def run_prompt(x):
        a_k, a_v, b_k, b_v, c_k, c_v, c_lf = [], [], [], [], [], [], []
        for li in range(DEPTH):
            mod, gpre, gpost = sublayer_params(li, 0, n_prompt)
            j = li // 2
            if li % 2 == 0:
                lam_init = _lam_init(li)
                x, q_t, k, v_t, ka_t, va_t, kb, vb = _ffn_inproj_ab_prompt(
                    x, mod[0], mod[1], gpre[0], gpost[0], wg, wu, wd, li, gpre[1],
                    w_in_ab16[j], w_in_ab16_t[j])
                out_a = _attn_a_prompt(q_t, k, v_t, pbias[j])
                out_b = _attn_b_prompt(q_t, k, v_t, lambda_b[j], subln_b[j][:, None], lam_init)
                w = min(A_PAST, s_prompt)
                a_k.append(ka_t[:, :, s_prompt - w:])
                a_v.append(va_t[:, :, s_prompt - w:])
                b_k.append(kb)
                b_v.append(vb)
                outs, w_outs = [out_a, out_b], [w_out_ab16_t[j, :, :D_A], w_out_ab16_t[j, :, D_A:]]
            else:
                x, q_t, k, v_t, k_t32, v_t32, logf, logf_t = _ffn_inproj_c_prompt(
                    x, mod[0], mod[1], gpre[0], gpost[0], wg, wu, wd, li, gpre[1],
                    w_qkv_c16[j, :, D_C:2 * D_C], w_qkv_c16_t[j],
                    w_f16[j], w_f16_t[j], b_f_row[j], b_f_col[j])
                dq_t, key_extras = _logf_scan(logf)
                outs, w_outs = [_attn_c_prompt(q_t, k, v_t, dq_t, key_extras)], [w_out_c16_t[j]]
                c_k.append(k_t32)
                c_v.append(v_t32)
                c_lf.append(logf_t)
            x = _outproj_prompt(x, mod[1], gpost[1], outs, w_outs)
            x = _ffn(x, mod[2], gpre[2], gpost[2], wg, wu, wd, li, 1)
        nb, sb = x.shape[:2]
        return (x,
                _heads_last(jnp.stack(a_k), H_A), _heads_last(jnp.stack(a_v), H_A),
                jnp.stack(b_k).reshape(-1, nb, sb, H_B, 2 * HEAD_DIM),
                jnp.stack(b_v).reshape(-1, nb, sb, H_B, 2 * HEAD_DIM),
                _heads_last(jnp.stack(c_k), H_C), _heads_last(jnp.stack(c_v), H_C),
                jnp.swapaxes(jnp.stack(c_lf), -1, -2))

    def run_sample(x):
        a_k, a_v, b_k, b_v, c_k, c_v, c_lf = [], [], [], [], [], [], []
        for li in range(DEPTH):
            mod, gpre, gpost = sublayer_params(li, n_prompt, n_sample)
            x = _ffn(x, mod[0], gpre[0], gpost[0], wg, wu, wd, li, 0)
            j = li // 2
            if li % 2 == 0:
                lam_init = _lam_init(li)
                qkv, ka, va, kb, vb = _inproj_ab(x, mod[1], gpre[1], w_in_ab16[j])
                subln = subln_b[j][None, :]
                out_a = _attn_pair_sample(
                    _attn_a_sample_kernel, "attn_a_sample", j, qkv, cache_a_k_t, cache_a_v_t,
                    [sbias[j]],
                    [pl.BlockSpec((2 * HP_SAMPLE, t_sample, SBIAS_W), lambda b, g: (g, 0, 0))])
                out_b = _attn_b_sample(j, qkv, cache_b_k_rows, cache_b_v_rows, lambda_b[j], subln,
                                       lam_init)
                a_k.append(ka)
                a_v.append(va)
                b_k.append(kb)
                b_v.append(vb)
                x = _outproj(x, mod[1], gpost[1], [out_a, out_b],
                             [w_out_ab16[j, :D_A], w_out_ab16[j, D_A:]])
            else:
                qkv, k, v, logf = _inproj_c(x, mod[1], gpre[1], w_qkv_c16[j], w_f16[j], b_f_row[j])
                past = clogf_t.shape[-1]
                out = _attn_pair_sample(
                    _attn_c_sample_kernel, "attn_c_sample", j, qkv, cache_c_k_t, cache_c_v_t,
                    [logf, clogf_t],
                    [pl.BlockSpec((1, t_sample, LANES), lambda b, g: (b, 0, 0)),
                     pl.BlockSpec((1, 1, H_C, past), lambda b, g, j=j: (j, b, 0, 0))],
                    scratch=[pltpu.VMEM((H_C, past), F32)])
                c_k.append(k)
                c_v.append(v)
                c_lf.append(logf[:, :, :H_C])
                x = _outproj(x, mod[1], gpost[1], [out], [w_out_c16[j]])
            x = _ffn(x, mod[2], gpre[2], gpost[2], wg, wu, wd, li, 1)
        nb, sb = x.shape[:2]
        return (x,
                jnp.stack(a_k).reshape(-1, nb, sb, H_A, HEAD_DIM),
                jnp.stack(a_v).reshape(-1, nb, sb, H_A, HEAD_DIM),
                jnp.stack(b_k).reshape(-1, nb, sb, H_B, 2 * HEAD_DIM),
                jnp.stack(b_v).reshape(-1, nb, sb, H_B, 2 * HEAD_DIM),
                jnp.stack(c_k).reshape(-1, nb, sb, H_C, HEAD_DIM),
                jnp.stack(c_v).reshape(-1, nb, sb, H_C, HEAD_DIM),
                jnp.stack(c_lf))

    p = run_prompt(x_prompt)
    s = run_sample(x_sample)
    return (p[0], s[0]) + tuple(p[1:]) + tuple(s[1:])
```

```python
import functools
import math

import jax
import jax.numpy as jnp
from jax import lax
from jax.experimental import pallas as pl
from jax.experimental.pallas import tpu as pltpu

D_MODEL = 1024
DEPTH = 4
CHUNK = 64
HEAD_DIM = 64
H_A = 8
H_B = 4
H_C = 16
D_A = H_A * HEAD_DIM
D_B = H_B * 2 * HEAD_DIM
D_C = H_C * HEAD_DIM
A_LEFT_CHUNKS = 8
A_PAST = A_LEFT_CHUNKS * CHUNK
REL_CLIP = 128
D_FF = 2816
N_SUB = 3
MACARON_W = 0.5
EPS = 1e-6
NEG_INF = -1e30
ATTN_SCALE = HEAD_DIM ** -0.5
LOG2E = 1.4426950408889634
QSCALE_LOG2 = ATTN_SCALE * LOG2E

LANES = 128
TOK_TILE = 256
WIDE_TOK_TILE = 512
SAMPLE_GROUP = 8
TP = 256
SUM_ROWS = 16
DIAG_TILES = 2
TQW = DIAG_TILES * TP
HP_A = 4
HP_B = 4
HP_C = 4
HP_SAMPLE = 4
BAND_TILES = 3
BAND_A = BAND_TILES * TP
SCAN_T = 512
TAB_PAD = 384
TOEP = 1024
SBIAS_W = 640
ADA_TN = 1536
VMEM_LIMIT = 56 * 1024 * 1024

BF16 = jnp.bfloat16
F32 = jnp.float32


def _lam_init(li):
    return 0.8 - 0.6 * math.exp(-0.3 * li)


def _cparams(n_axes, vmem=None):
    return pltpu.CompilerParams(dimension_semantics=("arbitrary",) * n_axes,
                                vmem_limit_bytes=vmem)


def _dot(a, b):
    return jnp.dot(a, b, preferred_element_type=F32)


def _dot_nt(a, b):
    return lax.dot_general(a, b, (((1,), (1,)), ((), ())), preferred_element_type=F32)


def _dot_exact(a, b):
    return jnp.dot(a, b, preferred_element_type=F32, precision=lax.Precision.HIGHEST)


def _dot_nt_exact(a, b):
    return lax.dot_general(a, b, (((1,), (1,)), ((), ())), preferred_element_type=F32,
                           precision=lax.Precision.HIGHEST)


def _split3(x):
    hi = x.astype(BF16).astype(F32)
    r1 = x - hi
    mid = r1.astype(BF16).astype(F32)
    lo = (r1 - mid).astype(BF16).astype(F32)
    return hi, mid, lo


def _norm_mod(x, m, g_pre):
    y = x * lax.rsqrt(jnp.mean(x * x, axis=-1, keepdims=True) + EPS)
    return (y * g_pre) * (1.0 + m[:, 1:2, :]) + m[:, 0:1, :]


def _gated_residual(x, m, g_post, out, res_w):
    y = out * lax.rsqrt(jnp.mean(out * out, axis=-1, keepdims=True) + EPS)
    return x + (res_w * (1.0 + m[:, 2:3, :])) * (y * g_post)


def _tok_layout(x, tile=TOK_TILE):
    nb_total, sb_total, _ = x.shape
    if sb_total >= tile:
        nb, sb = 1, tile
    else:
        nb, sb = SAMPLE_GROUP, sb_total
    tiles = sb_total // sb
    grid = (nb_total // nb) * tiles

    def spec(width, rows_per_token=1):
        return pl.BlockSpec((nb, sb * rows_per_token, width), lambda i: (i // tiles, i % tiles, 0))

    def spec_t(width):
        return pl.BlockSpec((nb, width, sb), lambda i: (i // tiles, 0, i % tiles))

    mod_spec = pl.BlockSpec((nb, 3, D_MODEL), lambda i: (i // tiles, 0, 0))
    return nb, sb, grid, spec, spec_t, mod_spec


def _full(shape):
    return pl.BlockSpec(shape, lambda *_: (0,) * len(shape), pipeline_mode=pl.Buffered(1))


def _ada_kernel(c_ref, w_ref, b_ref, o_ref):
    c = c_ref[...]
    a = (c * jax.nn.sigmoid(c)).astype(BF16)
    o_ref[0] = _dot(a, w_ref[0].astype(BF16)) + b_ref[0]


def _ada(c_all, w_ada, b_ada):
    rows = c_all.shape[0]
    n_out = w_ada.shape[-1]
    return pl.pallas_call(
        _ada_kernel,
        grid=(DEPTH, n_out // ADA_TN),
        in_specs=[pl.BlockSpec((rows, D_MODEL), lambda l, n: (0, 0)),
                  pl.BlockSpec((1, D_MODEL, ADA_TN), lambda l, n: (l, 0, n)),
                  pl.BlockSpec((1, 1, ADA_TN), lambda l, n: (l, 0, n))],
        out_specs=pl.BlockSpec((1, rows, ADA_TN), lambda l, n: (l, 0, n)),
        out_shape=jax.ShapeDtypeStruct((DEPTH, rows, n_out), F32),
        compiler_params=_cparams(2, VMEM_LIMIT),
        name="ada",
    )(c_all, w_ada, b_ada.reshape(DEPTH, 1, n_out))


def _ffn_body(x, m, g_pre, g_post, wg_ref, wu_ref, wd_ref):
    nb, sb, d = x.shape
    h = _norm_mod(x, m, g_pre).reshape(nb * sb, d).astype(BF16)
    g = _dot(h, wg_ref[0, 0])
    u = _dot(h, wu_ref[0, 0])
    a = ((g * jax.nn.sigmoid(g)) * u).astype(BF16)
    out = _dot(a, wd_ref[0, 0]).reshape(nb, sb, d)
    return _gated_residual(x, m, g_post, out, MACARON_W)


def _ffn_weight_specs(li, k):
    def one(rows, cols):
        return pl.BlockSpec((1, 1, rows, cols), lambda *_: (li, k, 0, 0), pipeline_mode=pl.Buffered(1))
    return [one(D_MODEL, D_FF), one(D_MODEL, D_FF), one(D_FF, D_MODEL)]


def _ffn_kernel(x_ref, mod_ref, gpre_ref, gpost_ref, wg_ref, wu_ref, wd_ref, o_ref):
    o_ref[...] = _ffn_body(x_ref[...], mod_ref[...], gpre_ref[...], gpost_ref[...],
                           wg_ref, wu_ref, wd_ref)


def _ffn(x, mod, g_pre, g_post, wg, wu, wd, li, k):
    nb, sb, grid, spec, _, mod_spec = _tok_layout(x, WIDE_TOK_TILE)
    return pl.pallas_call(
        _ffn_kernel,
        grid=(grid,),
        in_specs=[spec(D_MODEL), mod_spec, _full((1, D_MODEL)), _full((1, D_MODEL))]
        + _ffn_weight_specs(li, k),
        out_specs=spec(D_MODEL),
        out_shape=jax.ShapeDtypeStruct(x.shape, F32),
        compiler_params=_cparams(1, VMEM_LIMIT),
        name="ffn",
    )(x, mod, g_pre, g_post, wg, wu, wd)


def _inproj_ab_kernel(x_ref, mod_ref, gpre_ref, w_ref, qkv_ref, ka_ref, va_ref, kb_ref, vb_ref):
    x = x_ref[...]
    nb, sb, d = x.shape
    h = _norm_mod(x, mod_ref[...], gpre_ref[...]).reshape(nb * sb, d).astype(BF16)
    f32_outs = {1: ka_ref, 2: va_ref, 4: kb_ref, 5: vb_ref}
    for c in range(6):
        cols = slice(c * D_A, (c + 1) * D_A)
        p = _dot(h, w_ref[:, cols]).reshape(nb, sb, D_A)
        if c in f32_outs:
            f32_outs[c][...] = p
            qkv_ref[:, :, cols] = p.astype(BF16)
        else:
            qkv_ref[:, :, cols] = (p * ATTN_SCALE).astype(BF16)


def _inproj_ab(x, mod, g_pre, w_in):
    nb, sb, grid, spec, _, mod_spec = _tok_layout(x)
    nbt, sbt, _ = x.shape
    f32_out = jax.ShapeDtypeStruct((nbt, sbt, D_A), F32)
    return pl.pallas_call(
        _inproj_ab_kernel,
        grid=(grid,),
        in_specs=[spec(D_MODEL), mod_spec, _full((1, D_MODEL)), _full((D_MODEL, 6 * D_A))],
        out_specs=[spec(6 * D_A)] + [spec(D_A)] * 4,
        out_shape=[jax.ShapeDtypeStruct((nbt, sbt, 6 * D_A), BF16)] + [f32_out] * 4,
        compiler_params=_cparams(1, VMEM_LIMIT),
        name="inproj_ab",
    )(x, mod, g_pre, w_in)


def _log_sigmoid(z):
    return jnp.minimum(z, 0.0) - jnp.log1p(jnp.exp(-jnp.abs(z)))


def _inproj_c_kernel(x_ref, mod_ref, gpre_ref, w_ref, wf_ref, bf_ref,
                     qkv_ref, k_ref, v_ref, logf_ref):
    x = x_ref[...]
    nb, sb, d = x.shape
    h = _norm_mod(x, mod_ref[...], gpre_ref[...]).reshape(nb * sb, d).astype(BF16)
    f32_outs = {1: k_ref, 2: v_ref}
    for c in range(3):
        cols = slice(c * D_C, (c + 1) * D_C)
        p = _dot(h, w_ref[:, cols]).reshape(nb, sb, D_C)
        if c in f32_outs:
            f32_outs[c][...] = p
            qkv_ref[:, :, cols] = p.astype(BF16)
        else:
            qkv_ref[:, :, cols] = (p * ATTN_SCALE).astype(BF16)
    logf = _log_sigmoid(_dot(h, wf_ref[...]) + bf_ref[...])
    logf_ref[...] = logf.reshape(nb, sb, LANES)


def _inproj_c(x, mod, g_pre, w_qkv, w_f, b_f):
    nb, sb, grid, spec, _, mod_spec = _tok_layout(x)
    nbt, sbt, _ = x.shape
    f32_out = jax.ShapeDtypeStruct((nbt, sbt, D_C), F32)
    return pl.pallas_call(
        _inproj_c_kernel,
        grid=(grid,),
        in_specs=[spec(D_MODEL), mod_spec, _full((1, D_MODEL)), _full((D_MODEL, 3 * D_C)),
                  _full((D_MODEL, LANES)), _full((1, LANES))],
        out_specs=[spec(3 * D_C), spec(D_C), spec(D_C), spec(LANES)],
        out_shape=[jax.ShapeDtypeStruct((nbt, sbt, 3 * D_C), BF16), f32_out, f32_out,
                   jax.ShapeDtypeStruct((nbt, sbt, LANES), F32)],
        compiler_params=_cparams(1, VMEM_LIMIT),
        name="inproj_c",
    )(x, mod, g_pre, w_qkv, w_f, b_f)


def _outproj_kernel(*refs, n_in):
    x_ref, mod_ref, gpost_ref = refs[:3]
    a_refs = refs[3:3 + n_in]
    w_refs = refs[3 + n_in:3 + 2 * n_in]
    o_ref = refs[-1]
    x = x_ref[...]
    nb, sb, d = x.shape
    out = None
    for a_ref, w_ref in zip(a_refs, w_refs):
        a = a_ref[...]
        t = _dot(a.reshape(nb * sb, a.shape[-1]), w_ref[...])
        out = t if out is None else out + t
    o_ref[...] = _gated_residual(x, mod_ref[...], gpost_ref[...], out.reshape(nb, sb, d), 1.0)


def _outproj(x, mod, g_post, acts, weights):
    nb, sb, grid, spec, _, mod_spec = _tok_layout(x)
    n_in = len(acts)
    return pl.pallas_call(
        functools.partial(_outproj_kernel, n_in=n_in),
        grid=(grid,),
        in_specs=([spec(D_MODEL), mod_spec, _full((1, D_MODEL))]
                  + [spec(a.shape[-1]) for a in acts]
                  + [_full(w.shape) for w in weights]),
        out_specs=spec(D_MODEL),
        out_shape=jax.ShapeDtypeStruct(x.shape, F32),
        compiler_params=_cparams(1, VMEM_LIMIT),
        name="outproj",
    )(x, mod, g_post, *acts, *weights)


def _inproj_ab_prompt_kernel(x_ref, mod0_ref, mod_ref, gpre0_ref, gpost0_ref, wg_ref, wu_ref, wd_ref,
                             gpre_ref, w_ref, wt_ref,
                             xo_ref, qt_ref, k_ref, vt_ref, kat_ref, vat_ref, kb_ref, vb_ref):
    x = _ffn_body(x_ref[...], mod0_ref[...], gpre0_ref[...], gpost0_ref[...], wg_ref, wu_ref, wd_ref)
    xo_ref[...] = x
    _, sb, d = x.shape
    h = _norm_mod(x, mod_ref[...], gpre_ref[...]).reshape(sb, d).astype(BF16)

    def nat(c):
        return _dot(h, w_ref[:, c * D_A:(c + 1) * D_A])

    def tr(c):
        return _dot_nt(wt_ref[c * D_A:(c + 1) * D_A, :], h)

    qt_ref[0, :D_A, :] = (tr(0) * QSCALE_LOG2).astype(BF16)
    qt_ref[0, D_A:, :] = (tr(3) * QSCALE_LOG2).astype(BF16)
    ka_t = tr(1)
    kat_ref[0] = ka_t
    k_ref[0, :, :D_A] = ka_t.T.astype(BF16)
    kb = nat(4)
    k_ref[0, :, D_A:] = kb.astype(BF16)
    va_t = tr(2)
    vat_ref[0] = va_t
    vt_ref[0, :D_A, :] = va_t.astype(BF16)
    vb = nat(5)
    vt_ref[0, D_A:, :] = vb.T.astype(BF16)
    for hd in range(H_B):
        kb_ref[0, pl.ds(hd, sb, stride=H_B), :] = kb[:, hd * LANES:(hd + 1) * LANES]
        vb_ref[0, pl.ds(hd, sb, stride=H_B), :] = vb[:, hd * LANES:(hd + 1) * LANES]


def _ffn_inproj_ab_prompt(x, mod0, mod1, g_pre0, g_post0, wg, wu, wd, li, g_pre1, w_in, w_in_t):
    nb, sb, grid, spec, spec_t, mod_spec = _tok_layout(x)
    b, s, _ = x.shape
    width = D_A + D_B
    vec = _full((1, D_MODEL))
    return pl.pallas_call(
        _inproj_ab_prompt_kernel,
        grid=(grid,),
        in_specs=[spec(D_MODEL), mod_spec, mod_spec, vec, vec] + _ffn_weight_specs(li, 0)
        + [vec, _full(w_in.shape), _full(w_in_t.shape)],
        out_specs=[spec(D_MODEL), spec_t(width), spec(width), spec_t(width), spec_t(D_A), spec_t(D_A),
                   spec(LANES, H_B), spec(LANES, H_B)],
        out_shape=[jax.ShapeDtypeStruct(x.shape, F32),
                   jax.ShapeDtypeStruct((b, width, s), BF16),
                   jax.ShapeDtypeStruct((b, s, width), BF16),
                   jax.ShapeDtypeStruct((b, width, s), BF16),
                   jax.ShapeDtypeStruct((b, D_A, s), F32),
                   jax.ShapeDtypeStruct((b, D_A, s), F32),
                   jax.ShapeDtypeStruct((b, s * H_B, LANES), F32),
                   jax.ShapeDtypeStruct((b, s * H_B, LANES), F32)],
        compiler_params=_cparams(1, VMEM_LIMIT),
        name="ffn_inproj_ab_prompt",
    )(x, mod0, mod1, g_pre0, g_post0, wg, wu, wd, g_pre1, w_in, w_in_t)


def _inproj_c_prompt_kernel(x_ref, mod0_ref, mod_ref, gpre0_ref, gpost0_ref, wg_ref, wu_ref, wd_ref,
                            gpre_ref, wk_ref, wt_ref, wf_ref, wft_ref, bf_ref, bft_ref,
                            kt32_in_ref, vt32_in_ref,
                            xo_ref, qt_ref, k_ref, vt_ref, kt32_ref, vt32_ref, logf_ref, logft_ref):
    del kt32_in_ref, vt32_in_ref
    x = _ffn_body(x_ref[...], mod0_ref[...], gpre0_ref[...], gpost0_ref[...], wg_ref, wu_ref, wd_ref)
    xo_ref[...] = x
    _, sb, d = x.shape
    h = _norm_mod(x, mod_ref[...], gpre_ref[...]).reshape(sb, d).astype(BF16)
    qt_ref[0] = (_dot_nt(wt_ref[:D_C, :], h) * QSCALE_LOG2).astype(BF16)
    k_t = _dot_nt(wt_ref[D_C:2 * D_C, :], h)
    kt32_ref[0, 0] = k_t
    k_ref[0] = k_t.T.astype(BF16)
    v_t = _dot_nt(wt_ref[2 * D_C:, :], h)
    vt32_ref[0, 0] = v_t
    vt_ref[0] = v_t.astype(BF16)
    logf_ref[0] = _log_sigmoid(_dot(h, wf_ref[...]) + bf_ref[...])
    logft_ref[0] = _log_sigmoid(_dot_nt(wft_ref[...], h) + bft_ref[...])[:H_C, :]


def _ffn_inproj_c_prompt(x, mod0, mod1, g_pre0, g_post0, wg, wu, wd, li, g_pre1,
                         w_k, w_qkv_t, w_f, w_f_t, b_f, b_f_t, kt_stack, vt_stack, j):
    nb, sb, grid, spec, spec_t, mod_spec = _tok_layout(x)
    b, s, _ = x.shape
    tiles = s // sb
    vec = _full((1, D_MODEL))
    stack_spec = pl.BlockSpec((1, nb, D_C, sb), lambda i: (j, i // tiles, 0, i % tiles))
    in_arrays = (x, mod0, mod1, g_pre0, g_post0, wg, wu, wd, g_pre1, w_k, w_qkv_t, w_f, w_f_t, b_f, b_f_t,
                 kt_stack, vt_stack)
    return pl.pallas_call(
        _inproj_c_prompt_kernel,
        grid=(grid,),
        in_specs=[spec(D_MODEL), mod_spec, mod_spec, vec, vec] + _ffn_weight_specs(li, 0)
        + [vec, _full(w_k.shape), _full(w_qkv_t.shape), _full(w_f.shape), _full(w_f_t.shape),
           _full(b_f.shape), _full(b_f_t.shape),
           pl.BlockSpec(memory_space=pl.ANY), pl.BlockSpec(memory_space=pl.ANY)],
        out_specs=[spec(D_MODEL), spec_t(D_C), spec(D_C), spec_t(D_C), stack_spec, stack_spec,
                   spec(LANES), spec_t(H_C)],
        out_shape=[jax.ShapeDtypeStruct(x.shape, F32),
                   jax.ShapeDtypeStruct((b, D_C, s), BF16),
                   jax.ShapeDtypeStruct((b, s, D_C), BF16),
                   jax.ShapeDtypeStruct((b, D_C, s), BF16),
                   jax.ShapeDtypeStruct(kt_stack.shape, F32),
                   jax.ShapeDtypeStruct(vt_stack.shape, F32),
                   jax.ShapeDtypeStruct((b, s, LANES), F32),
                   jax.ShapeDtypeStruct((b, H_C, s), F32)],
        input_output_aliases={len(in_arrays) - 2: 4, len(in_arrays) - 1: 5},
        compiler_params=_cparams(1, VMEM_LIMIT),
        name="ffn_inproj_c_prompt",
    )(*in_arrays)


def _outproj_prompt_kernel(*refs, n_in):
    x_ref, mod_ref, gpost_ref = refs[:3]
    a_refs = refs[3:3 + n_in]
    wt_refs = refs[3 + n_in:3 + 2 * n_in]
    o_ref = refs[-1]
    x = x_ref[...]
    out_t = None
    for a_ref, wt_ref in zip(a_refs, wt_refs):
        t = _dot(wt_ref[...], a_ref[0])
        out_t = t if out_t is None else out_t + t
    out = out_t.T.reshape(x.shape)
    o_ref[...] = _gated_residual(x, mod_ref[...], gpost_ref[...], out, 1.0)


def _outproj_prompt(x, mod, g_post, acts_t, weights_t):
    nb, sb, grid, spec, spec_t, mod_spec = _tok_layout(x, WIDE_TOK_TILE)
    n_in = len(acts_t)
    return pl.pallas_call(
        functools.partial(_outproj_prompt_kernel, n_in=n_in),
        grid=(grid,),
        in_specs=([spec(D_MODEL), mod_spec, _full((1, D_MODEL))]
                  + [spec_t(a.shape[1]) for a in acts_t]
                  + [_full(w.shape) for w in weights_t]),
        out_specs=spec(D_MODEL),
        out_shape=jax.ShapeDtypeStruct(x.shape, F32),
        compiler_params=_cparams(1, VMEM_LIMIT),
        name="outproj_prompt",
    )(x, mod, g_post, *acts_t, *weights_t)


def _diff_lambda(lam_ref, lam_init):
    lv = lam_ref[...]
    a = jnp.sum(lv[0:1] * lv[1:2], axis=1, keepdims=True)
    b = jnp.sum(lv[2:3] * lv[3:4], axis=1, keepdims=True)
    return jnp.exp(a) - jnp.exp(b) + lam_init


def _alibi_slope(g):
    slope = jnp.float32(0.0)
    for h in range(H_B):
        slope = jnp.where(g == h, jnp.float32(2.0 ** (-8.0 * (h + 1) / H_B)), slope)
    return slope


def _select_lane(a, idx):
    lane = lax.broadcasted_iota(jnp.int32, a.shape, 1)
    return jnp.sum(jnp.where(lane == idx, a, 0.0), axis=1, keepdims=True)


def _select_row(a, idx):
    row = lax.broadcasted_iota(jnp.int32, a.shape, 0)
    return jnp.sum(jnp.where(row == idx, a, 0.0), axis=0, keepdims=True)


def _relpos_bias_kernel(tab_ref, pbias_ref, sbias_ref):
    t = tab_ref[0]
    t_hi = t.astype(BF16)
    r1 = t - t_hi.astype(F32)
    t_mid = r1.astype(BF16)
    t_lo = (r1 - t_mid.astype(F32)).astype(BF16)
    ent = lax.broadcasted_iota(jnp.int32, (TAB_PAD, TOEP), 0)
    n = lax.broadcasted_iota(jnp.int32, (TAB_PAD, TOEP), 1)

    def lookup(rel_pos):
        onehot = jnp.where(ent == jnp.clip(rel_pos, -REL_CLIP, REL_CLIP) + REL_CLIP, 1.0, 0.0)
        onehot = onehot.astype(BF16)
        return _dot(t_hi, onehot) + _dot(t_mid, onehot) + _dot(t_lo, onehot)

    gen_t = lookup(A_PAST + jnp.where(n < TP, n, n - TOEP)) * LOG2E
    d = (lax.broadcasted_iota(jnp.int32, (BAND_A, TP), 0) // CHUNK
         - lax.broadcasted_iota(jnp.int32, (BAND_A, TP), 1) // CHUNK)
    valid = (d >= 0) & (d <= A_LEFT_CHUNKS)
    gen = lookup(A_PAST + TP - 1 - n)
    t_sample = sbias_ref.shape[2]
    for h in range(H_A):
        rows_t = jnp.broadcast_to(gen_t[h:h + 1, :], (BAND_A, TOEP))
        toep_t = pltpu.roll(rows_t, 0, 1, stride=1, stride_axis=0)
        pbias_ref[0, h] = jnp.where(valid, toep_t[:, :TP], NEG_INF)
        rows = jnp.broadcast_to(gen[h:h + 1, :], (t_sample, TOEP))
        toep = pltpu.roll(rows, TOEP - TP + 1, 1, stride=1, stride_axis=0)
        sbias_ref[0, h] = toep[:, :SBIAS_W]


def _relpos_bias(relpos_a, sample_len):
    ne = relpos_a.shape[0]
    tab = jnp.pad(relpos_a, ((0, 0), (0, 0), (0, TAB_PAD - relpos_a.shape[-1])))
    return pl.pallas_call(
        _relpos_bias_kernel,
        grid=(ne,),
        in_specs=[pl.BlockSpec((1, H_A, TAB_PAD), lambda l: (l, 0, 0))],
        out_specs=[pl.BlockSpec((1, H_A, BAND_A, TP), lambda l: (l, 0, 0, 0)),
                   pl.BlockSpec((1, H_A, sample_len, SBIAS_W), lambda l: (l, 0, 0, 0))],
        out_shape=[jax.ShapeDtypeStruct((ne, H_A, BAND_A, TP), F32),
                   jax.ShapeDtypeStruct((ne, H_A, sample_len, SBIAS_W), F32)],
        compiler_params=_cparams(1, VMEM_LIMIT),
        name="relpos_bias",
    )(tab)


def _stats_init(m_ref, acc_ref):
    m_ref[...] = jnp.full(m_ref.shape, NEG_INF, F32)
    acc_ref[...] = jnp.zeros(acc_ref.shape, F32)


def _with_ones(v_t):
    return jnp.concatenate([v_t, jnp.ones((SUM_ROWS, v_t.shape[1]), v_t.dtype)], axis=0)


def _stats_update_all(m_ref, acc_ref, scores, values):
    n = len(scores)
    m_prev = [m_ref[i] for i in range(n)]
    m_new = [jnp.maximum(m_prev[i], jnp.max(scores[i], axis=0, keepdims=True)) for i in range(n)]
    alpha = [jnp.exp2(m_prev[i] - m_new[i]) for i in range(n)]
    p = [jnp.exp2(scores[i] - m_new[i]).astype(BF16) for i in range(n)]
    pv = [_dot(values[i], p[i]) for i in range(n)]
    for i in range(n):
        acc_ref[i] = alpha[i] * acc_ref[i] + pv[i]
        m_ref[i] = m_new[i]


def _stats_result(acc_ref, mi):
    acc = acc_ref[mi]
    d = acc.shape[0] - SUM_ROWS
    return acc[:d] * (1.0 / acc[d:d + 1])


def _stats_scratch(n, d, tq):
    return [pltpu.VMEM((n, 1, tq), F32), pltpu.VMEM((n, d + SUM_ROWS, tq), F32)]


def _map_rows(q_t, mi):
    row = lax.broadcasted_iota(jnp.int32, q_t.shape, 0)
    keep = (row >= mi * HEAD_DIM) & (row < (mi + 1) * HEAD_DIM)
    return jnp.where(keep, q_t, jnp.zeros_like(q_t))


def _attn_a_prompt_kernel(qt_ref, k_ref, vt_ref, bias_ref, o_ref, m_ref, acc_ref, *, hp):
    qi = pl.program_id(2)
    qm = [_map_rows(qt_ref[0, p * LANES:(p + 1) * LANES, :], mi) for p in range(hp) for mi in range(2)]
    _stats_init(m_ref, acc_ref)
    for j in range(BAND_TILES):
        kt = qi - (BAND_TILES - 1) + j

        def tile(j=j, kt=kt):
            start = pl.multiple_of(kt * TP, TP)
            scores, values = [], []
            for p in range(hp):
                cols = slice(p * LANES, (p + 1) * LANES)
                k = k_ref[0, pl.ds(start, TP), cols]
                v_t = vt_ref[0, cols, pl.ds(start, TP)]
                for mi in range(2):
                    scores.append(_dot(k, qm[2 * p + mi]) + bias_ref[2 * p + mi, j * TP:(j + 1) * TP, :])
                    values.append(_with_ones(v_t[mi * HEAD_DIM:(mi + 1) * HEAD_DIM]))
            _stats_update_all(m_ref, acc_ref, scores, values)

        if j == BAND_TILES - 1:
            tile()
        else:
            pl.when(kt >= 0)(tile)
    o = jnp.concatenate([_stats_result(acc_ref, i) for i in range(2 * hp)], axis=0)
    o_ref[0] = o.astype(o_ref.dtype)


def _attn_a_prompt(q_t, k, v_t, pbias):
    b, s, _ = k.shape
    hp = HP_A
    groups = D_A // (hp * LANES)
    return pl.pallas_call(
        functools.partial(_attn_a_prompt_kernel, hp=hp),
        grid=(b, groups, s // TP),
        in_specs=[pl.BlockSpec((1, hp * LANES, TP), lambda b, g, i: (b, g, i)),
                  pl.BlockSpec((1, s, hp * LANES), lambda b, g, i: (b, 0, g)),
                  pl.BlockSpec((1, hp * LANES, s), lambda b, g, i: (b, g, 0)),
                  pl.BlockSpec((2 * hp, BAND_A, TP), lambda b, g, i: (g, 0, 0))],
        out_specs=pl.BlockSpec((1, hp * LANES, TP), lambda b, g, i: (b, g, i)),
        out_shape=jax.ShapeDtypeStruct((b, D_A, s), BF16),
        scratch_shapes=_stats_scratch(2 * hp, HEAD_DIM, TP),
        compiler_params=_cparams(3, VMEM_LIMIT),
        name="attn_a_prompt",
    )(q_t, k, v_t, pbias)


def _attn_b_prompt_kernel(qt_ref, k_ref, vt_ref, ke_ref, lam_ref, subln_ref, o_ref,
                          m_ref, acc_ref, diag_ref, *, lam_init, hp):
    g = pl.program_id(1)
    qi = pl.program_id(2)
    coefs = [_alibi_slope(g * hp + p) * LOG2E for p in range(hp)]

    @pl.when(qi == 0)
    def _():
        for d in range(DIAG_TILES):
            kr = lax.broadcasted_iota(jnp.int32, (TP, TQW), 0) + d * TP
            qc = lax.broadcasted_iota(jnp.int32, (TP, TQW), 1)
            dist = jnp.abs(qc - kr).astype(F32)
            for p in range(hp):
                diag_ref[p, d] = jnp.where(kr // CHUNK <= qc // CHUNK, -coefs[p] * dist, NEG_INF)

    row = lax.broadcasted_iota(jnp.int32, (LANES, TQW), 0)
    qpos = (qi * TQW + lax.broadcasted_iota(jnp.int32, (LANES, TQW), 1)).astype(F32)
    part = row % 3
    qa_past, qa_diag = [], []
    for p in range(hp):
        c = coefs[p]
        q_t = qt_ref[0, p * LANES:(p + 1) * LANES, :]
        hi, mid, lo = _split3(jnp.where(row < 3, CHUNK * c, jnp.where(row < 6, c, -c * qpos)))
        qe = jnp.where(row < 9, jnp.where(part == 0, hi, jnp.where(part == 1, mid, lo)), 0.0).astype(BF16)
        for mi in range(2):
            qm = _map_rows(q_t, mi)
            qa_past.append(jnp.concatenate([qm, qe], axis=0))
            qa_diag.append(jnp.concatenate([qm, jnp.zeros_like(qe)], axis=0))
    _stats_init(m_ref, acc_ref)

    def tile(j, qa, diag):
        start = pl.multiple_of(j * TP, TP)
        ke = ke_ref[pl.ds(start, TP), :]
        scores, values = [], []
        for p in range(hp):
            cols = slice(p * LANES, (p + 1) * LANES)
            kx = jnp.concatenate([k_ref[0, pl.ds(start, TP), cols], ke], axis=1)
            v_t = _with_ones(vt_ref[0, cols, pl.ds(start, TP)])
            for mi in range(2):
                s_t = _dot(kx, qa[2 * p + mi])
                scores.append(s_t if diag is None else s_t + diag_ref[p, diag])
                values.append(v_t)
        _stats_update_all(m_ref, acc_ref, scores, values)

    def past_tile(j, carry):
        tile(j, qa_past, None)
        return carry

    lax.fori_loop(0, DIAG_TILES * qi, past_tile, 0)
    for d in range(DIAG_TILES):
        tile(DIAG_TILES * qi + d, qa_diag, d)
    lam = _diff_lambda(lam_ref, lam_init)
    outs = []
    for p in range(hp):
        o = _stats_result(acc_ref, 2 * p) - lam * _stats_result(acc_ref, 2 * p + 1)
        y = o * lax.rsqrt(jnp.mean(o * o, axis=0, keepdims=True) + EPS)
        outs.append((y * subln_ref[...]) * (1.0 - lam_init))
    o_ref[0] = jnp.concatenate(outs, axis=0).astype(o_ref.dtype)


def _attn_b_prompt(q_t, k, v_t, lam_vec, subln_col, lam_init):
    b, s, _ = k.shape
    hp = HP_B
    groups = D_B // (hp * LANES)
    blk0 = D_A // (hp * LANES)
    pos = jnp.arange(s, dtype=jnp.int32)[:, None]
    lane = jnp.arange(LANES, dtype=jnp.int32)[None, :]
    key_extras = jnp.where(lane < 3, pos // CHUNK,
                           jnp.where(lane < 6, pos % CHUNK, jnp.where(lane < 9, 1, 0))).astype(BF16)
    return pl.pallas_call(
        functools.partial(_attn_b_prompt_kernel, lam_init=lam_init, hp=hp),
        grid=(b, groups, s // TQW),
        in_specs=[pl.BlockSpec((1, hp * LANES, TQW), lambda b, g, i: (b, blk0 + g, i)),
                  pl.BlockSpec((1, s, hp * LANES), lambda b, g, i: (b, 0, blk0 + g)),
                  pl.BlockSpec((1, hp * LANES, s), lambda b, g, i: (b, blk0 + g, 0)),
                  _full(key_extras.shape), _full(lam_vec.shape), _full(subln_col.shape)],
        out_specs=pl.BlockSpec((1, hp * LANES, TQW), lambda b, g, i: (b, g, i)),
        out_shape=jax.ShapeDtypeStruct((b, D_B, s), BF16),
        scratch_shapes=(_stats_scratch(2 * hp, 2 * HEAD_DIM, TQW)
                        + [pltpu.VMEM((hp, DIAG_TILES, TP, TQW), F32)]),
        compiler_params=_cparams(3, VMEM_LIMIT),
        name="attn_b_prompt",
    )(q_t, k, v_t, key_extras, lam_vec, subln_col)


def _logf_scan_kernel(logf_ref, dqt_ref, e_ref):
    r = lax.broadcasted_iota(jnp.int32, (SCAN_T, SCAN_T), 0)
    c = lax.broadcasted_iota(jnp.int32, (SCAN_T, SCAN_T), 1)
    tri = jnp.where(r >= c, 1.0, 0.0).astype(BF16)
    pairs = H_C // 2
    i_sel = lax.broadcasted_iota(jnp.int32, (3 * LANES, pairs * LANES), 0)
    col = lax.broadcasted_iota(jnp.int32, (3 * LANES, pairs * LANES), 1)
    j_sel = col % LANES
    src = ((j_sel - 3) % 3) * LANES + 2 * (col // LANES) + (j_sel - 3) // 3
    sel = jnp.where((j_sel >= 3) & (j_sel < 9) & (i_sel == src), 1.0, 0.0).astype(BF16)
    ones_cols = jnp.where(lax.broadcasted_iota(jnp.int32, (1, pairs * LANES), 1) % LANES < 3, 1.0, 0.0)
    carry = jnp.zeros((1, LANES), F32)
    for t in range(logf_ref.shape[1] // SCAN_T):
        rows = slice(t * SCAN_T, (t + 1) * SCAN_T)
        hi, mid, lo = _split3(logf_ref[0, rows, :])
        cs = (_dot(tri, hi.astype(BF16)) + _dot(tri, mid.astype(BF16)) + _dot(tri, lo.astype(BF16))) + carry
        carry = cs[SCAN_T - 1:SCAN_T, :]
        x = cs * LOG2E
        dqt_ref[0, :, rows] = x.T[:H_C, :]
        hi, mid, lo = _split3(x)
        parts = jnp.concatenate([hi.astype(BF16), mid.astype(BF16), lo.astype(BF16)], axis=1)
        e_all = (_dot(parts, sel) + ones_cols).astype(BF16)
        for g in range(pairs):
            e_ref[0, g, rows, :] = e_all[:, g * LANES:(g + 1) * LANES]


def _logf_scan(logf):
    b, s, _ = logf.shape
    return pl.pallas_call(
        _logf_scan_kernel,
        grid=(b,),
        in_specs=[pl.BlockSpec((1, s, LANES), lambda b: (b, 0, 0))],
        out_specs=[pl.BlockSpec((1, H_C, s), lambda b: (b, 0, 0)),
                   pl.BlockSpec((1, H_C // 2, s, LANES), lambda b: (b, 0, 0, 0))],
        out_shape=[jax.ShapeDtypeStruct((b, H_C, s), F32),
                   jax.ShapeDtypeStruct((b, H_C // 2, s, LANES), BF16)],
        compiler_params=_cparams(1, VMEM_LIMIT),
        name="logf_scan",
    )(logf)


def _attn_c_prompt_kernel(qt_ref, k_ref, vt_ref, dqt_ref, e_ref, o_ref, m_ref, acc_ref, *, hp):
    qi = pl.program_id(2)
    row = lax.broadcasted_iota(jnp.int32, (LANES, TQW), 0)
    qa = []
    for p in range(hp):
        q_t = qt_ref[0, p * LANES:(p + 1) * LANES, :]
        dq = dqt_ref[0, p]
        for mi in range(2):
            hi, mid, lo = _split3(dq[mi:mi + 1, :])
            dk_rows = (row >= 3 + 3 * mi) & (row < 6 + 3 * mi)
            qe = jnp.where(row == 0, hi, jnp.where(row == 1, mid, jnp.where(row == 2, lo,
                           jnp.where(dk_rows, -1.0, 0.0))))
            qa.append(jnp.concatenate([_map_rows(q_t, mi), qe.astype(BF16)], axis=0))
    _stats_init(m_ref, acc_ref)

    def tile(j, diag):
        start = pl.multiple_of(j * TP, TP)
        scores, values = [], []
        for p in range(hp):
            cols = slice(p * LANES, (p + 1) * LANES)
            kx = jnp.concatenate([k_ref[0, pl.ds(start, TP), cols], e_ref[0, p, pl.ds(start, TP), :]],
                                 axis=1)
            v_t = vt_ref[0, cols, pl.ds(start, TP)]
            for mi in range(2):
                s_t = _dot(kx, qa[2 * p + mi])
                if diag is not None:
                    kr = lax.broadcasted_iota(jnp.int32, (TP, TQW), 0) + diag * TP
                    qc = lax.broadcasted_iota(jnp.int32, (TP, TQW), 1)
                    s_t = jnp.where(kr <= qc, s_t, NEG_INF)
                scores.append(s_t)
                values.append(_with_ones(v_t[mi * HEAD_DIM:(mi + 1) * HEAD_DIM]))
        _stats_update_all(m_ref, acc_ref, scores, values)

    def past_tile(j, carry):
        tile(j, None)
        return carry

    lax.fori_loop(0, DIAG_TILES * qi, past_tile, 0)
    for d in range(DIAG_TILES):
        tile(DIAG_TILES * qi + d, d)
    o = jnp.concatenate([_stats_result(acc_ref, i) for i in range(2 * hp)], axis=0)
    o_ref[0] = o.astype(o_ref.dtype)


def _attn_c_prompt(q_t, k, v_t, dq_t, key_extras):
    b, s, _ = k.shape
    hp = HP_C
    groups = D_C // (hp * LANES)
    dq_t = dq_t.reshape(b, D_C // LANES, 2, s)
    return pl.pallas_call(
        functools.partial(_attn_c_prompt_kernel, hp=hp),
        grid=(b, groups, s // TQW),
        in_specs=[pl.BlockSpec((1, hp * LANES, TQW), lambda b, g, i: (b, g, i)),
                  pl.BlockSpec((1, s, hp * LANES), lambda b, g, i: (b, 0, g)),
                  pl.BlockSpec((1, hp * LANES, s), lambda b, g, i: (b, g, 0)),
                  pl.BlockSpec((1, hp, 2, TQW), lambda b, g, i: (b, g, 0, i)),
                  pl.BlockSpec((1, hp, s, LANES), lambda b, g, i: (b, g, 0, 0))],
        out_specs=pl.BlockSpec((1, hp * LANES, TQW), lambda b, g, i: (b, g, i)),
        out_shape=jax.ShapeDtypeStruct((b, D_C, s), BF16),
        scratch_shapes=_stats_scratch(2 * hp, HEAD_DIM, TQW),
        compiler_params=_cparams(3, VMEM_LIMIT),
        name="attn_c_prompt",
    )(q_t, k, v_t, dq_t, key_extras)


def _split_maps(q):
    lane = lax.broadcasted_iota(jnp.int32, q.shape, q.ndim - 1)
    zero = jnp.zeros_like(q)
    return jnp.where(lane < HEAD_DIM, q, zero), jnp.where(lane >= HEAD_DIM, q, zero)


def _pair_heads(o0, o1):
    lane = lax.broadcasted_iota(jnp.int32, o0.shape, 1)
    return jnp.where(lane < HEAD_DIM, o0, o1)


def _diff_out(o0, o1, lam, subln, lam_init):
    o = o0 - lam * o1
    y = o * lax.rsqrt(jnp.mean(o * o, axis=-1, keepdims=True) + EPS)
    return (y * subln) * (1.0 - lam_init)


def _sample_scores(q, kn, kc, cache_t):
    q0, q1 = _split_maps(q)
    qs = jnp.concatenate([q0, q1], axis=0)
    s_c = _dot(qs, kc) if cache_t else _dot_nt(qs, kc)
    return s_c, _dot_nt(qs, kn)


def _sample_softmax_pv(s_c, s_n, vn, vc, cache_t):
    n = len(s_c)
    m = [jnp.maximum(jnp.max(s_c[i], axis=-1, keepdims=True), jnp.max(s_n[i], axis=-1, keepdims=True))
         for i in range(n)]
    p_c = [jnp.exp(s_c[i] - m[i]) for i in range(n)]
    p_n = [jnp.exp(s_n[i] - m[i]) for i in range(n)]
    l = [jnp.sum(p_c[i], axis=-1, keepdims=True) + jnp.sum(p_n[i], axis=-1, keepdims=True)
         for i in range(n)]
    outs = []
    for i in range(n):
        pc = p_c[i].astype(BF16)
        o_c = _dot_nt(pc, vc[i]) if cache_t else _dot(pc, vc[i])
        o = (o_c + _dot(p_n[i].astype(BF16), vn[i])) * (1.0 / l[i])
        t = o.shape[0] // 2
        outs.append((o[:t], o[t:]))
    return outs


def _cache_pair_t(ref, p):
    blk = ref[0, 0, 2 * p:2 * p + 2]
    return blk.reshape(2 * HEAD_DIM, blk.shape[-1]).astype(BF16)


def _attn_a_sample_kernel(q_ref, kn_ref, vn_ref, kc_ref, vc_ref, bias_ref, o_ref, *, hp):
    t = q_ref.shape[1]
    w = kc_ref.shape[-1]
    s_c, s_n, vn, vc = [], [], [], []
    for p in range(hp):
        cols = slice(p * LANES, (p + 1) * LANES)
        sc, sn = _sample_scores(q_ref[0, :, cols], kn_ref[0, :, cols], _cache_pair_t(kc_ref, p), True)
        bias = bias_ref[2 * p:2 * p + 2].reshape(2 * t, SBIAS_W)
        s_c.append(sc + bias[:, :w])
        s_n.append(sn + bias[:, w:w + t])
        vn.append(vn_ref[0, :, cols])
        vc.append(_cache_pair_t(vc_ref, p))
    outs = _sample_softmax_pv(s_c, s_n, vn, vc, True)
    for p in range(hp):
        o_ref[0, :, p * LANES:(p + 1) * LANES] = _pair_heads(*outs[p]).astype(o_ref.dtype)


def _attn_c_sample_kernel(q_ref, kn_ref, vn_ref, kc_ref, vc_ref, logf_ref, clogf_ref, o_ref,
                          dpast_ref, *, hp):
    g = pl.program_id(1)
    t = q_ref.shape[1]
    past = clogf_ref.shape[-1]

    @pl.when(g == 0)
    def _():
        r = lax.broadcasted_iota(jnp.int32, (SCAN_T, SCAN_T), 0)
        c = lax.broadcasted_iota(jnp.int32, (SCAN_T, SCAN_T), 1)
        triu = jnp.where(r <= c, 1.0, 0.0).astype(F32)
        carry = jnp.zeros((H_C, 1), F32)
        for i in range(past // SCAN_T):
            cols = slice(i * SCAN_T, (i + 1) * SCAN_T)
            cs = _dot_exact(clogf_ref[0, 0, :, cols], triu) + carry
            dpast_ref[:, cols] = cs
            carry = cs[:, SCAN_T - 1:SCAN_T]
        dpast_ref[...] = dpast_ref[...] - carry

    r = lax.broadcasted_iota(jnp.int32, (t, t), 0)
    c = lax.broadcasted_iota(jnp.int32, (t, t), 1)
    tri = jnp.where(r >= c, 1.0, 0.0).astype(F32)
    dq = _dot_exact(tri, logf_ref[0])
    er = lax.broadcasted_iota(jnp.int32, (LANES, LANES), 0)
    ec = lax.broadcasted_iota(jnp.int32, (LANES, LANES), 1)
    eye = jnp.where(er == ec, 1.0, 0.0).astype(F32)
    dq_t = _dot_nt_exact(eye, dq)
    dpast = dpast_ref[...]
    top = lax.broadcasted_iota(jnp.int32, (2 * t, 1), 0) < t
    qrow = lax.broadcasted_iota(jnp.int32, (2 * t, t), 0) % t
    kcol = lax.broadcasted_iota(jnp.int32, (2 * t, t), 1)
    s_c, s_n, vn, vc = [], [], [], []
    for p in range(hp):
        cols = slice(p * LANES, (p + 1) * LANES)
        h0 = 2 * (g * hp + p)
        dq_col = jnp.concatenate([_select_lane(dq, h0), _select_lane(dq, h0 + 1)], axis=0)
        dk_c = jnp.where(top, _select_row(dpast, h0), _select_row(dpast, h0 + 1))
        dk_n = jnp.where(top, _select_row(dq_t, h0), _select_row(dq_t, h0 + 1))
        sc, sn = _sample_scores(q_ref[0, :, cols], kn_ref[0, :, cols], _cache_pair_t(kc_ref, p), True)
        s_c.append(sc + dq_col - dk_c)
        s_n.append(jnp.where(kcol <= qrow, sn + dq_col - dk_n, NEG_INF))
        vn.append(vn_ref[0, :, cols])
        vc.append(_cache_pair_t(vc_ref, p))
    outs = _sample_softmax_pv(s_c, s_n, vn, vc, True)
    for p in range(hp):
        o_ref[0, :, p * LANES:(p + 1) * LANES] = _pair_heads(*outs[p]).astype(o_ref.dtype)


def _attn_pair_sample(kernel_fn, name, layer, qkv, cache_k_t, cache_v_t, extras, extra_specs,
                      scratch=()):
    b, t, _ = qkv.shape
    heads, _, past = cache_k_t.shape[2:]
    hp = HP_SAMPLE
    groups = heads // (2 * hp)
    width = hp * LANES
    cache_spec = pl.BlockSpec((1, 1, 2 * hp, HEAD_DIM, past), lambda b, g: (layer, b, g, 0, 0))
    return pl.pallas_call(
        functools.partial(kernel_fn, hp=hp),
        grid=(b, groups),
        in_specs=[pl.BlockSpec((1, t, width), lambda b, g: (b, 0, g)),
                  pl.BlockSpec((1, t, width), lambda b, g: (b, 0, groups + g)),
                  pl.BlockSpec((1, t, width), lambda b, g: (b, 0, 2 * groups + g)),
                  cache_spec, cache_spec] + list(extra_specs),
        out_specs=pl.BlockSpec((1, t, width), lambda b, g: (b, 0, g)),
        out_shape=jax.ShapeDtypeStruct((b, t, heads * HEAD_DIM), BF16),
        scratch_shapes=list(scratch),
        compiler_params=_cparams(2, VMEM_LIMIT),
        name=name,
    )(qkv, qkv, qkv, cache_k_t, cache_v_t, *extras)


def _attn_b_sample_kernel(q_ref, kn_ref, vn_ref, kc_ref, vc_ref, lam_ref, subln_ref, o_ref,
                          *, lam_init):
    t = q_ref.shape[1]
    past = kc_ref.shape[2] // H_B
    shape_c = (2 * t, past)
    shape_n = (2 * t, t)
    qpos_c = past + lax.broadcasted_iota(jnp.int32, shape_c, 0) % t
    dist_c = jnp.abs(qpos_c - lax.broadcasted_iota(jnp.int32, shape_c, 1)).astype(F32)
    dist_n = jnp.abs(lax.broadcasted_iota(jnp.int32, shape_n, 0) % t
                     - lax.broadcasted_iota(jnp.int32, shape_n, 1)).astype(F32)
    lam = _diff_lambda(lam_ref, lam_init)
    s_c, s_n, vn, vc = [], [], [], []
    for h in range(H_B):
        cols = slice(h * LANES, (h + 1) * LANES)
        slope = 2.0 ** (-8.0 * (h + 1) / H_B)
        kc = kc_ref[0, 0, pl.ds(h, past, stride=H_B), :].astype(BF16)
        sc, sn = _sample_scores(q_ref[0, :, cols], kn_ref[0, :, cols], kc, False)
        s_c.append(sc - slope * dist_c)
        s_n.append(sn - slope * dist_n)
        vn.append(vn_ref[0, :, cols])
        vc.append(vc_ref[0, 0, pl.ds(h, past, stride=H_B), :].astype(BF16))
    outs = _sample_softmax_pv(s_c, s_n, vn, vc, False)
    for h in range(H_B):
        o_ref[0, :, h * LANES:(h + 1) * LANES] = _diff_out(
            outs[h][0], outs[h][1], lam, subln_ref[...], lam_init).astype(o_ref.dtype)


def _attn_b_sample(layer, qkv, cache_k, cache_v, lam_vec, subln, lam_init):
    b, t, _ = qkv.shape
    rows = cache_k.shape[2]
    col0 = 3 * D_A // D_B
    cache_spec = pl.BlockSpec((1, 1, rows, LANES), lambda b: (layer, b, 0, 0))
    return pl.pallas_call(
        functools.partial(_attn_b_sample_kernel, lam_init=lam_init),
        grid=(b,),
        in_specs=[pl.BlockSpec((1, t, D_B), lambda b: (b, 0, col0)),
                  pl.BlockSpec((1, t, D_B), lambda b: (b, 0, col0 + 1)),
                  pl.BlockSpec((1, t, D_B), lambda b: (b, 0, col0 + 2)),
                  cache_spec, cache_spec, _full(lam_vec.shape), _full(subln.shape)],
        out_specs=pl.BlockSpec((1, t, D_B), lambda b: (b, 0, 0)),
        out_shape=jax.ShapeDtypeStruct((b, t, D_B), BF16),
        compiler_params=_cparams(1, VMEM_LIMIT),
        name="attn_b_sample",
    )(qkv, qkv, qkv, cache_k, cache_v, lam_vec, subln)


def _heads_last(a_t, heads):
    lead = a_t.shape[:-2]
    s = a_t.shape[-1]
    a = a_t.reshape(lead + (heads, HEAD_DIM, s))
    n = len(lead)
    return jnp.transpose(a, tuple(range(n)) + (n + 2, n, n + 1))


def kernel(x_prompt, x_sample, c_prompt, c_sample, cache_a_k, cache_a_v, cache_b_k, cache_b_v, cache_c_k, cache_c_v, cache_c_logf, w_ada, b_ada, norm_pre, norm_post, ffn_w_gate, ffn_w_up, ffn_w_down, w_in_ab, w_out_ab, relpos_a, lambda_b, subln_b, w_in_c, b_f, w_out_c):
    n_prompt, s_prompt = x_prompt.shape[:2]
    n_sample, t_sample = x_sample.shape[:2]

    wg = ffn_w_gate.astype(BF16)
    wu = ffn_w_up.astype(BF16)
    wd = ffn_w_down.astype(BF16)
    w_in_ab16 = w_in_ab.astype(BF16)
    w_in_ab16_t = jnp.swapaxes(w_in_ab, 1, 2).astype(BF16)
    w_out_ab16 = w_out_ab.astype(BF16)
    w_out_ab16_t = jnp.swapaxes(w_out_ab, 1, 2).astype(BF16)
    w_qkv_c16 = w_in_c[:, :, :3 * D_C].astype(BF16)
    w_qkv_c16_t = jnp.swapaxes(w_in_c[:, :, :3 * D_C], 1, 2).astype(BF16)
    w_f16 = jnp.pad(w_in_c[:, :, 3 * D_C:], ((0, 0), (0, 0), (0, LANES - H_C))).astype(BF16)
    w_f16_t = jnp.swapaxes(w_f16, 1, 2)
    b_f_row = jnp.pad(b_f, ((0, 0), (0, LANES - H_C)))[:, None, :]
    b_f_col = jnp.swapaxes(b_f_row, 1, 2)
    w_out_c16 = w_out_c.astype(BF16)
    w_out_c16_t = jnp.swapaxes(w_out_c, 1, 2).astype(BF16)
    cache_a_k_t = jnp.transpose(cache_a_k, (0, 1, 3, 4, 2))
    cache_a_v_t = jnp.transpose(cache_a_v, (0, 1, 3, 4, 2))
    cache_c_k_t = jnp.transpose(cache_c_k, (0, 1, 3, 4, 2))
    cache_c_v_t = jnp.transpose(cache_c_v, (0, 1, 3, 4, 2))
    clogf_t = jnp.swapaxes(cache_c_logf, -1, -2)
    past_b = cache_b_k.shape[2]
    cache_b_k_rows = cache_b_k.reshape(cache_b_k.shape[:2] + (past_b * H_B, 2 * HEAD_DIM))
    cache_b_v_rows = cache_b_v.reshape(cache_b_v.shape[:2] + (past_b * H_B, 2 * HEAD_DIM))

    rows = n_prompt + n_sample
    rows_pad = -(-rows // 8) * 8
    c_all = jnp.concatenate([c_prompt, c_sample, jnp.zeros((rows_pad - rows, D_MODEL), F32)], axis=0)
    mod_all = _ada(c_all, w_ada, b_ada).reshape(DEPTH, rows_pad, N_SUB, 3, D_MODEL)
    pbias, sbias = _relpos_bias(relpos_a, t_sample)

    def sublayer_params(li, row0, nrows):
        mod = [mod_all[li, row0:row0 + nrows, i] for i in range(N_SUB)]
        gpre = [norm_pre[li, i][None, :] for i in range(N_SUB)]
        gpost = [norm_post[li, i][None, :] for i in range(N_SUB)]
        return mod, gpre, gpost

    def run_prompt(x):
        a_k, a_v, b_k, b_v, c_lf = [], [], [], [], []
        c_k = jnp.zeros((DEPTH // 2, n_prompt, D_C, s_prompt), F32)
        c_v = jnp.zeros((DEPTH // 2, n_prompt, D_C, s_prompt), F32)
        for li in range(DEPTH):
            mod, gpre, gpost = sublayer_params(li, 0, n_prompt)
            j = li // 2
            if li % 2 == 0:
                lam_init = _lam_init(li)
                x, q_t, k, v_t, ka_t, va_t, kb, vb = _ffn_inproj_ab_prompt(
                    x, mod[0], mod[1], gpre[0], gpost[0], wg, wu, wd, li, gpre[1],
                    w_in_ab16[j], w_in_ab16_t[j])
                out_a = _attn_a_prompt(q_t, k, v_t, pbias[j])
                out_b = _attn_b_prompt(q_t, k, v_t, lambda_b[j], subln_b[j][:, None], lam_init)
                w = min(A_PAST, s_prompt)
                a_k.append(ka_t[:, :, s_prompt - w:])
                a_v.append(va_t[:, :, s_prompt - w:])
                b_k.append(kb)
                b_v.append(vb)
                outs, w_outs = [out_a, out_b], [w_out_ab16_t[j, :, :D_A], w_out_ab16_t[j, :, D_A:]]
            else:
                x, q_t, k, v_t, c_k, c_v, logf, logf_t = _ffn_inproj_c_prompt(
                    x, mod[0], mod[1], gpre[0], gpost[0], wg, wu, wd, li, gpre[1],
                    w_qkv_c16[j, :, D_C:2 * D_C], w_qkv_c16_t[j],
                    w_f16[j], w_f16_t[j], b_f_row[j], b_f_col[j], c_k, c_v, j)
                dq_t, key_extras = _logf_scan(logf)
                outs, w_outs = [_attn_c_prompt(q_t, k, v_t, dq_t, key_extras)], [w_out_c16_t[j]]
                c_lf.append(logf_t)
            x = _outproj_prompt(x, mod[1], gpost[1], outs, w_outs)
            x = _ffn(x, mod[2], gpre[2], gpost[2], wg, wu, wd, li, 1)
        nb, sb = x.shape[:2]
        return (x,
                _heads_last(jnp.stack(a_k), H_A), _heads_last(jnp.stack(a_v), H_A),
                jnp.stack(b_k).reshape(-1, nb, sb, H_B, 2 * HEAD_DIM),
                jnp.stack(b_v).reshape(-1, nb, sb, H_B, 2 * HEAD_DIM),
                _heads_last(c_k, H_C), _heads_last(c_v, H_C),
                jnp.swapaxes(jnp.stack(c_lf), -1, -2))

    def run_sample(x):
        a_k, a_v, b_k, b_v, c_k, c_v, c_lf = [], [], [], [], [], [], []
        for li in range(DEPTH):
            mod, gpre, gpost = sublayer_params(li, n_prompt, n_sample)
            x = _ffn(x, mod[0], gpre[0], gpost[0], wg, wu, wd, li, 0)
            j = li // 2
            if li % 2 == 0:
                lam_init = _lam_init(li)
                qkv, ka, va, kb, vb = _inproj_ab(x, mod[1], gpre[1], w_in_ab16[j])
                subln = subln_b[j][None, :]
                out_a = _attn_pair_sample(
                    _attn_a_sample_kernel, "attn_a_sample", j, qkv, cache_a_k_t, cache_a_v_t,
                    [sbias[j]],
                    [pl.BlockSpec((2 * HP_SAMPLE, t_sample, SBIAS_W), lambda b, g: (g, 0, 0))])
                out_b = _attn_b_sample(j, qkv, cache_b_k_rows, cache_b_v_rows, lambda_b[j], subln,
                                       lam_init)
                a_k.append(ka)
                a_v.append(va)
                b_k.append(kb)
                b_v.append(vb)
                x = _outproj(x, mod[1], gpost[1], [out_a, out_b],
                             [w_out_ab16[j, :D_A], w_out_ab16[j, D_A:]])
            else:
                qkv, k, v, logf = _inproj_c(x, mod[1], gpre[1], w_qkv_c16[j], w_f16[j], b_f_row[j])
                past = clogf_t.shape[-1]
                out = _attn_pair_sample(
                    _attn_c_sample_kernel, "attn_c_sample", j, qkv, cache_c_k_t, cache_c_v_t,
                    [logf, clogf_t],
                    [pl.BlockSpec((1, t_sample, LANES), lambda b, g: (b, 0, 0)),
                     pl.BlockSpec((1, 1, H_C, past), lambda b, g, j=j: (j, b, 0, 0))],
                    scratch=[pltpu.VMEM((H_C, past), F32)])
                c_k.append(k)
                c_v.append(v)
                c_lf.append(logf[:, :, :H_C])
                x = _outproj(x, mod[1], gpost[1], [out], [w_out_c16[j]])
            x = _ffn(x, mod[2], gpre[2], gpost[2], wg, wu, wd, li, 1)
        nb, sb = x.shape[:2]
        return (x,
                jnp.stack(a_k).reshape(-1, nb, sb, H_A, HEAD_DIM),
                jnp.stack(a_v).reshape(-1, nb, sb, H_A, HEAD_DIM),
                jnp.stack(b_k).reshape(-1, nb, sb, H_B, 2 * HEAD_DIM),
                jnp.stack(b_v).reshape(-1, nb, sb, H_B, 2 * HEAD_DIM),
                jnp.stack(c_k).reshape(-1, nb, sb, H_C, HEAD_DIM),
                jnp.stack(c_v).reshape(-1, nb, sb, H_C, HEAD_DIM),
                jnp.stack(c_lf))

    p = run_prompt(x_prompt)
    s = run_sample(x_sample)
    return (p[0], s[0]) + tuple(p[1:]) + tuple(s[1:])
```

```python
import functools
import math

import jax
import jax.numpy as jnp
from jax import lax
from jax.experimental import pallas as pl
from jax.experimental.pallas import tpu as pltpu

D_MODEL = 1024
DEPTH = 4
CHUNK = 64
HEAD_DIM = 64
H_A = 8
H_B = 4
H_C = 16
D_A = H_A * HEAD_DIM
D_B = H_B * 2 * HEAD_DIM
D_C = H_C * HEAD_DIM
A_LEFT_CHUNKS = 8
A_PAST = A_LEFT_CHUNKS * CHUNK
REL_CLIP = 128
D_FF = 2816
N_SUB = 3
MACARON_W = 0.5
EPS = 1e-6
NEG_INF = -1e30
ATTN_SCALE = HEAD_DIM ** -0.5
LOG2E = 1.4426950408889634
QSCALE_LOG2 = ATTN_SCALE * LOG2E

LANES = 128
TOK_TILE = 256
WIDE_TOK_TILE = 512
SAMPLE_GROUP = 8
TP = 256
EXTRA_ROWS = 16
SUM_ROWS = 16
DIAG_TILES = 2
TQW = DIAG_TILES * TP
HP_A = 4
HP_B = 4
HP_C = 4
HP_SAMPLE = 4
BAND_TILES = 3
BAND_A = BAND_TILES * TP
SCAN_T = 512
TAB_PAD = 384
TOEP = 1024
SBIAS_W = 640
ADA_TN = 1536
VMEM_LIMIT = 56 * 1024 * 1024

BF16 = jnp.bfloat16
F32 = jnp.float32


def _lam_init(li):
    return 0.8 - 0.6 * math.exp(-0.3 * li)


def _cparams(n_axes, vmem=None):
    return pltpu.CompilerParams(dimension_semantics=("arbitrary",) * n_axes,
                                vmem_limit_bytes=vmem)


def _dot(a, b):
    return jnp.dot(a, b, preferred_element_type=F32)


def _dot_nt(a, b):
    return lax.dot_general(a, b, (((1,), (1,)), ((), ())), preferred_element_type=F32)


def _dot_exact(a, b):
    return jnp.dot(a, b, preferred_element_type=F32, precision=lax.Precision.HIGHEST)


def _dot_nt_exact(a, b):
    return lax.dot_general(a, b, (((1,), (1,)), ((), ())), preferred_element_type=F32,
                           precision=lax.Precision.HIGHEST)


def _split3(x):
    hi = x.astype(BF16).astype(F32)
    r1 = x - hi
    mid = r1.astype(BF16).astype(F32)
    lo = (r1 - mid).astype(BF16).astype(F32)
    return hi, mid, lo


def _norm_mod(x, m, g_pre):
    y = x * lax.rsqrt(jnp.mean(x * x, axis=-1, keepdims=True) + EPS)
    return (y * g_pre) * (1.0 + m[:, 1:2, :]) + m[:, 0:1, :]


def _gated_residual(x, m, g_post, out, res_w):
    y = out * lax.rsqrt(jnp.mean(out * out, axis=-1, keepdims=True) + EPS)
    return x + (res_w * (1.0 + m[:, 2:3, :])) * (y * g_post)


def _tok_layout(x, tile=TOK_TILE):
    nb_total, sb_total, _ = x.shape
    if sb_total >= tile:
        nb, sb = 1, tile
    else:
        nb, sb = SAMPLE_GROUP, sb_total
    tiles = sb_total // sb
    grid = (nb_total // nb) * tiles

    def spec(width, rows_per_token=1):
        return pl.BlockSpec((nb, sb * rows_per_token, width), lambda i: (i // tiles, i % tiles, 0))

    def spec_t(width):
        return pl.BlockSpec((nb, width, sb), lambda i: (i // tiles, 0, i % tiles))

    mod_spec = pl.BlockSpec((nb, 3, D_MODEL), lambda i: (i // tiles, 0, 0))
    return nb, sb, grid, spec, spec_t, mod_spec


def _full(shape):
    return pl.BlockSpec(shape, lambda *_: (0,) * len(shape), pipeline_mode=pl.Buffered(1))


def _ada_kernel(c_ref, w_ref, b_ref, o_ref):
    c = c_ref[...]
    a = (c * jax.nn.sigmoid(c)).astype(BF16)
    o_ref[0] = _dot(a, w_ref[0].astype(BF16)) + b_ref[0]


def _ada(c_all, w_ada, b_ada):
    rows = c_all.shape[0]
    n_out = w_ada.shape[-1]
    return pl.pallas_call(
        _ada_kernel,
        grid=(DEPTH, n_out // ADA_TN),
        in_specs=[pl.BlockSpec((rows, D_MODEL), lambda l, n: (0, 0)),
                  pl.BlockSpec((1, D_MODEL, ADA_TN), lambda l, n: (l, 0, n)),
                  pl.BlockSpec((1, 1, ADA_TN), lambda l, n: (l, 0, n))],
        out_specs=pl.BlockSpec((1, rows, ADA_TN), lambda l, n: (l, 0, n)),
        out_shape=jax.ShapeDtypeStruct((DEPTH, rows, n_out), F32),
        compiler_params=_cparams(2, VMEM_LIMIT),
        name="ada",
    )(c_all, w_ada, b_ada.reshape(DEPTH, 1, n_out))


def _ffn_body(x, m, g_pre, g_post, wg_ref, wu_ref, wd_ref):
    nb, sb, d = x.shape
    h = _norm_mod(x, m, g_pre).reshape(nb * sb, d).astype(BF16)
    g = _dot(h, wg_ref[0, 0])
    u = _dot(h, wu_ref[0, 0])
    a = ((g * jax.nn.sigmoid(g)) * u).astype(BF16)
    out = _dot(a, wd_ref[0, 0]).reshape(nb, sb, d)
    return _gated_residual(x, m, g_post, out, MACARON_W)


def _ffn_weight_specs(li, k):
    def one(rows, cols):
        return pl.BlockSpec((1, 1, rows, cols), lambda *_: (li, k, 0, 0), pipeline_mode=pl.Buffered(1))
    return [one(D_MODEL, D_FF), one(D_MODEL, D_FF), one(D_FF, D_MODEL)]


def _ffn_kernel(x_ref, mod_ref, gpre_ref, gpost_ref, wg_ref, wu_ref, wd_ref, o_ref):
    o_ref[...] = _ffn_body(x_ref[...], mod_ref[...], gpre_ref[...], gpost_ref[...],
                           wg_ref, wu_ref, wd_ref)


def _ffn(x, mod, g_pre, g_post, wg, wu, wd, li, k):
    nb, sb, grid, spec, _, mod_spec = _tok_layout(x, WIDE_TOK_TILE)
    return pl.pallas_call(
        _ffn_kernel,
        grid=(grid,),
        in_specs=[spec(D_MODEL), mod_spec, _full((1, D_MODEL)), _full((1, D_MODEL))]
        + _ffn_weight_specs(li, k),
        out_specs=spec(D_MODEL),
        out_shape=jax.ShapeDtypeStruct(x.shape, F32),
        compiler_params=_cparams(1, VMEM_LIMIT),
        name="ffn",
    )(x, mod, g_pre, g_post, wg, wu, wd)


def _inproj_ab_kernel(x_ref, mod_ref, gpre_ref, w_ref, qkv_ref, ka_ref, va_ref, kb_ref, vb_ref):
    x = x_ref[...]
    nb, sb, d = x.shape
    h = _norm_mod(x, mod_ref[...], gpre_ref[...]).reshape(nb * sb, d).astype(BF16)
    f32_outs = {1: ka_ref, 2: va_ref, 4: kb_ref, 5: vb_ref}
    for c in range(6):
        cols = slice(c * D_A, (c + 1) * D_A)
        p = _dot(h, w_ref[:, cols]).reshape(nb, sb, D_A)
        if c in f32_outs:
            f32_outs[c][...] = p
            qkv_ref[:, :, cols] = p.astype(BF16)
        else:
            qkv_ref[:, :, cols] = (p * ATTN_SCALE).astype(BF16)


def _inproj_ab(x, mod, g_pre, w_in):
    nb, sb, grid, spec, _, mod_spec = _tok_layout(x)
    nbt, sbt, _ = x.shape
    f32_out = jax.ShapeDtypeStruct((nbt, sbt, D_A), F32)
    return pl.pallas_call(
        _inproj_ab_kernel,
        grid=(grid,),
        in_specs=[spec(D_MODEL), mod_spec, _full((1, D_MODEL)), _full((D_MODEL, 6 * D_A))],
        out_specs=[spec(6 * D_A)] + [spec(D_A)] * 4,
        out_shape=[jax.ShapeDtypeStruct((nbt, sbt, 6 * D_A), BF16)] + [f32_out] * 4,
        compiler_params=_cparams(1, VMEM_LIMIT),
        name="inproj_ab",
    )(x, mod, g_pre, w_in)


def _log_sigmoid(z):
    return jnp.minimum(z, 0.0) - jnp.log1p(jnp.exp(-jnp.abs(z)))


def _inproj_c_kernel(x_ref, mod_ref, gpre_ref, w_ref, wf_ref, bf_ref,
                     qkv_ref, k_ref, v_ref, logf_ref):
    x = x_ref[...]
    nb, sb, d = x.shape
    h = _norm_mod(x, mod_ref[...], gpre_ref[...]).reshape(nb * sb, d).astype(BF16)
    f32_outs = {1: k_ref, 2: v_ref}
    for c in range(3):
        cols = slice(c * D_C, (c + 1) * D_C)
        p = _dot(h, w_ref[:, cols]).reshape(nb, sb, D_C)
        if c in f32_outs:
            f32_outs[c][...] = p
            qkv_ref[:, :, cols] = p.astype(BF16)
        else:
            qkv_ref[:, :, cols] = (p * ATTN_SCALE).astype(BF16)
    logf = _log_sigmoid(_dot(h, wf_ref[...]) + bf_ref[...])
    logf_ref[...] = logf.reshape(nb, sb, LANES)


def _inproj_c(x, mod, g_pre, w_qkv, w_f, b_f):
    nb, sb, grid, spec, _, mod_spec = _tok_layout(x)
    nbt, sbt, _ = x.shape
    f32_out = jax.ShapeDtypeStruct((nbt, sbt, D_C), F32)
    return pl.pallas_call(
        _inproj_c_kernel,
        grid=(grid,),
        in_specs=[spec(D_MODEL), mod_spec, _full((1, D_MODEL)), _full((D_MODEL, 3 * D_C)),
                  _full((D_MODEL, LANES)), _full((1, LANES))],
        out_specs=[spec(3 * D_C), spec(D_C), spec(D_C), spec(LANES)],
        out_shape=[jax.ShapeDtypeStruct((nbt, sbt, 3 * D_C), BF16), f32_out, f32_out,
                   jax.ShapeDtypeStruct((nbt, sbt, LANES), F32)],
        compiler_params=_cparams(1, VMEM_LIMIT),
        name="inproj_c",
    )(x, mod, g_pre, w_qkv, w_f, b_f)


def _outproj_kernel(*refs, n_in):
    x_ref, mod_ref, gpost_ref = refs[:3]
    a_refs = refs[3:3 + n_in]
    w_refs = refs[3 + n_in:3 + 2 * n_in]
    o_ref = refs[-1]
    x = x_ref[...]
    nb, sb, d = x.shape
    out = None
    for a_ref, w_ref in zip(a_refs, w_refs):
        a = a_ref[...]
        t = _dot(a.reshape(nb * sb, a.shape[-1]), w_ref[...])
        out = t if out is None else out + t
    o_ref[...] = _gated_residual(x, mod_ref[...], gpost_ref[...], out.reshape(nb, sb, d), 1.0)


def _outproj(x, mod, g_post, acts, weights):
    nb, sb, grid, spec, _, mod_spec = _tok_layout(x)
    n_in = len(acts)
    return pl.pallas_call(
        functools.partial(_outproj_kernel, n_in=n_in),
        grid=(grid,),
        in_specs=([spec(D_MODEL), mod_spec, _full((1, D_MODEL))]
                  + [spec(a.shape[-1]) for a in acts]
                  + [_full(w.shape) for w in weights]),
        out_specs=spec(D_MODEL),
        out_shape=jax.ShapeDtypeStruct(x.shape, F32),
        compiler_params=_cparams(1, VMEM_LIMIT),
        name="outproj",
    )(x, mod, g_post, *acts, *weights)


def _inproj_ab_prompt_kernel(x_ref, mod0_ref, mod_ref, gpre0_ref, gpost0_ref, wg_ref, wu_ref, wd_ref,
                             gpre_ref, w_ref, wt_ref, kb_in_ref, vb_in_ref,
                             xo_ref, qt_ref, k_ref, vt_ref, kat_ref, vat_ref, kb_ref, vb_ref):
    del kb_in_ref, vb_in_ref
    x = _ffn_body(x_ref[...], mod0_ref[...], gpre0_ref[...], gpost0_ref[...], wg_ref, wu_ref, wd_ref)
    xo_ref[...] = x
    _, sb, d = x.shape
    h = _norm_mod(x, mod_ref[...], gpre_ref[...]).reshape(sb, d).astype(BF16)

    def nat(c):
        return _dot(h, w_ref[:, c * D_A:(c + 1) * D_A])

    def tr(c):
        return _dot_nt(wt_ref[c * D_A:(c + 1) * D_A, :], h)

    qt_ref[0, :D_A, :] = (tr(0) * QSCALE_LOG2).astype(BF16)
    qt_ref[0, D_A:, :] = (tr(3) * QSCALE_LOG2).astype(BF16)
    ka_t = tr(1)
    kat_ref[0] = ka_t
    k_ref[0, :, :D_A] = ka_t.T.astype(BF16)
    kb = nat(4)
    k_ref[0, :, D_A:] = kb.astype(BF16)
    va_t = tr(2)
    vat_ref[0] = va_t
    vt_ref[0, :D_A, :] = va_t.astype(BF16)
    vb = nat(5)
    vt_ref[0, D_A:, :] = vb.T.astype(BF16)
    for hd in range(H_B):
        kb_ref[0, 0, pl.ds(hd, sb, stride=H_B), :] = kb[:, hd * LANES:(hd + 1) * LANES]
        vb_ref[0, 0, pl.ds(hd, sb, stride=H_B), :] = vb[:, hd * LANES:(hd + 1) * LANES]


def _ffn_inproj_ab_prompt(x, mod0, mod1, g_pre0, g_post0, wg, wu, wd, li, g_pre1, w_in, w_in_t,
                          kb_stack, vb_stack, j):
    nb, sb, grid, spec, spec_t, mod_spec = _tok_layout(x)
    b, s, _ = x.shape
    tiles = s // sb
    width = D_A + D_B
    vec = _full((1, D_MODEL))
    stack_spec = pl.BlockSpec((1, nb, sb * H_B, LANES), lambda i: (j, i // tiles, i % tiles, 0))
    in_arrays = (x, mod0, mod1, g_pre0, g_post0, wg, wu, wd, g_pre1, w_in, w_in_t, kb_stack, vb_stack)
    return pl.pallas_call(
        _inproj_ab_prompt_kernel,
        grid=(grid,),
        in_specs=[spec(D_MODEL), mod_spec, mod_spec, vec, vec] + _ffn_weight_specs(li, 0)
        + [vec, _full(w_in.shape), _full(w_in_t.shape),
           pl.BlockSpec(memory_space=pl.ANY), pl.BlockSpec(memory_space=pl.ANY)],
        out_specs=[spec(D_MODEL), spec_t(width), spec(width), spec_t(width), spec_t(D_A), spec_t(D_A),
                   stack_spec, stack_spec],
        out_shape=[jax.ShapeDtypeStruct(x.shape, F32),
                   jax.ShapeDtypeStruct((b, width, s), BF16),
                   jax.ShapeDtypeStruct((b, s, width), BF16),
                   jax.ShapeDtypeStruct((b, width, s), BF16),
                   jax.ShapeDtypeStruct((b, D_A, s), F32),
                   jax.ShapeDtypeStruct((b, D_A, s), F32),
                   jax.ShapeDtypeStruct(kb_stack.shape, F32),
                   jax.ShapeDtypeStruct(vb_stack.shape, F32)],
        input_output_aliases={len(in_arrays) - 2: 6, len(in_arrays) - 1: 7},
        compiler_params=_cparams(1, VMEM_LIMIT),
        name="ffn_inproj_ab_prompt",
    )(*in_arrays)


def _inproj_c_prompt_kernel(x_ref, mod0_ref, mod_ref, gpre0_ref, gpost0_ref, wg_ref, wu_ref, wd_ref,
                            gpre_ref, wk_ref, wt_ref, wf_ref, wft_ref, bf_ref, bft_ref,
                            kt32_in_ref, vt32_in_ref,
                            xo_ref, qt_ref, k_ref, vt_ref, kt32_ref, vt32_ref, logf_ref, logft_ref):
    del kt32_in_ref, vt32_in_ref
    x = _ffn_body(x_ref[...], mod0_ref[...], gpre0_ref[...], gpost0_ref[...], wg_ref, wu_ref, wd_ref)
    xo_ref[...] = x
    _, sb, d = x.shape
    h = _norm_mod(x, mod_ref[...], gpre_ref[...]).reshape(sb, d).astype(BF16)
    qt_ref[0] = (_dot_nt(wt_ref[:D_C, :], h) * QSCALE_LOG2).astype(BF16)
    k_t = _dot_nt(wt_ref[D_C:2 * D_C, :], h)
    kt32_ref[0, 0] = k_t
    k_ref[0] = k_t.T.astype(BF16)
    v_t = _dot_nt(wt_ref[2 * D_C:, :], h)
    vt32_ref[0, 0] = v_t
    vt_ref[0] = v_t.astype(BF16)
    logf_ref[0] = _log_sigmoid(_dot(h, wf_ref[...]) + bf_ref[...])
    logft_ref[0] = _log_sigmoid(_dot_nt(wft_ref[...], h) + bft_ref[...])[:H_C, :]


def _ffn_inproj_c_prompt(x, mod0, mod1, g_pre0, g_post0, wg, wu, wd, li, g_pre1,
                         w_k, w_qkv_t, w_f, w_f_t, b_f, b_f_t, kt_stack, vt_stack, j):
    nb, sb, grid, spec, spec_t, mod_spec = _tok_layout(x)
    b, s, _ = x.shape
    tiles = s // sb
    vec = _full((1, D_MODEL))
    stack_spec = pl.BlockSpec((1, nb, D_C, sb), lambda i: (j, i // tiles, 0, i % tiles))
    in_arrays = (x, mod0, mod1, g_pre0, g_post0, wg, wu, wd, g_pre1, w_k, w_qkv_t, w_f, w_f_t, b_f, b_f_t,
                 kt_stack, vt_stack)
    return pl.pallas_call(
        _inproj_c_prompt_kernel,
        grid=(grid,),
        in_specs=[spec(D_MODEL), mod_spec, mod_spec, vec, vec] + _ffn_weight_specs(li, 0)
        + [vec, _full(w_k.shape), _full(w_qkv_t.shape), _full(w_f.shape), _full(w_f_t.shape),
           _full(b_f.shape), _full(b_f_t.shape),
           pl.BlockSpec(memory_space=pl.ANY), pl.BlockSpec(memory_space=pl.ANY)],
        out_specs=[spec(D_MODEL), spec_t(D_C), spec(D_C), spec_t(D_C), stack_spec, stack_spec,
                   spec(LANES), spec_t(H_C)],
        out_shape=[jax.ShapeDtypeStruct(x.shape, F32),
                   jax.ShapeDtypeStruct((b, D_C, s), BF16),
                   jax.ShapeDtypeStruct((b, s, D_C), BF16),
                   jax.ShapeDtypeStruct((b, D_C, s), BF16),
                   jax.ShapeDtypeStruct(kt_stack.shape, F32),
                   jax.ShapeDtypeStruct(vt_stack.shape, F32),
                   jax.ShapeDtypeStruct((b, s, LANES), F32),
                   jax.ShapeDtypeStruct((b, H_C, s), F32)],
        input_output_aliases={len(in_arrays) - 2: 4, len(in_arrays) - 1: 5},
        compiler_params=_cparams(1, VMEM_LIMIT),
        name="ffn_inproj_c_prompt",
    )(*in_arrays)


def _outproj_prompt_kernel(*refs, n_in):
    x_ref, mod_ref, gpost_ref = refs[:3]
    a_refs = refs[3:3 + n_in]
    wt_refs = refs[3 + n_in:3 + 2 * n_in]
    o_ref = refs[-1]
    x = x_ref[...]
    out_t = None
    for a_ref, wt_ref in zip(a_refs, wt_refs):
        t = _dot(wt_ref[...], a_ref[0])
        out_t = t if out_t is None else out_t + t
    out = out_t.T.reshape(x.shape)
    o_ref[...] = _gated_residual(x, mod_ref[...], gpost_ref[...], out, 1.0)


def _outproj_prompt(x, mod, g_post, acts_t, weights_t):
    nb, sb, grid, spec, spec_t, mod_spec = _tok_layout(x, WIDE_TOK_TILE)
    n_in = len(acts_t)
    return pl.pallas_call(
        functools.partial(_outproj_prompt_kernel, n_in=n_in),
        grid=(grid,),
        in_specs=([spec(D_MODEL), mod_spec, _full((1, D_MODEL))]
                  + [spec_t(a.shape[1]) for a in acts_t]
                  + [_full(w.shape) for w in weights_t]),
        out_specs=spec(D_MODEL),
        out_shape=jax.ShapeDtypeStruct(x.shape, F32),
        compiler_params=_cparams(1, VMEM_LIMIT),
        name="outproj_prompt",
    )(x, mod, g_post, *acts_t, *weights_t)


def _diff_lambda(lam_ref, lam_init):
    lv = lam_ref[...]
    a = jnp.sum(lv[0:1] * lv[1:2], axis=1, keepdims=True)
    b = jnp.sum(lv[2:3] * lv[3:4], axis=1, keepdims=True)
    return jnp.exp(a) - jnp.exp(b) + lam_init


def _alibi_slope(g):
    slope = jnp.float32(0.0)
    for h in range(H_B):
        slope = jnp.where(g == h, jnp.float32(2.0 ** (-8.0 * (h + 1) / H_B)), slope)
    return slope


def _select_lane(a, idx):
    lane = lax.broadcasted_iota(jnp.int32, a.shape, 1)
    return jnp.sum(jnp.where(lane == idx, a, 0.0), axis=1, keepdims=True)


def _select_row(a, idx):
    row = lax.broadcasted_iota(jnp.int32, a.shape, 0)
    return jnp.sum(jnp.where(row == idx, a, 0.0), axis=0, keepdims=True)


def _relpos_bias_kernel(tab_ref, pbias_ref, sbias_ref):
    t = tab_ref[0]
    t_hi = t.astype(BF16)
    r1 = t - t_hi.astype(F32)
    t_mid = r1.astype(BF16)
    t_lo = (r1 - t_mid.astype(F32)).astype(BF16)
    ent = lax.broadcasted_iota(jnp.int32, (TAB_PAD, TOEP), 0)
    n = lax.broadcasted_iota(jnp.int32, (TAB_PAD, TOEP), 1)

    def lookup(rel_pos):
        onehot = jnp.where(ent == jnp.clip(rel_pos, -REL_CLIP, REL_CLIP) + REL_CLIP, 1.0, 0.0)
        onehot = onehot.astype(BF16)
        return _dot(t_hi, onehot) + _dot(t_mid, onehot) + _dot(t_lo, onehot)

    gen_t = lookup(A_PAST + jnp.where(n < TP, n, n - TOEP)) * LOG2E
    d = (lax.broadcasted_iota(jnp.int32, (BAND_A, TP), 0) // CHUNK
         - lax.broadcasted_iota(jnp.int32, (BAND_A, TP), 1) // CHUNK)
    valid = (d >= 0) & (d <= A_LEFT_CHUNKS)
    gen = lookup(A_PAST + TP - 1 - n)
    t_sample = sbias_ref.shape[2]
    for h in range(H_A):
        rows_t = jnp.broadcast_to(gen_t[h:h + 1, :], (BAND_A, TOEP))
        toep_t = pltpu.roll(rows_t, 0, 1, stride=1, stride_axis=0)
        pbias_ref[0, h] = jnp.where(valid, toep_t[:, :TP], NEG_INF)
        rows = jnp.broadcast_to(gen[h:h + 1, :], (t_sample, TOEP))
        toep = pltpu.roll(rows, TOEP - TP + 1, 1, stride=1, stride_axis=0)
        sbias_ref[0, h] = toep[:, :SBIAS_W]


def _relpos_bias(relpos_a, sample_len):
    ne = relpos_a.shape[0]
    tab = jnp.pad(relpos_a, ((0, 0), (0, 0), (0, TAB_PAD - relpos_a.shape[-1])))
    return pl.pallas_call(
        _relpos_bias_kernel,
        grid=(ne,),
        in_specs=[pl.BlockSpec((1, H_A, TAB_PAD), lambda l: (l, 0, 0))],
        out_specs=[pl.BlockSpec((1, H_A, BAND_A, TP), lambda l: (l, 0, 0, 0)),
                   pl.BlockSpec((1, H_A, sample_len, SBIAS_W), lambda l: (l, 0, 0, 0))],
        out_shape=[jax.ShapeDtypeStruct((ne, H_A, BAND_A, TP), F32),
                   jax.ShapeDtypeStruct((ne, H_A, sample_len, SBIAS_W), F32)],
        compiler_params=_cparams(1, VMEM_LIMIT),
        name="relpos_bias",
    )(tab)


def _stats_init(m_ref, acc_ref):
    m_ref[...] = jnp.full(m_ref.shape, NEG_INF, F32)
    acc_ref[...] = jnp.zeros(acc_ref.shape, F32)


def _with_ones(v_t):
    return jnp.concatenate([v_t, jnp.ones((SUM_ROWS, v_t.shape[1]), v_t.dtype)], axis=0)


def _stats_update_all(m_ref, acc_ref, scores, values):
    n = len(scores)
    m_prev = [m_ref[i] for i in range(n)]
    m_new = [jnp.maximum(m_prev[i], jnp.max(scores[i], axis=0, keepdims=True)) for i in range(n)]
    alpha = [jnp.exp2(m_prev[i] - m_new[i]) for i in range(n)]
    p = [jnp.exp2(scores[i] - m_new[i]).astype(BF16) for i in range(n)]
    pv = [_dot(values[i], p[i]) for i in range(n)]
    for i in range(n):
        acc_ref[i] = alpha[i] * acc_ref[i] + pv[i]
        m_ref[i] = m_new[i]


def _stats_result(acc_ref, mi):
    acc = acc_ref[mi]
    d = acc.shape[0] - SUM_ROWS
    return acc[:d] * (1.0 / acc[d:d + 1])


def _stats_scratch(n, d, tq):
    return [pltpu.VMEM((n, 1, tq), F32), pltpu.VMEM((n, d + SUM_ROWS, tq), F32)]


def _pad_extras(qe):
    return jnp.concatenate([qe, jnp.zeros((LANES - EXTRA_ROWS, qe.shape[1]), qe.dtype)], axis=0)


def _map_rows(q_t, mi):
    row = lax.broadcasted_iota(jnp.int32, q_t.shape, 0)
    keep = (row >= mi * HEAD_DIM) & (row < (mi + 1) * HEAD_DIM)
    return jnp.where(keep, q_t, jnp.zeros_like(q_t))


def _attn_a_prompt_kernel(qt_ref, k_ref, vt_ref, bias_ref, o_ref, m_ref, acc_ref, *, hp):
    qi = pl.program_id(2)
    qm = [_map_rows(qt_ref[0, p * LANES:(p + 1) * LANES, :], mi) for p in range(hp) for mi in range(2)]
    _stats_init(m_ref, acc_ref)
    for j in range(BAND_TILES):
        kt = qi - (BAND_TILES - 1) + j

        def tile(j=j, kt=kt):
            start = pl.multiple_of(kt * TP, TP)
            scores, values = [], []
            for p in range(hp):
                cols = slice(p * LANES, (p + 1) * LANES)
                k = k_ref[0, pl.ds(start, TP), cols]
                v_t = vt_ref[0, cols, pl.ds(start, TP)]
                for mi in range(2):
                    scores.append(_dot(k, qm[2 * p + mi]) + bias_ref[2 * p + mi, j * TP:(j + 1) * TP, :])
                    values.append(_with_ones(v_t[mi * HEAD_DIM:(mi + 1) * HEAD_DIM]))
            _stats_update_all(m_ref, acc_ref, scores, values)

        if j == BAND_TILES - 1:
            tile()
        else:
            pl.when(kt >= 0)(tile)
    o = jnp.concatenate([_stats_result(acc_ref, i) for i in range(2 * hp)], axis=0)
    o_ref[0] = o.astype(o_ref.dtype)


def _attn_a_prompt(q_t, k, v_t, pbias):
    b, s, _ = k.shape
    hp = HP_A
    groups = D_A // (hp * LANES)
    return pl.pallas_call(
        functools.partial(_attn_a_prompt_kernel, hp=hp),
        grid=(b, groups, s // TP),
        in_specs=[pl.BlockSpec((1, hp * LANES, TP), lambda b, g, i: (b, g, i)),
                  pl.BlockSpec((1, s, hp * LANES), lambda b, g, i: (b, 0, g)),
                  pl.BlockSpec((1, hp * LANES, s), lambda b, g, i: (b, g, 0)),
                  pl.BlockSpec((2 * hp, BAND_A, TP), lambda b, g, i: (g, 0, 0))],
        out_specs=pl.BlockSpec((1, hp * LANES, TP), lambda b, g, i: (b, g, i)),
        out_shape=jax.ShapeDtypeStruct((b, D_A, s), BF16),
        scratch_shapes=_stats_scratch(2 * hp, HEAD_DIM, TP),
        compiler_params=_cparams(3, VMEM_LIMIT),
        name="attn_a_prompt",
    )(q_t, k, v_t, pbias)


def _attn_b_prompt_kernel(qt_ref, k_ref, vt_ref, ke_ref, lam_ref, subln_ref, o_ref,
                          m_ref, acc_ref, diag_ref, *, lam_init, hp):
    g = pl.program_id(1)
    qi = pl.program_id(2)
    coefs = [_alibi_slope(g * hp + p) * LOG2E for p in range(hp)]

    @pl.when(qi == 0)
    def _():
        for d in range(DIAG_TILES):
            kr = lax.broadcasted_iota(jnp.int32, (TP, TQW), 0) + d * TP
            qc = lax.broadcasted_iota(jnp.int32, (TP, TQW), 1)
            dist = jnp.abs(qc - kr).astype(F32)
            for p in range(hp):
                diag_ref[p, d] = jnp.where(kr // CHUNK <= qc // CHUNK, -coefs[p] * dist, NEG_INF)

    row = lax.broadcasted_iota(jnp.int32, (EXTRA_ROWS, TQW), 0)
    qpos = (qi * TQW + lax.broadcasted_iota(jnp.int32, (EXTRA_ROWS, TQW), 1)).astype(F32)
    part = row % 3
    qa_past, qa_diag = [], []
    for p in range(hp):
        c = coefs[p]
        q_t = qt_ref[0, p * LANES:(p + 1) * LANES, :]
        hi, mid, lo = _split3(jnp.where(row < 3, CHUNK * c, jnp.where(row < 6, c, -c * qpos)))
        qe = jnp.where(row < 9, jnp.where(part == 0, hi, jnp.where(part == 1, mid, lo)), 0.0)
        qe = _pad_extras(qe.astype(BF16))
        for mi in range(2):
            qm = _map_rows(q_t, mi)
            qa_past.append(jnp.concatenate([qm, qe], axis=0))
            qa_diag.append(jnp.concatenate([qm, jnp.zeros_like(qe)], axis=0))
    _stats_init(m_ref, acc_ref)

    def tile(j, qa, diag):
        start = pl.multiple_of(j * TP, TP)
        ke = ke_ref[pl.ds(start, TP), :]
        scores, values = [], []
        for p in range(hp):
            cols = slice(p * LANES, (p + 1) * LANES)
            kx = jnp.concatenate([k_ref[0, pl.ds(start, TP), cols], ke], axis=1)
            v_t = _with_ones(vt_ref[0, cols, pl.ds(start, TP)])
            for mi in range(2):
                s_t = _dot(kx, qa[2 * p + mi])
                scores.append(s_t if diag is None else s_t + diag_ref[p, diag])
                values.append(v_t)
        _stats_update_all(m_ref, acc_ref, scores, values)

    def past_tile(j, carry):
        tile(j, qa_past, None)
        return carry

    lax.fori_loop(0, DIAG_TILES * qi, past_tile, 0)
    for d in range(DIAG_TILES):
        tile(DIAG_TILES * qi + d, qa_diag, d)
    lam = _diff_lambda(lam_ref, lam_init)
    outs = []
    for p in range(hp):
        o = _stats_result(acc_ref, 2 * p) - lam * _stats_result(acc_ref, 2 * p + 1)
        y = o * lax.rsqrt(jnp.mean(o * o, axis=0, keepdims=True) + EPS)
        outs.append((y * subln_ref[...]) * (1.0 - lam_init))
    o_ref[0] = jnp.concatenate(outs, axis=0).astype(o_ref.dtype)


def _attn_b_prompt(q_t, k, v_t, lam_vec, subln_col, lam_init):
    b, s, _ = k.shape
    hp = HP_B
    groups = D_B // (hp * LANES)
    blk0 = D_A // (hp * LANES)
    pos = jnp.arange(s, dtype=jnp.int32)[:, None]
    lane = jnp.arange(LANES, dtype=jnp.int32)[None, :]
    key_extras = jnp.where(lane < 3, pos // CHUNK,
                           jnp.where(lane < 6, pos % CHUNK, jnp.where(lane < 9, 1, 0))).astype(BF16)
    return pl.pallas_call(
        functools.partial(_attn_b_prompt_kernel, lam_init=lam_init, hp=hp),
        grid=(b, groups, s // TQW),
        in_specs=[pl.BlockSpec((1, hp * LANES, TQW), lambda b, g, i: (b, blk0 + g, i)),
                  pl.BlockSpec((1, s, hp * LANES), lambda b, g, i: (b, 0, blk0 + g)),
                  pl.BlockSpec((1, hp * LANES, s), lambda b, g, i: (b, blk0 + g, 0)),
                  _full(key_extras.shape), _full(lam_vec.shape), _full(subln_col.shape)],
        out_specs=pl.BlockSpec((1, hp * LANES, TQW), lambda b, g, i: (b, g, i)),
        out_shape=jax.ShapeDtypeStruct((b, D_B, s), BF16),
        scratch_shapes=(_stats_scratch(2 * hp, 2 * HEAD_DIM, TQW)
                        + [pltpu.VMEM((hp, DIAG_TILES, TP, TQW), F32)]),
        compiler_params=_cparams(3, VMEM_LIMIT),
        name="attn_b_prompt",
    )(q_t, k, v_t, key_extras, lam_vec, subln_col)


def _logf_scan_kernel(logf_ref, dqt_ref, e_ref):
    r = lax.broadcasted_iota(jnp.int32, (SCAN_T, SCAN_T), 0)
    c = lax.broadcasted_iota(jnp.int32, (SCAN_T, SCAN_T), 1)
    tri = jnp.where(r >= c, 1.0, 0.0).astype(BF16)
    pairs = H_C // 2
    i_sel = lax.broadcasted_iota(jnp.int32, (3 * LANES, pairs * LANES), 0)
    col = lax.broadcasted_iota(jnp.int32, (3 * LANES, pairs * LANES), 1)
    j_sel = col % LANES
    src = ((j_sel - 3) % 3) * LANES + 2 * (col // LANES) + (j_sel - 3) // 3
    sel = jnp.where((j_sel >= 3) & (j_sel < 9) & (i_sel == src), 1.0, 0.0).astype(BF16)
    ones_cols = jnp.where(lax.broadcasted_iota(jnp.int32, (1, pairs * LANES), 1) % LANES < 3, 1.0, 0.0)
    carry = jnp.zeros((1, LANES), F32)
    for t in range(logf_ref.shape[1] // SCAN_T):
        rows = slice(t * SCAN_T, (t + 1) * SCAN_T)
        hi, mid, lo = _split3(logf_ref[0, rows, :])
        cs = (_dot(tri, hi.astype(BF16)) + _dot(tri, mid.astype(BF16)) + _dot(tri, lo.astype(BF16))) + carry
        carry = cs[SCAN_T - 1:SCAN_T, :]
        x = cs * LOG2E
        dqt_ref[0, :, rows] = x.T[:H_C, :]
        hi, mid, lo = _split3(x)
        parts = jnp.concatenate([hi.astype(BF16), mid.astype(BF16), lo.astype(BF16)], axis=1)
        e_all = (_dot(parts, sel) + ones_cols).astype(BF16)
        for g in range(pairs):
            e_ref[0, g, rows, :] = e_all[:, g * LANES:(g + 1) * LANES]


def _logf_scan(logf):
    b, s, _ = logf.shape
    return pl.pallas_call(
        _logf_scan_kernel,
        grid=(b,),
        in_specs=[pl.BlockSpec((1, s, LANES), lambda b: (b, 0, 0))],
        out_specs=[pl.BlockSpec((1, H_C, s), lambda b: (b, 0, 0)),
                   pl.BlockSpec((1, H_C // 2, s, LANES), lambda b: (b, 0, 0, 0))],
        out_shape=[jax.ShapeDtypeStruct((b, H_C, s), F32),
                   jax.ShapeDtypeStruct((b, H_C // 2, s, LANES), BF16)],
        compiler_params=_cparams(1, VMEM_LIMIT),
        name="logf_scan",
    )(logf)


def _attn_c_prompt_kernel(qt_ref, k_ref, vt_ref, dqt_ref, e_ref, o_ref, m_ref, acc_ref, *, hp):
    qi = pl.program_id(2)
    row = lax.broadcasted_iota(jnp.int32, (EXTRA_ROWS, TQW), 0)
    qa = []
    for p in range(hp):
        q_t = qt_ref[0, p * LANES:(p + 1) * LANES, :]
        dq = dqt_ref[0, p]
        for mi in range(2):
            hi, mid, lo = _split3(dq[mi:mi + 1, :])
            dk_rows = (row >= 3 + 3 * mi) & (row < 6 + 3 * mi)
            qe = jnp.where(row == 0, hi, jnp.where(row == 1, mid, jnp.where(row == 2, lo,
                           jnp.where(dk_rows, -1.0, 0.0))))
            qa.append(jnp.concatenate([_map_rows(q_t, mi), _pad_extras(qe.astype(BF16))], axis=0))
    _stats_init(m_ref, acc_ref)

    def tile(j, diag):
        start = pl.multiple_of(j * TP, TP)
        scores, values = [], []
        for p in range(hp):
            cols = slice(p * LANES, (p + 1) * LANES)
            kx = jnp.concatenate([k_ref[0, pl.ds(start, TP), cols], e_ref[0, p, pl.ds(start, TP), :]],
                                 axis=1)
            v_t = vt_ref[0, cols, pl.ds(start, TP)]
            for mi in range(2):
                s_t = _dot(kx, qa[2 * p + mi])
                if diag is not None:
                    kr = lax.broadcasted_iota(jnp.int32, (TP, TQW), 0) + diag * TP
                    qc = lax.broadcasted_iota(jnp.int32, (TP, TQW), 1)
                    s_t = jnp.where(kr <= qc, s_t, NEG_INF)
                scores.append(s_t)
                values.append(_with_ones(v_t[mi * HEAD_DIM:(mi + 1) * HEAD_DIM]))
        _stats_update_all(m_ref, acc_ref, scores, values)

    def past_tile(j, carry):
        tile(j, None)
        return carry

    lax.fori_loop(0, DIAG_TILES * qi, past_tile, 0)
    for d in range(DIAG_TILES):
        tile(DIAG_TILES * qi + d, d)
    o = jnp.concatenate([_stats_result(acc_ref, i) for i in range(2 * hp)], axis=0)
    o_ref[0] = o.astype(o_ref.dtype)


def _attn_c_prompt(q_t, k, v_t, dq_t, key_extras):
    b, s, _ = k.shape
    hp = HP_C
    groups = D_C // (hp * LANES)
    dq_t = dq_t.reshape(b, D_C // LANES, 2, s)
    return pl.pallas_call(
        functools.partial(_attn_c_prompt_kernel, hp=hp),
        grid=(b, groups, s // TQW),
        in_specs=[pl.BlockSpec((1, hp * LANES, TQW), lambda b, g, i: (b, g, i)),
                  pl.BlockSpec((1, s, hp * LANES), lambda b, g, i: (b, 0, g)),
                  pl.BlockSpec((1, hp * LANES, s), lambda b, g, i: (b, g, 0)),
                  pl.BlockSpec((1, hp, 2, TQW), lambda b, g, i: (b, g, 0, i)),
                  pl.BlockSpec((1, hp, s, LANES), lambda b, g, i: (b, g, 0, 0))],
        out_specs=pl.BlockSpec((1, hp * LANES, TQW), lambda b, g, i: (b, g, i)),
        out_shape=jax.ShapeDtypeStruct((b, D_C, s), BF16),
        scratch_shapes=_stats_scratch(2 * hp, HEAD_DIM, TQW),
        compiler_params=_cparams(3, VMEM_LIMIT),
        name="attn_c_prompt",
    )(q_t, k, v_t, dq_t, key_extras)


def _split_maps(q):
    lane = lax.broadcasted_iota(jnp.int32, q.shape, q.ndim - 1)
    zero = jnp.zeros_like(q)
    return jnp.where(lane < HEAD_DIM, q, zero), jnp.where(lane >= HEAD_DIM, q, zero)


def _pair_heads(o0, o1):
    lane = lax.broadcasted_iota(jnp.int32, o0.shape, 1)
    return jnp.where(lane < HEAD_DIM, o0, o1)


def _diff_out(o0, o1, lam, subln, lam_init):
    o = o0 - lam * o1
    y = o * lax.rsqrt(jnp.mean(o * o, axis=-1, keepdims=True) + EPS)
    return (y * subln) * (1.0 - lam_init)


def _sample_scores(q, kn, kc, cache_t):
    q0, q1 = _split_maps(q)
    qs = jnp.concatenate([q0, q1], axis=0)
    s_c = _dot(qs, kc) if cache_t else _dot_nt(qs, kc)
    return s_c, _dot_nt(qs, kn)


def _sample_softmax_pv(s_c, s_n, vn, vc, cache_t):
    n = len(s_c)
    m = [jnp.maximum(jnp.max(s_c[i], axis=-1, keepdims=True), jnp.max(s_n[i], axis=-1, keepdims=True))
         for i in range(n)]
    p_c = [jnp.exp(s_c[i] - m[i]) for i in range(n)]
    p_n = [jnp.exp(s_n[i] - m[i]) for i in range(n)]
    l = [jnp.sum(p_c[i], axis=-1, keepdims=True) + jnp.sum(p_n[i], axis=-1, keepdims=True)
         for i in range(n)]
    outs = []
    for i in range(n):
        pc = p_c[i].astype(BF16)
        o_c = _dot_nt(pc, vc[i]) if cache_t else _dot(pc, vc[i])
        o = (o_c + _dot(p_n[i].astype(BF16), vn[i])) * (1.0 / l[i])
        t = o.shape[0] // 2
        outs.append((o[:t], o[t:]))
    return outs


def _cache_pair_t(ref, p):
    blk = ref[0, 0, 2 * p:2 * p + 2]
    return blk.reshape(2 * HEAD_DIM, blk.shape[-1]).astype(BF16)


def _attn_a_sample_kernel(q_ref, kn_ref, vn_ref, kc_ref, vc_ref, bias_ref, o_ref, *, hp):
    t = q_ref.shape[1]
    w = kc_ref.shape[-1]
    s_c, s_n, vn, vc = [], [], [], []
    for p in range(hp):
        cols = slice(p * LANES, (p + 1) * LANES)
        sc, sn = _sample_scores(q_ref[0, :, cols], kn_ref[0, :, cols], _cache_pair_t(kc_ref, p), True)
        bias = bias_ref[2 * p:2 * p + 2].reshape(2 * t, SBIAS_W)
        s_c.append(sc + bias[:, :w])
        s_n.append(sn + bias[:, w:w + t])
        vn.append(vn_ref[0, :, cols])
        vc.append(_cache_pair_t(vc_ref, p))
    outs = _sample_softmax_pv(s_c, s_n, vn, vc, True)
    for p in range(hp):
        o_ref[0, :, p * LANES:(p + 1) * LANES] = _pair_heads(*outs[p]).astype(o_ref.dtype)


def _attn_c_sample_kernel(q_ref, kn_ref, vn_ref, kc_ref, vc_ref, logf_ref, clogf_ref, o_ref,
                          dpast_ref, *, hp):
    g = pl.program_id(1)
    t = q_ref.shape[1]
    past = clogf_ref.shape[-1]

    @pl.when(g == 0)
    def _():
        r = lax.broadcasted_iota(jnp.int32, (SCAN_T, SCAN_T), 0)
        c = lax.broadcasted_iota(jnp.int32, (SCAN_T, SCAN_T), 1)
        triu = jnp.where(r <= c, 1.0, 0.0).astype(F32)
        carry = jnp.zeros((H_C, 1), F32)
        for i in range(past // SCAN_T):
            cols = slice(i * SCAN_T, (i + 1) * SCAN_T)
            cs = _dot_exact(clogf_ref[0, 0, :, cols], triu) + carry
            dpast_ref[:, cols] = cs
            carry = cs[:, SCAN_T - 1:SCAN_T]
        dpast_ref[...] = dpast_ref[...] - carry

    r = lax.broadcasted_iota(jnp.int32, (t, t), 0)
    c = lax.broadcasted_iota(jnp.int32, (t, t), 1)
    tri = jnp.where(r >= c, 1.0, 0.0).astype(F32)
    dq = _dot_exact(tri, logf_ref[0])
    er = lax.broadcasted_iota(jnp.int32, (LANES, LANES), 0)
    ec = lax.broadcasted_iota(jnp.int32, (LANES, LANES), 1)
    eye = jnp.where(er == ec, 1.0, 0.0).astype(F32)
    dq_t = _dot_nt_exact(eye, dq)
    dpast = dpast_ref[...]
    top = lax.broadcasted_iota(jnp.int32, (2 * t, 1), 0) < t
    qrow = lax.broadcasted_iota(jnp.int32, (2 * t, t), 0) % t
    kcol = lax.broadcasted_iota(jnp.int32, (2 * t, t), 1)
    s_c, s_n, vn, vc = [], [], [], []
    for p in range(hp):
        cols = slice(p * LANES, (p + 1) * LANES)
        h0 = 2 * (g * hp + p)
        dq_col = jnp.concatenate([_select_lane(dq, h0), _select_lane(dq, h0 + 1)], axis=0)
        dk_c = jnp.where(top, _select_row(dpast, h0), _select_row(dpast, h0 + 1))
        dk_n = jnp.where(top, _select_row(dq_t, h0), _select_row(dq_t, h0 + 1))
        sc, sn = _sample_scores(q_ref[0, :, cols], kn_ref[0, :, cols], _cache_pair_t(kc_ref, p), True)
        s_c.append(sc + dq_col - dk_c)
        s_n.append(jnp.where(kcol <= qrow, sn + dq_col - dk_n, NEG_INF))
        vn.append(vn_ref[0, :, cols])
        vc.append(_cache_pair_t(vc_ref, p))
    outs = _sample_softmax_pv(s_c, s_n, vn, vc, True)
    for p in range(hp):
        o_ref[0, :, p * LANES:(p + 1) * LANES] = _pair_heads(*outs[p]).astype(o_ref.dtype)


def _attn_pair_sample(kernel_fn, name, layer, qkv, cache_k_t, cache_v_t, extras, extra_specs,
                      scratch=()):
    b, t, _ = qkv.shape
    heads, _, past = cache_k_t.shape[2:]
    hp = HP_SAMPLE
    groups = heads // (2 * hp)
    width = hp * LANES
    cache_spec = pl.BlockSpec((1, 1, 2 * hp, HEAD_DIM, past), lambda b, g: (layer, b, g, 0, 0))
    return pl.pallas_call(
        functools.partial(kernel_fn, hp=hp),
        grid=(b, groups),
        in_specs=[pl.BlockSpec((1, t, width), lambda b, g: (b, 0, g)),
                  pl.BlockSpec((1, t, width), lambda b, g: (b, 0, groups + g)),
                  pl.BlockSpec((1, t, width), lambda b, g: (b, 0, 2 * groups + g)),
                  cache_spec, cache_spec] + list(extra_specs),
        out_specs=pl.BlockSpec((1, t, width), lambda b, g: (b, 0, g)),
        out_shape=jax.ShapeDtypeStruct((b, t, heads * HEAD_DIM), BF16),
        scratch_shapes=list(scratch),
        compiler_params=_cparams(2, VMEM_LIMIT),
        name=name,
    )(qkv, qkv, qkv, cache_k_t, cache_v_t, *extras)


def _attn_b_sample_kernel(q_ref, kn_ref, vn_ref, kc_ref, vc_ref, lam_ref, subln_ref, o_ref,
                          *, lam_init):
    t = q_ref.shape[1]
    past = kc_ref.shape[2] // H_B
    shape_c = (2 * t, past)
    shape_n = (2 * t, t)
    qpos_c = past + lax.broadcasted_iota(jnp.int32, shape_c, 0) % t
    dist_c = jnp.abs(qpos_c - lax.broadcasted_iota(jnp.int32, shape_c, 1)).astype(F32)
    dist_n = jnp.abs(lax.broadcasted_iota(jnp.int32, shape_n, 0) % t
                     - lax.broadcasted_iota(jnp.int32, shape_n, 1)).astype(F32)
    lam = _diff_lambda(lam_ref, lam_init)
    s_c, s_n, vn, vc = [], [], [], []
    for h in range(H_B):
        cols = slice(h * LANES, (h + 1) * LANES)
        slope = 2.0 ** (-8.0 * (h + 1) / H_B)
        kc = kc_ref[0, 0, pl.ds(h, past, stride=H_B), :].astype(BF16)
        sc, sn = _sample_scores(q_ref[0, :, cols], kn_ref[0, :, cols], kc, False)
        s_c.append(sc - slope * dist_c)
        s_n.append(sn - slope * dist_n)
        vn.append(vn_ref[0, :, cols])
        vc.append(vc_ref[0, 0, pl.ds(h, past, stride=H_B), :].astype(BF16))
    outs = _sample_softmax_pv(s_c, s_n, vn, vc, False)
    for h in range(H_B):
        o_ref[0, :, h * LANES:(h + 1) * LANES] = _diff_out(
            outs[h][0], outs[h][1], lam, subln_ref[...], lam_init).astype(o_ref.dtype)


def _attn_b_sample(layer, qkv, cache_k, cache_v, lam_vec, subln, lam_init):
    b, t, _ = qkv.shape
    rows = cache_k.shape[2]
    col0 = 3 * D_A // D_B
    cache_spec = pl.BlockSpec((1, 1, rows, LANES), lambda b: (layer, b, 0, 0))
    return pl.pallas_call(
        functools.partial(_attn_b_sample_kernel, lam_init=lam_init),
        grid=(b,),
        in_specs=[pl.BlockSpec((1, t, D_B), lambda b: (b, 0, col0)),
                  pl.BlockSpec((1, t, D_B), lambda b: (b, 0, col0 + 1)),
                  pl.BlockSpec((1, t, D_B), lambda b: (b, 0, col0 + 2)),
                  cache_spec, cache_spec, _full(lam_vec.shape), _full(subln.shape)],
        out_specs=pl.BlockSpec((1, t, D_B), lambda b: (b, 0, 0)),
        out_shape=jax.ShapeDtypeStruct((b, t, D_B), BF16),
        compiler_params=_cparams(1, VMEM_LIMIT),
        name="attn_b_sample",
    )(qkv, qkv, qkv, cache_k, cache_v, lam_vec, subln)


def _heads_last(a_t, heads):
    lead = a_t.shape[:-2]
    s = a_t.shape[-1]
    a = a_t.reshape(lead + (heads, HEAD_DIM, s))
    n = len(lead)
    return jnp.transpose(a, tuple(range(n)) + (n + 2, n, n + 1))


def kernel(x_prompt, x_sample, c_prompt, c_sample, cache_a_k, cache_a_v, cache_b_k, cache_b_v, cache_c_k, cache_c_v, cache_c_logf, w_ada, b_ada, norm_pre, norm_post, ffn_w_gate, ffn_w_up, ffn_w_down, w_in_ab, w_out_ab, relpos_a, lambda_b, subln_b, w_in_c, b_f, w_out_c):
    n_prompt, s_prompt = x_prompt.shape[:2]
    n_sample, t_sample = x_sample.shape[:2]

    wg = ffn_w_gate.astype(BF16)
    wu = ffn_w_up.astype(BF16)
    wd = ffn_w_down.astype(BF16)
    w_in_ab16 = w_in_ab.astype(BF16)
    w_in_ab16_t = jnp.swapaxes(w_in_ab, 1, 2).astype(BF16)
    w_out_ab16 = w_out_ab.astype(BF16)
    w_out_ab16_t = jnp.swapaxes(w_out_ab, 1, 2).astype(BF16)
    w_qkv_c16 = w_in_c[:, :, :3 * D_C].astype(BF16)
    w_qkv_c16_t = jnp.swapaxes(w_in_c[:, :, :3 * D_C], 1, 2).astype(BF16)
    w_f16 = jnp.pad(w_in_c[:, :, 3 * D_C:], ((0, 0), (0, 0), (0, LANES - H_C))).astype(BF16)
    w_f16_t = jnp.swapaxes(w_f16, 1, 2)
    b_f_row = jnp.pad(b_f, ((0, 0), (0, LANES - H_C)))[:, None, :]
    b_f_col = jnp.swapaxes(b_f_row, 1, 2)
    w_out_c16 = w_out_c.astype(BF16)
    w_out_c16_t = jnp.swapaxes(w_out_c, 1, 2).astype(BF16)
    cache_a_k_t = jnp.transpose(cache_a_k, (0, 1, 3, 4, 2))
    cache_a_v_t = jnp.transpose(cache_a_v, (0, 1, 3, 4, 2))
    cache_c_k_t = jnp.transpose(cache_c_k, (0, 1, 3, 4, 2))
    cache_c_v_t = jnp.transpose(cache_c_v, (0, 1, 3, 4, 2))
    clogf_t = jnp.swapaxes(cache_c_logf, -1, -2)
    past_b = cache_b_k.shape[2]
    cache_b_k_rows = cache_b_k.reshape(cache_b_k.shape[:2] + (past_b * H_B, 2 * HEAD_DIM))
    cache_b_v_rows = cache_b_v.reshape(cache_b_v.shape[:2] + (past_b * H_B, 2 * HEAD_DIM))

    rows = n_prompt + n_sample
    rows_pad = -(-rows // 8) * 8
    c_all = jnp.concatenate([c_prompt, c_sample, jnp.zeros((rows_pad - rows, D_MODEL), F32)], axis=0)
    mod_all = _ada(c_all, w_ada, b_ada).reshape(DEPTH, rows_pad, N_SUB, 3, D_MODEL)
    pbias, sbias = _relpos_bias(relpos_a, t_sample)

    def sublayer_params(li, row0, nrows):
        mod = [mod_all[li, row0:row0 + nrows, i] for i in range(N_SUB)]
        gpre = [norm_pre[li, i][None, :] for i in range(N_SUB)]
        gpost = [norm_post[li, i][None, :] for i in range(N_SUB)]
        return mod, gpre, gpost

    def run_prompt(x):
        a_k, a_v, c_lf = [], [], []
        c_k = jnp.zeros((DEPTH // 2, n_prompt, D_C, s_prompt), F32)
        c_v = jnp.zeros((DEPTH // 2, n_prompt, D_C, s_prompt), F32)
        b_k = jnp.zeros(((DEPTH + 1) // 2, n_prompt, s_prompt * H_B, 2 * HEAD_DIM), F32)
        b_v = jnp.zeros(((DEPTH + 1) // 2, n_prompt, s_prompt * H_B, 2 * HEAD_DIM), F32)
        for li in range(DEPTH):
            mod, gpre, gpost = sublayer_params(li, 0, n_prompt)
            j = li // 2
            if li % 2 == 0:
                lam_init = _lam_init(li)
                x, q_t, k, v_t, ka_t, va_t, b_k, b_v = _ffn_inproj_ab_prompt(
                    x, mod[0], mod[1], gpre[0], gpost[0], wg, wu, wd, li, gpre[1],
                    w_in_ab16[j], w_in_ab16_t[j], b_k, b_v, j)
                out_a = _attn_a_prompt(q_t, k, v_t, pbias[j])
                out_b = _attn_b_prompt(q_t, k, v_t, lambda_b[j], subln_b[j][:, None], lam_init)
                w = min(A_PAST, s_prompt)
                a_k.append(ka_t[:, :, s_prompt - w:])
                a_v.append(va_t[:, :, s_prompt - w:])
                outs, w_outs = [out_a, out_b], [w_out_ab16_t[j, :, :D_A], w_out_ab16_t[j, :, D_A:]]
            else:
                x, q_t, k, v_t, c_k, c_v, logf, logf_t = _ffn_inproj_c_prompt(
                    x, mod[0], mod[1], gpre[0], gpost[0], wg, wu, wd, li, gpre[1],
                    w_qkv_c16[j, :, D_C:2 * D_C], w_qkv_c16_t[j],
                    w_f16[j], w_f16_t[j], b_f_row[j], b_f_col[j], c_k, c_v, j)
                dq_t, key_extras = _logf_scan(logf)
                outs, w_outs = [_attn_c_prompt(q_t, k, v_t, dq_t, key_extras)], [w_out_c16_t[j]]
                c_lf.append(logf_t)
            x = _outproj_prompt(x, mod[1], gpost[1], outs, w_outs)
            x = _ffn(x, mod[2], gpre[2], gpost[2], wg, wu, wd, li, 1)
        nb, sb = x.shape[:2]
        return (x,
                _heads_last(jnp.stack(a_k), H_A), _heads_last(jnp.stack(a_v), H_A),
                b_k.reshape(-1, nb, sb, H_B, 2 * HEAD_DIM),
                b_v.reshape(-1, nb, sb, H_B, 2 * HEAD_DIM),
                _heads_last(c_k, H_C), _heads_last(c_v, H_C),
                jnp.swapaxes(jnp.stack(c_lf), -1, -2))

    def run_sample(x):
        a_k, a_v, b_k, b_v, c_k, c_v, c_lf = [], [], [], [], [], [], []
        for li in range(DEPTH):
            mod, gpre, gpost = sublayer_params(li, n_prompt, n_sample)
            x = _ffn(x, mod[0], gpre[0], gpost[0], wg, wu, wd, li, 0)
            j = li // 2
            if li % 2 == 0:
                lam_init = _lam_init(li)
                qkv, ka, va, kb, vb = _inproj_ab(x, mod[1], gpre[1], w_in_ab16[j])
                subln = subln_b[j][None, :]
                out_a = _attn_pair_sample(
                    _attn_a_sample_kernel, "attn_a_sample", j, qkv, cache_a_k_t, cache_a_v_t,
                    [sbias[j]],
                    [pl.BlockSpec((2 * HP_SAMPLE, t_sample, SBIAS_W), lambda b, g: (g, 0, 0))])
                out_b = _attn_b_sample(j, qkv, cache_b_k_rows, cache_b_v_rows, lambda_b[j], subln,
                                       lam_init)
                a_k.append(ka)
                a_v.append(va)
                b_k.append(kb)
                b_v.append(vb)
                x = _outproj(x, mod[1], gpost[1], [out_a, out_b],
                             [w_out_ab16[j, :D_A], w_out_ab16[j, D_A:]])
            else:
                qkv, k, v, logf = _inproj_c(x, mod[1], gpre[1], w_qkv_c16[j], w_f16[j], b_f_row[j])
                past = clogf_t.shape[-1]
                out = _attn_pair_sample(
                    _attn_c_sample_kernel, "attn_c_sample", j, qkv, cache_c_k_t, cache_c_v_t,
                    [logf, clogf_t],
                    [pl.BlockSpec((1, t_sample, LANES), lambda b, g: (b, 0, 0)),
                     pl.BlockSpec((1, 1, H_C, past), lambda b, g, j=j: (j, b, 0, 0))],
                    scratch=[pltpu.VMEM((H_C, past), F32)])
                c_k.append(k)
                c_v.append(v)
                c_lf.append(logf[:, :, :H_C])
                x = _outproj(x, mod[1], gpost[1], [out], [w_out_c16[j]])
            x = _ffn(x, mod[2], gpre[2], gpost[2], wg, wu, wd, li, 1)
        nb, sb = x.shape[:2]
        return (x,
                jnp.stack(a_k).reshape(-1, nb, sb, H_A, HEAD_DIM),
                jnp.stack(a_v).reshape(-1, nb, sb, H_A, HEAD_DIM),
                jnp.stack(b_k).reshape(-1, nb, sb, H_B, 2 * HEAD_DIM),
                jnp.stack(b_v).reshape(-1, nb, sb, H_B, 2 * HEAD_DIM),
                jnp.stack(c_k).reshape(-1, nb, sb, H_C, HEAD_DIM),
                jnp.stack(c_v).reshape(-1, nb, sb, H_C, HEAD_DIM),
                jnp.stack(c_lf))

    p = run_prompt(x_prompt)
    s = run_sample(x_sample)
    return (p[0], s[0]) + tuple(p[1:]) + tuple(s[1:])
```

```python
import functools
import math

import jax
import jax.numpy as jnp
from jax import lax
from jax.experimental import pallas as pl
from jax.experimental.pallas import tpu as pltpu

D_MODEL = 1024
DEPTH = 4
CHUNK = 64
HEAD_DIM = 64
H_A = 8
H_B = 4
H_C = 16
D_A = H_A * HEAD_DIM
D_B = H_B * 2 * HEAD_DIM
D_C = H_C * HEAD_DIM
A_LEFT_CHUNKS = 8
A_PAST = A_LEFT_CHUNKS * CHUNK
REL_CLIP = 128
D_FF = 2816
N_SUB = 3
MACARON_W = 0.5
EPS = 1e-6
NEG_INF = -1e30
ATTN_SCALE = HEAD_DIM ** -0.5
LOG2E = 1.4426950408889634
QSCALE_LOG2 = ATTN_SCALE * LOG2E

LANES = 128
TOK_TILE = 256
WIDE_TOK_TILE = 512
SAMPLE_GROUP = 8
TP = 256
EXTRA_ROWS = 16
SUM_ROWS = 16
DIAG_TILES = 2
TQW = DIAG_TILES * TP
HP_A = 4
HP_B = 4
HP_C = 4
HP_SAMPLE = 4
BAND_TILES = 3
BAND_A = BAND_TILES * TP
SCAN_T = 512
TAB_PAD = 384
TOEP = 1024
SBIAS_W = 640
ADA_TN = 1536
VMEM_LIMIT = 56 * 1024 * 1024

BF16 = jnp.bfloat16
F32 = jnp.float32


def _lam_init(li):
    return 0.8 - 0.6 * math.exp(-0.3 * li)


def _cparams(n_axes, vmem=None):
    return pltpu.CompilerParams(dimension_semantics=("arbitrary",) * n_axes,
                                vmem_limit_bytes=vmem)


def _dot(a, b):
    return jnp.dot(a, b, preferred_element_type=F32)


def _dot_nt(a, b):
    return lax.dot_general(a, b, (((1,), (1,)), ((), ())), preferred_element_type=F32)


def _dot_exact(a, b):
    return jnp.dot(a, b, preferred_element_type=F32, precision=lax.Precision.HIGHEST)


def _dot_nt_exact(a, b):
    return lax.dot_general(a, b, (((1,), (1,)), ((), ())), preferred_element_type=F32,
                           precision=lax.Precision.HIGHEST)


def _split3(x):
    hi = x.astype(BF16).astype(F32)
    r1 = x - hi
    mid = r1.astype(BF16).astype(F32)
    lo = (r1 - mid).astype(BF16).astype(F32)
    return hi, mid, lo


def _norm_mod(x, m, g_pre):
    y = x * lax.rsqrt(jnp.mean(x * x, axis=-1, keepdims=True) + EPS)
    return (y * g_pre) * (1.0 + m[:, 1:2, :]) + m[:, 0:1, :]


def _gated_residual(x, m, g_post, out, res_w):
    y = out * lax.rsqrt(jnp.mean(out * out, axis=-1, keepdims=True) + EPS)
    return x + (res_w * (1.0 + m[:, 2:3, :])) * (y * g_post)


def _tok_layout(x, tile=TOK_TILE):
    nb_total, sb_total, _ = x.shape
    if sb_total >= tile:
        nb, sb = 1, tile
    else:
        nb, sb = SAMPLE_GROUP, sb_total
    tiles = sb_total // sb
    grid = (nb_total // nb) * tiles

    def spec(width, rows_per_token=1):
        return pl.BlockSpec((nb, sb * rows_per_token, width), lambda i: (i // tiles, i % tiles, 0))

    def spec_t(width):
        return pl.BlockSpec((nb, width, sb), lambda i: (i // tiles, 0, i % tiles))

    mod_spec = pl.BlockSpec((nb, 3, D_MODEL), lambda i: (i // tiles, 0, 0))
    return nb, sb, grid, spec, spec_t, mod_spec


def _full(shape):
    return pl.BlockSpec(shape, lambda *_: (0,) * len(shape), pipeline_mode=pl.Buffered(1))


def _ada_kernel(c_ref, w_ref, b_ref, o_ref):
    c = c_ref[...]
    a = (c * jax.nn.sigmoid(c)).astype(BF16)
    o_ref[0] = _dot(a, w_ref[0].astype(BF16)) + b_ref[0]


def _ada(c_all, w_ada, b_ada):
    rows = c_all.shape[0]
    n_out = w_ada.shape[-1]
    return pl.pallas_call(
        _ada_kernel,
        grid=(DEPTH, n_out // ADA_TN),
        in_specs=[pl.BlockSpec((rows, D_MODEL), lambda l, n: (0, 0)),
                  pl.BlockSpec((1, D_MODEL, ADA_TN), lambda l, n: (l, 0, n)),
                  pl.BlockSpec((1, 1, ADA_TN), lambda l, n: (l, 0, n))],
        out_specs=pl.BlockSpec((1, rows, ADA_TN), lambda l, n: (l, 0, n)),
        out_shape=jax.ShapeDtypeStruct((DEPTH, rows, n_out), F32),
        compiler_params=_cparams(2, VMEM_LIMIT),
        name="ada",
    )(c_all, w_ada, b_ada.reshape(DEPTH, 1, n_out))


def _ffn_body(x, m, g_pre, g_post, wg_ref, wu_ref, wd_ref):
    nb, sb, d = x.shape
    h = _norm_mod(x, m, g_pre).reshape(nb * sb, d).astype(BF16)
    g = _dot(h, wg_ref[0, 0])
    u = _dot(h, wu_ref[0, 0])
    a = ((g * jax.nn.sigmoid(g)) * u).astype(BF16)
    out = _dot(a, wd_ref[0, 0]).reshape(nb, sb, d)
    return _gated_residual(x, m, g_post, out, MACARON_W)


def _ffn_weight_specs(li, k):
    def one(rows, cols):
        return pl.BlockSpec((1, 1, rows, cols), lambda *_: (li, k, 0, 0), pipeline_mode=pl.Buffered(1))
    return [one(D_MODEL, D_FF), one(D_MODEL, D_FF), one(D_FF, D_MODEL)]


def _ffn_kernel(x_ref, mod_ref, gpre_ref, gpost_ref, wg_ref, wu_ref, wd_ref, o_ref):
    o_ref[...] = _ffn_body(x_ref[...], mod_ref[...], gpre_ref[...], gpost_ref[...],
                           wg_ref, wu_ref, wd_ref)


def _ffn(x, mod, g_pre, g_post, wg, wu, wd, li, k):
    nb, sb, grid, spec, _, mod_spec = _tok_layout(x, WIDE_TOK_TILE)
    return pl.pallas_call(
        _ffn_kernel,
        grid=(grid,),
        in_specs=[spec(D_MODEL), mod_spec, _full((1, D_MODEL)), _full((1, D_MODEL))]
        + _ffn_weight_specs(li, k),
        out_specs=spec(D_MODEL),
        out_shape=jax.ShapeDtypeStruct(x.shape, F32),
        compiler_params=_cparams(1, VMEM_LIMIT),
        name="ffn",
    )(x, mod, g_pre, g_post, wg, wu, wd)


def _inproj_ab_kernel(x_ref, mod_ref, gpre_ref, w_ref, qkv_ref, ka_ref, va_ref, kb_ref, vb_ref):
    x = x_ref[...]
    nb, sb, d = x.shape
    h = _norm_mod(x, mod_ref[...], gpre_ref[...]).reshape(nb * sb, d).astype(BF16)
    f32_outs = {1: ka_ref, 2: va_ref, 4: kb_ref, 5: vb_ref}
    for c in range(6):
        cols = slice(c * D_A, (c + 1) * D_A)
        p = _dot(h, w_ref[:, cols]).reshape(nb, sb, D_A)
        if c in f32_outs:
            f32_outs[c][...] = p
            qkv_ref[:, :, cols] = p.astype(BF16)
        else:
            qkv_ref[:, :, cols] = (p * ATTN_SCALE).astype(BF16)


def _inproj_ab(x, mod, g_pre, w_in):
    nb, sb, grid, spec, _, mod_spec = _tok_layout(x)
    nbt, sbt, _ = x.shape
    f32_out = jax.ShapeDtypeStruct((nbt, sbt, D_A), F32)
    return pl.pallas_call(
        _inproj_ab_kernel,
        grid=(grid,),
        in_specs=[spec(D_MODEL), mod_spec, _full((1, D_MODEL)), _full((D_MODEL, 6 * D_A))],
        out_specs=[spec(6 * D_A)] + [spec(D_A)] * 4,
        out_shape=[jax.ShapeDtypeStruct((nbt, sbt, 6 * D_A), BF16)] + [f32_out] * 4,
        compiler_params=_cparams(1, VMEM_LIMIT),
        name="inproj_ab",
    )(x, mod, g_pre, w_in)


def _log_sigmoid(z):
    return jnp.minimum(z, 0.0) - jnp.log1p(jnp.exp(-jnp.abs(z)))


def _inproj_c_kernel(x_ref, mod_ref, gpre_ref, w_ref, wf_ref, bf_ref,
                     qkv_ref, k_ref, v_ref, logf_ref):
    x = x_ref[...]
    nb, sb, d = x.shape
    h = _norm_mod(x, mod_ref[...], gpre_ref[...]).reshape(nb * sb, d).astype(BF16)
    f32_outs = {1: k_ref, 2: v_ref}
    for c in range(3):
        cols = slice(c * D_C, (c + 1) * D_C)
        p = _dot(h, w_ref[:, cols]).reshape(nb, sb, D_C)
        if c in f32_outs:
            f32_outs[c][...] = p
            qkv_ref[:, :, cols] = p.astype(BF16)
        else:
            qkv_ref[:, :, cols] = (p * ATTN_SCALE).astype(BF16)
    logf = _log_sigmoid(_dot(h, wf_ref[...]) + bf_ref[...])
    logf_ref[...] = logf.reshape(nb, sb, LANES)


def _inproj_c(x, mod, g_pre, w_qkv, w_f, b_f):
    nb, sb, grid, spec, _, mod_spec = _tok_layout(x)
    nbt, sbt, _ = x.shape
    f32_out = jax.ShapeDtypeStruct((nbt, sbt, D_C), F32)
    return pl.pallas_call(
        _inproj_c_kernel,
        grid=(grid,),
        in_specs=[spec(D_MODEL), mod_spec, _full((1, D_MODEL)), _full((D_MODEL, 3 * D_C)),
                  _full((D_MODEL, LANES)), _full((1, LANES))],
        out_specs=[spec(3 * D_C), spec(D_C), spec(D_C), spec(LANES)],
        out_shape=[jax.ShapeDtypeStruct((nbt, sbt, 3 * D_C), BF16), f32_out, f32_out,
                   jax.ShapeDtypeStruct((nbt, sbt, LANES), F32)],
        compiler_params=_cparams(1, VMEM_LIMIT),
        name="inproj_c",
    )(x, mod, g_pre, w_qkv, w_f, b_f)


def _outproj_kernel(*refs, n_in):
    x_ref, mod_ref, gpost_ref = refs[:3]
    a_refs = refs[3:3 + n_in]
    w_refs = refs[3 + n_in:3 + 2 * n_in]
    o_ref = refs[-1]
    x = x_ref[...]
    nb, sb, d = x.shape
    out = None
    for a_ref, w_ref in zip(a_refs, w_refs):
        a = a_ref[...]
        t = _dot(a.reshape(nb * sb, a.shape[-1]), w_ref[...])
        out = t if out is None else out + t
    o_ref[...] = _gated_residual(x, mod_ref[...], gpost_ref[...], out.reshape(nb, sb, d), 1.0)


def _outproj(x, mod, g_post, acts, weights):
    nb, sb, grid, spec, _, mod_spec = _tok_layout(x)
    n_in = len(acts)
    return pl.pallas_call(
        functools.partial(_outproj_kernel, n_in=n_in),
        grid=(grid,),
        in_specs=([spec(D_MODEL), mod_spec, _full((1, D_MODEL))]
                  + [spec(a.shape[-1]) for a in acts]
                  + [_full(w.shape) for w in weights]),
        out_specs=spec(D_MODEL),
        out_shape=jax.ShapeDtypeStruct(x.shape, F32),
        compiler_params=_cparams(1, VMEM_LIMIT),
        name="outproj",
    )(x, mod, g_post, *acts, *weights)


def _inproj_ab_prompt_kernel(x_ref, mod0_ref, mod_ref, gpre0_ref, gpost0_ref, wg_ref, wu_ref, wd_ref,
                             gpre_ref, w_ref, wt_ref, kb_in_ref, vb_in_ref,
                             xo_ref, qt_ref, k_ref, vt_ref, kat_ref, vat_ref, kb_ref, vb_ref):
    del kb_in_ref, vb_in_ref
    x = _ffn_body(x_ref[...], mod0_ref[...], gpre0_ref[...], gpost0_ref[...], wg_ref, wu_ref, wd_ref)
    xo_ref[...] = x
    _, sb, d = x.shape
    h = _norm_mod(x, mod_ref[...], gpre_ref[...]).reshape(sb, d).astype(BF16)

    def nat(c):
        return _dot(h, w_ref[:, c * D_A:(c + 1) * D_A])

    def tr(c):
        return _dot_nt(wt_ref[c * D_A:(c + 1) * D_A, :], h)

    qt_ref[0, :D_A, :] = (tr(0) * QSCALE_LOG2).astype(BF16)
    qt_ref[0, D_A:, :] = (tr(3) * QSCALE_LOG2).astype(BF16)
    ka_t = tr(1)
    kat_ref[0] = ka_t
    k_ref[0, :, :D_A] = ka_t.T.astype(BF16)
    kb = nat(4)
    k_ref[0, :, D_A:] = kb.astype(BF16)
    va_t = tr(2)
    vat_ref[0] = va_t
    vt_ref[0, :D_A, :] = va_t.astype(BF16)
    vb = nat(5)
    vt_ref[0, D_A:, :] = vb.T.astype(BF16)
    for hd in range(H_B):
        kb_ref[0, 0, pl.ds(hd, sb, stride=H_B), :] = kb[:, hd * LANES:(hd + 1) * LANES]
        vb_ref[0, 0, pl.ds(hd, sb, stride=H_B), :] = vb[:, hd * LANES:(hd + 1) * LANES]


def _ffn_inproj_ab_prompt(x, mod0, mod1, g_pre0, g_post0, wg, wu, wd, li, g_pre1, w_in, w_in_t,
                          kb_stack, vb_stack, j):
    nb, sb, grid, spec, spec_t, mod_spec = _tok_layout(x)
    b, s, _ = x.shape
    tiles = s // sb
    width = D_A + D_B
    vec = _full((1, D_MODEL))
    stack_spec = pl.BlockSpec((1, nb, sb * H_B, LANES), lambda i: (j, i // tiles, i % tiles, 0))
    in_arrays = (x, mod0, mod1, g_pre0, g_post0, wg, wu, wd, g_pre1, w_in, w_in_t, kb_stack, vb_stack)
    return pl.pallas_call(
        _inproj_ab_prompt_kernel,
        grid=(grid,),
        in_specs=[spec(D_MODEL), mod_spec, mod_spec, vec, vec] + _ffn_weight_specs(li, 0)
        + [vec, _full(w_in.shape), _full(w_in_t.shape),
           pl.BlockSpec(memory_space=pl.ANY), pl.BlockSpec(memory_space=pl.ANY)],
        out_specs=[spec(D_MODEL), spec_t(width), spec(width), spec_t(width), spec_t(D_A), spec_t(D_A),
                   stack_spec, stack_spec],
        out_shape=[jax.ShapeDtypeStruct(x.shape, F32),
                   jax.ShapeDtypeStruct((b, width, s), BF16),
                   jax.ShapeDtypeStruct((b, s, width), BF16),
                   jax.ShapeDtypeStruct((b, width, s), BF16),
                   jax.ShapeDtypeStruct((b, D_A, s), F32),
                   jax.ShapeDtypeStruct((b, D_A, s), F32),
                   jax.ShapeDtypeStruct(kb_stack.shape, F32),
                   jax.ShapeDtypeStruct(vb_stack.shape, F32)],
        input_output_aliases={len(in_arrays) - 2: 6, len(in_arrays) - 1: 7},
        compiler_params=_cparams(1, VMEM_LIMIT),
        name="ffn_inproj_ab_prompt",
    )(*in_arrays)


def _inproj_c_prompt_kernel(x_ref, mod0_ref, mod_ref, gpre0_ref, gpost0_ref, wg_ref, wu_ref, wd_ref,
                            gpre_ref, wk_ref, wt_ref, wf_ref, wft_ref, bf_ref, bft_ref,
                            kt32_in_ref, vt32_in_ref,
                            xo_ref, qt_ref, k_ref, vt_ref, kt32_ref, vt32_ref, logf_ref, logft_ref):
    del kt32_in_ref, vt32_in_ref
    x = _ffn_body(x_ref[...], mod0_ref[...], gpre0_ref[...], gpost0_ref[...], wg_ref, wu_ref, wd_ref)
    xo_ref[...] = x
    _, sb, d = x.shape
    h = _norm_mod(x, mod_ref[...], gpre_ref[...]).reshape(sb, d).astype(BF16)
    qt_ref[0] = (_dot_nt(wt_ref[:D_C, :], h) * QSCALE_LOG2).astype(BF16)
    k_t = _dot_nt(wt_ref[D_C:2 * D_C, :], h)
    kt32_ref[0, 0] = k_t
    k_ref[0] = k_t.T.astype(BF16)
    v_t = _dot_nt(wt_ref[2 * D_C:, :], h)
    vt32_ref[0, 0] = v_t
    vt_ref[0] = v_t.astype(BF16)
    logf_ref[0] = _log_sigmoid(_dot(h, wf_ref[...]) + bf_ref[...])
    logft_ref[0] = _log_sigmoid(_dot_nt(wft_ref[...], h) + bft_ref[...])[:H_C, :]


def _ffn_inproj_c_prompt(x, mod0, mod1, g_pre0, g_post0, wg, wu, wd, li, g_pre1,
                         w_k, w_qkv_t, w_f, w_f_t, b_f, b_f_t, kt_stack, vt_stack, j):
    nb, sb, grid, spec, spec_t, mod_spec = _tok_layout(x)
    b, s, _ = x.shape
    tiles = s // sb
    vec = _full((1, D_MODEL))
    stack_spec = pl.BlockSpec((1, nb, D_C, sb), lambda i: (j, i // tiles, 0, i % tiles))
    in_arrays = (x, mod0, mod1, g_pre0, g_post0, wg, wu, wd, g_pre1, w_k, w_qkv_t, w_f, w_f_t, b_f, b_f_t,
                 kt_stack, vt_stack)
    return pl.pallas_call(
        _inproj_c_prompt_kernel,
        grid=(grid,),
        in_specs=[spec(D_MODEL), mod_spec, mod_spec, vec, vec] + _ffn_weight_specs(li, 0)
        + [vec, _full(w_k.shape), _full(w_qkv_t.shape), _full(w_f.shape), _full(w_f_t.shape),
           _full(b_f.shape), _full(b_f_t.shape),
           pl.BlockSpec(memory_space=pl.ANY), pl.BlockSpec(memory_space=pl.ANY)],
        out_specs=[spec(D_MODEL), spec_t(D_C), spec(D_C), spec_t(D_C), stack_spec, stack_spec,
                   spec(LANES), spec_t(H_C)],
        out_shape=[jax.ShapeDtypeStruct(x.shape, F32),
                   jax.ShapeDtypeStruct((b, D_C, s), BF16),
                   jax.ShapeDtypeStruct((b, s, D_C), BF16),
                   jax.ShapeDtypeStruct((b, D_C, s), BF16),
                   jax.ShapeDtypeStruct(kt_stack.shape, F32),
                   jax.ShapeDtypeStruct(vt_stack.shape, F32),
                   jax.ShapeDtypeStruct((b, s, LANES), F32),
                   jax.ShapeDtypeStruct((b, H_C, s), F32)],
        input_output_aliases={len(in_arrays) - 2: 4, len(in_arrays) - 1: 5},
        compiler_params=_cparams(1, VMEM_LIMIT),
        name="ffn_inproj_c_prompt",
    )(*in_arrays)


def _outproj_prompt_kernel(*refs, n_in):
    x_ref, mod_ref, gpost_ref = refs[:3]
    a_refs = refs[3:3 + n_in]
    wt_refs = refs[3 + n_in:3 + 2 * n_in]
    o_ref = refs[-1]
    x = x_ref[...]
    out_t = None
    for a_ref, wt_ref in zip(a_refs, wt_refs):
        t = _dot(wt_ref[...], a_ref[0])
        out_t = t if out_t is None else out_t + t
    out = out_t.T.reshape(x.shape)
    o_ref[...] = _gated_residual(x, mod_ref[...], gpost_ref[...], out, 1.0)


def _outproj_prompt(x, mod, g_post, acts_t, weights_t):
    nb, sb, grid, spec, spec_t, mod_spec = _tok_layout(x, WIDE_TOK_TILE)
    n_in = len(acts_t)
    return pl.pallas_call(
        functools.partial(_outproj_prompt_kernel, n_in=n_in),
        grid=(grid,),
        in_specs=([spec(D_MODEL), mod_spec, _full((1, D_MODEL))]
                  + [spec_t(a.shape[1]) for a in acts_t]
                  + [_full(w.shape) for w in weights_t]),
        out_specs=spec(D_MODEL),
        out_shape=jax.ShapeDtypeStruct(x.shape, F32),
        compiler_params=_cparams(1, VMEM_LIMIT),
        name="outproj_prompt",
    )(x, mod, g_post, *acts_t, *weights_t)


def _diff_lambda(lam_ref, lam_init):
    lv = lam_ref[...]
    a = jnp.sum(lv[0:1] * lv[1:2], axis=1, keepdims=True)
    b = jnp.sum(lv[2:3] * lv[3:4], axis=1, keepdims=True)
    return jnp.exp(a) - jnp.exp(b) + lam_init


def _alibi_slope(g):
    slope = jnp.float32(0.0)
    for h in range(H_B):
        slope = jnp.where(g == h, jnp.float32(2.0 ** (-8.0 * (h + 1) / H_B)), slope)
    return slope


def _select_lane(a, idx):
    lane = lax.broadcasted_iota(jnp.int32, a.shape, 1)
    return jnp.sum(jnp.where(lane == idx, a, 0.0), axis=1, keepdims=True)


def _select_row(a, idx):
    row = lax.broadcasted_iota(jnp.int32, a.shape, 0)
    return jnp.sum(jnp.where(row == idx, a, 0.0), axis=0, keepdims=True)


def _relpos_bias_kernel(tab_ref, pbias_ref, sbias_ref):
    t = tab_ref[0]
    t_hi = t.astype(BF16)
    r1 = t - t_hi.astype(F32)
    t_mid = r1.astype(BF16)
    t_lo = (r1 - t_mid.astype(F32)).astype(BF16)
    ent = lax.broadcasted_iota(jnp.int32, (TAB_PAD, TOEP), 0)
    n = lax.broadcasted_iota(jnp.int32, (TAB_PAD, TOEP), 1)

    def lookup(rel_pos):
        onehot = jnp.where(ent == jnp.clip(rel_pos, -REL_CLIP, REL_CLIP) + REL_CLIP, 1.0, 0.0)
        onehot = onehot.astype(BF16)
        return _dot(t_hi, onehot) + _dot(t_mid, onehot) + _dot(t_lo, onehot)

    gen_t = lookup(A_PAST + jnp.where(n < TP, n, n - TOEP)) * LOG2E
    d = (lax.broadcasted_iota(jnp.int32, (BAND_A, TP), 0) // CHUNK
         - lax.broadcasted_iota(jnp.int32, (BAND_A, TP), 1) // CHUNK)
    valid = (d >= 0) & (d <= A_LEFT_CHUNKS)
    gen = lookup(A_PAST + TP - 1 - n)
    t_sample = sbias_ref.shape[2]
    for h in range(H_A):
        rows_t = jnp.broadcast_to(gen_t[h:h + 1, :], (BAND_A, TOEP))
        toep_t = pltpu.roll(rows_t, 0, 1, stride=1, stride_axis=0)
        pbias_ref[0, h] = jnp.where(valid, toep_t[:, :TP], NEG_INF)
        rows = jnp.broadcast_to(gen[h:h + 1, :], (t_sample, TOEP))
        toep = pltpu.roll(rows, TOEP - TP + 1, 1, stride=1, stride_axis=0)
        sbias_ref[0, h] = toep[:, :SBIAS_W]


def _relpos_bias(relpos_a, sample_len):
    ne = relpos_a.shape[0]
    tab = jnp.pad(relpos_a, ((0, 0), (0, 0), (0, TAB_PAD - relpos_a.shape[-1])))
    return pl.pallas_call(
        _relpos_bias_kernel,
        grid=(ne,),
        in_specs=[pl.BlockSpec((1, H_A, TAB_PAD), lambda l: (l, 0, 0))],
        out_specs=[pl.BlockSpec((1, H_A, BAND_A, TP), lambda l: (l, 0, 0, 0)),
                   pl.BlockSpec((1, H_A, sample_len, SBIAS_W), lambda l: (l, 0, 0, 0))],
        out_shape=[jax.ShapeDtypeStruct((ne, H_A, BAND_A, TP), F32),
                   jax.ShapeDtypeStruct((ne, H_A, sample_len, SBIAS_W), F32)],
        compiler_params=_cparams(1, VMEM_LIMIT),
        name="relpos_bias",
    )(tab)


def _stats_init(m_ref, acc_ref):
    m_ref[...] = jnp.full(m_ref.shape, NEG_INF, F32)
    acc_ref[...] = jnp.zeros(acc_ref.shape, F32)


def _with_ones(v_t):
    return jnp.concatenate([v_t, jnp.ones((SUM_ROWS, v_t.shape[1]), v_t.dtype)], axis=0)


def _stats_update_all(m_ref, acc_ref, scores, values):
    n = len(scores)
    m_prev = [m_ref[i] for i in range(n)]
    m_new = [jnp.maximum(m_prev[i], jnp.max(scores[i], axis=0, keepdims=True)) for i in range(n)]
    alpha = [jnp.exp2(m_prev[i] - m_new[i]) for i in range(n)]
    p = [jnp.exp2(scores[i] - m_new[i]).astype(BF16) for i in range(n)]
    pv = [_dot(values[i], p[i]) for i in range(n)]
    for i in range(n):
        acc_ref[i] = alpha[i] * acc_ref[i] + pv[i]
        m_ref[i] = m_new[i]


def _stats_result(acc_ref, mi):
    acc = acc_ref[mi]
    d = acc.shape[0] - SUM_ROWS
    return acc[:d] * (1.0 / acc[d:d + 1])


def _stats_scratch(n, d, tq):
    return [pltpu.VMEM((n, 1, tq), F32), pltpu.VMEM((n, d + SUM_ROWS, tq), F32)]


def _pad_extras(qe):
    return jnp.concatenate([qe, jnp.zeros((LANES - EXTRA_ROWS, qe.shape[1]), qe.dtype)], axis=0)


def _map_rows(q_t, mi):
    row = lax.broadcasted_iota(jnp.int32, q_t.shape, 0)
    keep = (row >= mi * HEAD_DIM) & (row < (mi + 1) * HEAD_DIM)
    return jnp.where(keep, q_t, jnp.zeros_like(q_t))


def _attn_a_prompt_kernel(qt_ref, k_ref, vt_ref, bias_ref, o_ref, m_ref, acc_ref, *, hp):
    qi = pl.program_id(2)
    qm = [_map_rows(qt_ref[0, p * LANES:(p + 1) * LANES, :], mi) for p in range(hp) for mi in range(2)]
    _stats_init(m_ref, acc_ref)
    for j in range(BAND_TILES):
        kt = qi - (BAND_TILES - 1) + j

        def tile(j=j, kt=kt):
            start = pl.multiple_of(kt * TP, TP)
            scores, values = [], []
            for p in range(hp):
                cols = slice(p * LANES, (p + 1) * LANES)
                k = k_ref[0, pl.ds(start, TP), cols]
                v_t = vt_ref[0, cols, pl.ds(start, TP)]
                for mi in range(2):
                    scores.append(_dot(k, qm[2 * p + mi]) + bias_ref[2 * p + mi, j * TP:(j + 1) * TP, :])
                    values.append(_with_ones(v_t[mi * HEAD_DIM:(mi + 1) * HEAD_DIM]))
            _stats_update_all(m_ref, acc_ref, scores, values)

        if j == BAND_TILES - 1:
            tile()
        else:
            pl.when(kt >= 0)(tile)
    o = jnp.concatenate([_stats_result(acc_ref, i) for i in range(2 * hp)], axis=0)
    o_ref[0] = o.astype(o_ref.dtype)


def _attn_a_prompt(q_t, k, v_t, pbias):
    b, s, _ = k.shape
    hp = HP_A
    groups = D_A // (hp * LANES)
    return pl.pallas_call(
        functools.partial(_attn_a_prompt_kernel, hp=hp),
        grid=(b, groups, s // TP),
        in_specs=[pl.BlockSpec((1, hp * LANES, TP), lambda b, g, i: (b, g, i)),
                  pl.BlockSpec((1, s, hp * LANES), lambda b, g, i: (b, 0, g)),
                  pl.BlockSpec((1, hp * LANES, s), lambda b, g, i: (b, g, 0)),
                  pl.BlockSpec((2 * hp, BAND_A, TP), lambda b, g, i: (g, 0, 0))],
        out_specs=pl.BlockSpec((1, hp * LANES, TP), lambda b, g, i: (b, g, i)),
        out_shape=jax.ShapeDtypeStruct((b, D_A, s), BF16),
        scratch_shapes=_stats_scratch(2 * hp, HEAD_DIM, TP),
        compiler_params=_cparams(3, VMEM_LIMIT),
        name="attn_a_prompt",
    )(q_t, k, v_t, pbias)


def _attn_b_prompt_kernel(qt_ref, k_ref, vt_ref, ke_ref, lam_ref, subln_ref, o_ref,
                          m_ref, acc_ref, diag_ref, *, lam_init, hp):
    g = pl.program_id(1)
    qi = pl.program_id(2)
    coefs = [_alibi_slope(g * hp + p) * LOG2E for p in range(hp)]

    @pl.when(qi == 0)
    def _():
        for d in range(DIAG_TILES):
            kr = lax.broadcasted_iota(jnp.int32, (TP, TQW), 0) + d * TP
            qc = lax.broadcasted_iota(jnp.int32, (TP, TQW), 1)
            dist = jnp.abs(qc - kr).astype(F32)
            for p in range(hp):
                diag_ref[p, d] = jnp.where(kr // CHUNK <= qc // CHUNK, -coefs[p] * dist, NEG_INF)

    row = lax.broadcasted_iota(jnp.int32, (EXTRA_ROWS, TQW), 0)
    qpos = (qi * TQW + lax.broadcasted_iota(jnp.int32, (EXTRA_ROWS, TQW), 1)).astype(F32)
    part = row % 3
    qa_past, qa_diag = [], []
    for p in range(hp):
        c = coefs[p]
        q_t = qt_ref[0, p * LANES:(p + 1) * LANES, :]
        hi, mid, lo = _split3(jnp.where(row < 3, CHUNK * c, jnp.where(row < 6, c, -c * qpos)))
        qe = jnp.where(row < 9, jnp.where(part == 0, hi, jnp.where(part == 1, mid, lo)), 0.0)
        qe = _pad_extras(qe.astype(BF16))
        for mi in range(2):
            qm = _map_rows(q_t, mi)
            qa_past.append(jnp.concatenate([qm, qe], axis=0))
            qa_diag.append(jnp.concatenate([qm, jnp.zeros_like(qe)], axis=0))
    _stats_init(m_ref, acc_ref)

    def tile(j, qa, diag):
        start = pl.multiple_of(j * TP, TP)
        ke = ke_ref[pl.ds(start, TP), :]
        scores, values = [], []
        for p in range(hp):
            cols = slice(p * LANES, (p + 1) * LANES)
            kx = jnp.concatenate([k_ref[0, pl.ds(start, TP), cols], ke], axis=1)
            v_t = _with_ones(vt_ref[0, cols, pl.ds(start, TP)])
            for mi in range(2):
                s_t = _dot(kx, qa[2 * p + mi])
                scores.append(s_t if diag is None else s_t + diag_ref[p, diag])
                values.append(v_t)
        _stats_update_all(m_ref, acc_ref, scores, values)

    def past_tile(j, carry):
        tile(j, qa_past, None)
        return carry

    lax.fori_loop(0, DIAG_TILES * qi, past_tile, 0)
    for d in range(DIAG_TILES):
        tile(DIAG_TILES * qi + d, qa_diag, d)
    lam = _diff_lambda(lam_ref, lam_init)
    outs = []
    for p in range(hp):
        o = _stats_result(acc_ref, 2 * p) - lam * _stats_result(acc_ref, 2 * p + 1)
        y = o * lax.rsqrt(jnp.mean(o * o, axis=0, keepdims=True) + EPS)
        outs.append((y * subln_ref[...]) * (1.0 - lam_init))
    o_ref[0] = jnp.concatenate(outs, axis=0).astype(o_ref.dtype)


def _attn_b_prompt(q_t, k, v_t, lam_vec, subln_col, lam_init):
    b, s, _ = k.shape
    hp = HP_B
    groups = D_B // (hp * LANES)
    blk0 = D_A // (hp * LANES)
    pos = jnp.arange(s, dtype=jnp.int32)[:, None]
    lane = jnp.arange(LANES, dtype=jnp.int32)[None, :]
    key_extras = jnp.where(lane < 3, pos // CHUNK,
                           jnp.where(lane < 6, pos % CHUNK, jnp.where(lane < 9, 1, 0))).astype(BF16)
    return pl.pallas_call(
        functools.partial(_attn_b_prompt_kernel, lam_init=lam_init, hp=hp),
        grid=(b, groups, s // TQW),
        in_specs=[pl.BlockSpec((1, hp * LANES, TQW), lambda b, g, i: (b, blk0 + g, i)),
                  pl.BlockSpec((1, s, hp * LANES), lambda b, g, i: (b, 0, blk0 + g)),
                  pl.BlockSpec((1, hp * LANES, s), lambda b, g, i: (b, blk0 + g, 0)),
                  _full(key_extras.shape), _full(lam_vec.shape), _full(subln_col.shape)],
        out_specs=pl.BlockSpec((1, hp * LANES, TQW), lambda b, g, i: (b, g, i)),
        out_shape=jax.ShapeDtypeStruct((b, D_B, s), BF16),
        scratch_shapes=(_stats_scratch(2 * hp, 2 * HEAD_DIM, TQW)
                        + [pltpu.VMEM((hp, DIAG_TILES, TP, TQW), F32)]),
        compiler_params=_cparams(3, VMEM_LIMIT),
        name="attn_b_prompt",
    )(q_t, k, v_t, key_extras, lam_vec, subln_col)


def _logf_scan_kernel(logf_ref, dqt_ref, e_ref):
    r = lax.broadcasted_iota(jnp.int32, (SCAN_T, SCAN_T), 0)
    c = lax.broadcasted_iota(jnp.int32, (SCAN_T, SCAN_T), 1)
    tri = jnp.where(r >= c, 1.0, 0.0).astype(BF16)
    pairs = H_C // 2
    i_sel = lax.broadcasted_iota(jnp.int32, (3 * LANES, pairs * LANES), 0)
    col = lax.broadcasted_iota(jnp.int32, (3 * LANES, pairs * LANES), 1)
    j_sel = col % LANES
    src = ((j_sel - 3) % 3) * LANES + 2 * (col // LANES) + (j_sel - 3) // 3
    sel = jnp.where((j_sel >= 3) & (j_sel < 9) & (i_sel == src), 1.0, 0.0).astype(BF16)
    ones_cols = jnp.where(lax.broadcasted_iota(jnp.int32, (1, pairs * LANES), 1) % LANES < 3, 1.0, 0.0)
    carry = jnp.zeros((1, LANES), F32)
    for t in range(logf_ref.shape[1] // SCAN_T):
        rows = slice(t * SCAN_T, (t + 1) * SCAN_T)
        hi, mid, lo = _split3(logf_ref[0, rows, :])
        cs = (_dot(tri, hi.astype(BF16)) + _dot(tri, mid.astype(BF16)) + _dot(tri, lo.astype(BF16))) + carry
        carry = cs[SCAN_T - 1:SCAN_T, :]
        x = cs * LOG2E
        dqt_ref[0, :, rows] = x.T[:H_C, :]
        hi, mid, lo = _split3(x)
        parts = jnp.concatenate([hi.astype(BF16), mid.astype(BF16), lo.astype(BF16)], axis=1)
        e_all = (_dot(parts, sel) + ones_cols).astype(BF16)
        for g in range(pairs):
            e_ref[0, g, rows, :] = e_all[:, g * LANES:(g + 1) * LANES]


def _logf_scan(logf):
    b, s, _ = logf.shape
    return pl.pallas_call(
        _logf_scan_kernel,
        grid=(b,),
        in_specs=[pl.BlockSpec((1, s, LANES), lambda b: (b, 0, 0))],
        out_specs=[pl.BlockSpec((1, H_C, s), lambda b: (b, 0, 0)),
                   pl.BlockSpec((1, H_C // 2, s, LANES), lambda b: (b, 0, 0, 0))],
        out_shape=[jax.ShapeDtypeStruct((b, H_C, s), F32),
                   jax.ShapeDtypeStruct((b, H_C // 2, s, LANES), BF16)],
        compiler_params=_cparams(1, VMEM_LIMIT),
        name="logf_scan",
    )(logf)


def _attn_c_prompt_kernel(qt_ref, k_ref, vt_ref, dqt_ref, e_ref, o_ref, m_ref, acc_ref, *, hp):
    qi = pl.program_id(2)
    row = lax.broadcasted_iota(jnp.int32, (EXTRA_ROWS, TQW), 0)
    qa = []
    for p in range(hp):
        q_t = qt_ref[0, p * LANES:(p + 1) * LANES, :]
        dq = dqt_ref[0, p]
        for mi in range(2):
            hi, mid, lo = _split3(dq[mi:mi + 1, :])
            dk_rows = (row >= 3 + 3 * mi) & (row < 6 + 3 * mi)
            qe = jnp.where(row == 0, hi, jnp.where(row == 1, mid, jnp.where(row == 2, lo,
                           jnp.where(dk_rows, -1.0, 0.0))))
            qa.append(jnp.concatenate([_map_rows(q_t, mi), _pad_extras(qe.astype(BF16))], axis=0))
    _stats_init(m_ref, acc_ref)

    def tile(j, diag):
        start = pl.multiple_of(j * TP, TP)
        scores, values = [], []
        for p in range(hp):
            cols = slice(p * LANES, (p + 1) * LANES)
            kx = jnp.concatenate([k_ref[0, pl.ds(start, TP), cols], e_ref[0, p, pl.ds(start, TP), :]],
                                 axis=1)
            v_t = vt_ref[0, cols, pl.ds(start, TP)]
            for mi in range(2):
                s_t = _dot(kx, qa[2 * p + mi])
                if diag is not None:
                    kr = lax.broadcasted_iota(jnp.int32, (TP, TQW), 0) + diag * TP
                    qc = lax.broadcasted_iota(jnp.int32, (TP, TQW), 1)
                    s_t = jnp.where(kr <= qc, s_t, NEG_INF)
                scores.append(s_t)
                values.append(_with_ones(v_t[mi * HEAD_DIM:(mi + 1) * HEAD_DIM]))
        _stats_update_all(m_ref, acc_ref, scores, values)

    def past_tile(j, carry):
        tile(j, None)
        return carry

    lax.fori_loop(0, DIAG_TILES * qi, past_tile, 0)
    for d in range(DIAG_TILES):
        tile(DIAG_TILES * qi + d, d)
    o = jnp.concatenate([_stats_result(acc_ref, i) for i in range(2 * hp)], axis=0)
    o_ref[0] = o.astype(o_ref.dtype)


def _attn_c_prompt(q_t, k, v_t, dq_t, key_extras):
    b, s, _ = k.shape
    hp = HP_C
    groups = D_C // (hp * LANES)
    dq_t = dq_t.reshape(b, D_C // LANES, 2, s)
    return pl.pallas_call(
        functools.partial(_attn_c_prompt_kernel, hp=hp),
        grid=(b, groups, s // TQW),
        in_specs=[pl.BlockSpec((1, hp * LANES, TQW), lambda b, g, i: (b, g, i)),
                  pl.BlockSpec((1, s, hp * LANES), lambda b, g, i: (b, 0, g)),
                  pl.BlockSpec((1, hp * LANES, s), lambda b, g, i: (b, g, 0)),
                  pl.BlockSpec((1, hp, 2, TQW), lambda b, g, i: (b, g, 0, i)),
                  pl.BlockSpec((1, hp, s, LANES), lambda b, g, i: (b, g, 0, 0))],
        out_specs=pl.BlockSpec((1, hp * LANES, TQW), lambda b, g, i: (b, g, i)),
        out_shape=jax.ShapeDtypeStruct((b, D_C, s), BF16),
        scratch_shapes=_stats_scratch(2 * hp, HEAD_DIM, TQW),
        compiler_params=_cparams(3, VMEM_LIMIT),
        name="attn_c_prompt",
    )(q_t, k, v_t, dq_t, key_extras)


def _split_maps(q):
    lane = lax.broadcasted_iota(jnp.int32, q.shape, q.ndim - 1)
    zero = jnp.zeros_like(q)
    return jnp.where(lane < HEAD_DIM, q, zero), jnp.where(lane >= HEAD_DIM, q, zero)


def _pair_heads(o0, o1):
    lane = lax.broadcasted_iota(jnp.int32, o0.shape, 1)
    return jnp.where(lane < HEAD_DIM, o0, o1)


def _diff_out(o0, o1, lam, subln, lam_init):
    o = o0 - lam * o1
    y = o * lax.rsqrt(jnp.mean(o * o, axis=-1, keepdims=True) + EPS)
    return (y * subln) * (1.0 - lam_init)


def _sample_scores(q, kn, kc, cache_t):
    q0, q1 = _split_maps(q)
    qs = jnp.concatenate([q0, q1], axis=0)
    s_c = _dot(qs, kc) if cache_t else _dot_nt(qs, kc)
    return s_c, _dot_nt(qs, kn)


def _sample_softmax_pv(s_c, s_n, vn, vc, cache_t):
    n = len(s_c)
    m = [jnp.maximum(jnp.max(s_c[i], axis=-1, keepdims=True), jnp.max(s_n[i], axis=-1, keepdims=True))
         for i in range(n)]
    p_c = [jnp.exp(s_c[i] - m[i]) for i in range(n)]
    p_n = [jnp.exp(s_n[i] - m[i]) for i in range(n)]
    l = [jnp.sum(p_c[i], axis=-1, keepdims=True) + jnp.sum(p_n[i], axis=-1, keepdims=True)
         for i in range(n)]
    outs = []
    for i in range(n):
        pc = p_c[i].astype(BF16)
        o_c = _dot_nt(pc, vc[i]) if cache_t else _dot(pc, vc[i])
        o = (o_c + _dot(p_n[i].astype(BF16), vn[i])) * (1.0 / l[i])
        t = o.shape[0] // 2
        outs.append((o[:t], o[t:]))
    return outs


def _cache_pair_t(ref, p):
    blk = ref[0, 0, 2 * p:2 * p + 2]
    return blk.reshape(2 * HEAD_DIM, blk.shape[-1]).astype(BF16)


def _attn_a_sample_kernel(q_ref, kn_ref, vn_ref, kc_ref, vc_ref, bias_ref, o_ref, *, hp):
    t = q_ref.shape[1]
    w = kc_ref.shape[-1]
    s_c, s_n, vn, vc = [], [], [], []
    for p in range(hp):
        cols = slice(p * LANES, (p + 1) * LANES)
        sc, sn = _sample_scores(q_ref[0, :, cols], kn_ref[0, :, cols], _cache_pair_t(kc_ref, p), True)
        bias = bias_ref[2 * p:2 * p + 2].reshape(2 * t, SBIAS_W)
        s_c.append(sc + bias[:, :w])
        s_n.append(sn + bias[:, w:w + t])
        vn.append(vn_ref[0, :, cols])
        vc.append(_cache_pair_t(vc_ref, p))
    outs = _sample_softmax_pv(s_c, s_n, vn, vc, True)
    for p in range(hp):
        o_ref[0, :, p * LANES:(p + 1) * LANES] = _pair_heads(*outs[p]).astype(o_ref.dtype)


def _attn_c_sample_kernel(q_ref, kn_ref, vn_ref, kc_ref, vc_ref, logf_ref, clogf_ref, o_ref,
                          dpast_ref, *, hp):
    g = pl.program_id(1)
    t = q_ref.shape[1]
    past = clogf_ref.shape[-1]

    @pl.when(g == 0)
    def _():
        cs = clogf_ref[0, 0]
        lane = lax.broadcasted_iota(jnp.int32, cs.shape, 1)
        shift = 1
        while shift < past:
            cs = cs + jnp.where(lane >= shift, pltpu.roll(cs, shift, 1), 0.0)
            shift *= 2
        dpast_ref[...] = cs - cs[:, past - 1:past]

    r = lax.broadcasted_iota(jnp.int32, (t, t), 0)
    c = lax.broadcasted_iota(jnp.int32, (t, t), 1)
    tri = jnp.where(r >= c, 1.0, 0.0).astype(F32)
    dq = _dot_exact(tri, logf_ref[0])
    er = lax.broadcasted_iota(jnp.int32, (LANES, LANES), 0)
    ec = lax.broadcasted_iota(jnp.int32, (LANES, LANES), 1)
    eye = jnp.where(er == ec, 1.0, 0.0).astype(F32)
    dq_t = _dot_nt_exact(eye, dq)
    dpast = dpast_ref[...]
    top = lax.broadcasted_iota(jnp.int32, (2 * t, 1), 0) < t
    qrow = lax.broadcasted_iota(jnp.int32, (2 * t, t), 0) % t
    kcol = lax.broadcasted_iota(jnp.int32, (2 * t, t), 1)
    s_c, s_n, vn, vc = [], [], [], []
    for p in range(hp):
        cols = slice(p * LANES, (p + 1) * LANES)
        h0 = 2 * (g * hp + p)
        dq_col = jnp.concatenate([_select_lane(dq, h0), _select_lane(dq, h0 + 1)], axis=0)
        dk_c = jnp.where(top, _select_row(dpast, h0), _select_row(dpast, h0 + 1))
        dk_n = jnp.where(top, _select_row(dq_t, h0), _select_row(dq_t, h0 + 1))
        sc, sn = _sample_scores(q_ref[0, :, cols], kn_ref[0, :, cols], _cache_pair_t(kc_ref, p), True)
        s_c.append(sc + dq_col - dk_c)
        s_n.append(jnp.where(kcol <= qrow, sn + dq_col - dk_n, NEG_INF))
        vn.append(vn_ref[0, :, cols])
        vc.append(_cache_pair_t(vc_ref, p))
    outs = _sample_softmax_pv(s_c, s_n, vn, vc, True)
    for p in range(hp):
        o_ref[0, :, p * LANES:(p + 1) * LANES] = _pair_heads(*outs[p]).astype(o_ref.dtype)


def _attn_pair_sample(kernel_fn, name, layer, qkv, cache_k_t, cache_v_t, extras, extra_specs,
                      scratch=()):
    b, t, _ = qkv.shape
    heads, _, past = cache_k_t.shape[2:]
    hp = HP_SAMPLE
    groups = heads // (2 * hp)
    width = hp * LANES
    cache_spec = pl.BlockSpec((1, 1, 2 * hp, HEAD_DIM, past), lambda b, g: (layer, b, g, 0, 0))
    return pl.pallas_call(
        functools.partial(kernel_fn, hp=hp),
        grid=(b, groups),
        in_specs=[pl.BlockSpec((1, t, width), lambda b, g: (b, 0, g)),
                  pl.BlockSpec((1, t, width), lambda b, g: (b, 0, groups + g)),
                  pl.BlockSpec((1, t, width), lambda b, g: (b, 0, 2 * groups + g)),
                  cache_spec, cache_spec] + list(extra_specs),
        out_specs=pl.BlockSpec((1, t, width), lambda b, g: (b, 0, g)),
        out_shape=jax.ShapeDtypeStruct((b, t, heads * HEAD_DIM), BF16),
        scratch_shapes=list(scratch),
        compiler_params=_cparams(2, VMEM_LIMIT),
        name=name,
    )(qkv, qkv, qkv, cache_k_t, cache_v_t, *extras)


def _attn_b_sample_kernel(q_ref, kn_ref, vn_ref, kc_ref, vc_ref, lam_ref, subln_ref, o_ref,
                          *, lam_init):
    t = q_ref.shape[1]
    past = kc_ref.shape[2] // H_B
    shape_c = (2 * t, past)
    shape_n = (2 * t, t)
    qpos_c = past + lax.broadcasted_iota(jnp.int32, shape_c, 0) % t
    dist_c = jnp.abs(qpos_c - lax.broadcasted_iota(jnp.int32, shape_c, 1)).astype(F32)
    dist_n = jnp.abs(lax.broadcasted_iota(jnp.int32, shape_n, 0) % t
                     - lax.broadcasted_iota(jnp.int32, shape_n, 1)).astype(F32)
    lam = _diff_lambda(lam_ref, lam_init)
    s_c, s_n, vn, vc = [], [], [], []
    for h in range(H_B):
        cols = slice(h * LANES, (h + 1) * LANES)
        slope = 2.0 ** (-8.0 * (h + 1) / H_B)
        kc = kc_ref[0, 0, pl.ds(h, past, stride=H_B), :].astype(BF16)
        sc, sn = _sample_scores(q_ref[0, :, cols], kn_ref[0, :, cols], kc, False)
        s_c.append(sc - slope * dist_c)
        s_n.append(sn - slope * dist_n)
        vn.append(vn_ref[0, :, cols])
        vc.append(vc_ref[0, 0, pl.ds(h, past, stride=H_B), :].astype(BF16))
    outs = _sample_softmax_pv(s_c, s_n, vn, vc, False)
    for h in range(H_B):
        o_ref[0, :, h * LANES:(h + 1) * LANES] = _diff_out(
            outs[h][0], outs[h][1], lam, subln_ref[...], lam_init).astype(o_ref.dtype)


def _attn_b_sample(layer, qkv, cache_k, cache_v, lam_vec, subln, lam_init):
    b, t, _ = qkv.shape
    rows = cache_k.shape[2]
    col0 = 3 * D_A // D_B
    cache_spec = pl.BlockSpec((1, 1, rows, LANES), lambda b: (layer, b, 0, 0))
    return pl.pallas_call(
        functools.partial(_attn_b_sample_kernel, lam_init=lam_init),
        grid=(b,),
        in_specs=[pl.BlockSpec((1, t, D_B), lambda b: (b, 0, col0)),
                  pl.BlockSpec((1, t, D_B), lambda b: (b, 0, col0 + 1)),
                  pl.BlockSpec((1, t, D_B), lambda b: (b, 0, col0 + 2)),
                  cache_spec, cache_spec, _full(lam_vec.shape), _full(subln.shape)],
        out_specs=pl.BlockSpec((1, t, D_B), lambda b: (b, 0, 0)),
        out_shape=jax.ShapeDtypeStruct((b, t, D_B), BF16),
        compiler_params=_cparams(1, VMEM_LIMIT),
        name="attn_b_sample",
    )(qkv, qkv, qkv, cache_k, cache_v, lam_vec, subln)


def _heads_last(a_t, heads):
    lead = a_t.shape[:-2]
    s = a_t.shape[-1]
    a = a_t.reshape(lead + (heads, HEAD_DIM, s))
    n = len(lead)
    return jnp.transpose(a, tuple(range(n)) + (n + 2, n, n + 1))


def kernel(x_prompt, x_sample, c_prompt, c_sample, cache_a_k, cache_a_v, cache_b_k, cache_b_v, cache_c_k, cache_c_v, cache_c_logf, w_ada, b_ada, norm_pre, norm_post, ffn_w_gate, ffn_w_up, ffn_w_down, w_in_ab, w_out_ab, relpos_a, lambda_b, subln_b, w_in_c, b_f, w_out_c):
    n_prompt, s_prompt = x_prompt.shape[:2]
    n_sample, t_sample = x_sample.shape[:2]

    wg = ffn_w_gate.astype(BF16)
    wu = ffn_w_up.astype(BF16)
    wd = ffn_w_down.astype(BF16)
    w_in_ab16 = w_in_ab.astype(BF16)
    w_in_ab16_t = jnp.swapaxes(w_in_ab, 1, 2).astype(BF16)
    w_out_ab16 = w_out_ab.astype(BF16)
    w_out_ab16_t = jnp.swapaxes(w_out_ab, 1, 2).astype(BF16)
    w_qkv_c16 = w_in_c[:, :, :3 * D_C].astype(BF16)
    w_qkv_c16_t = jnp.swapaxes(w_in_c[:, :, :3 * D_C], 1, 2).astype(BF16)
    w_f16 = jnp.pad(w_in_c[:, :, 3 * D_C:], ((0, 0), (0, 0), (0, LANES - H_C))).astype(BF16)
    w_f16_t = jnp.swapaxes(w_f16, 1, 2)
    b_f_row = jnp.pad(b_f, ((0, 0), (0, LANES - H_C)))[:, None, :]
    b_f_col = jnp.swapaxes(b_f_row, 1, 2)
    w_out_c16 = w_out_c.astype(BF16)
    w_out_c16_t = jnp.swapaxes(w_out_c, 1, 2).astype(BF16)
    cache_a_k_t = jnp.transpose(cache_a_k, (0, 1, 3, 4, 2))
    cache_a_v_t = jnp.transpose(cache_a_v, (0, 1, 3, 4, 2))
    cache_c_k_t = jnp.transpose(cache_c_k, (0, 1, 3, 4, 2))
    cache_c_v_t = jnp.transpose(cache_c_v, (0, 1, 3, 4, 2))
    clogf_t = jnp.swapaxes(cache_c_logf, -1, -2)
    past_b = cache_b_k.shape[2]
    cache_b_k_rows = cache_b_k.reshape(cache_b_k.shape[:2] + (past_b * H_B, 2 * HEAD_DIM))
    cache_b_v_rows = cache_b_v.reshape(cache_b_v.shape[:2] + (past_b * H_B, 2 * HEAD_DIM))

    rows = n_prompt + n_sample
    rows_pad = -(-rows // 8) * 8
    c_all = jnp.concatenate([c_prompt, c_sample, jnp.zeros((rows_pad - rows, D_MODEL), F32)], axis=0)
    mod_all = _ada(c_all, w_ada, b_ada).reshape(DEPTH, rows_pad, N_SUB, 3, D_MODEL)
    pbias, sbias = _relpos_bias(relpos_a, t_sample)

    def sublayer_params(li, row0, nrows):
        mod = [mod_all[li, row0:row0 + nrows, i] for i in range(N_SUB)]
        gpre = [norm_pre[li, i][None, :] for i in range(N_SUB)]
        gpost = [norm_post[li, i][None, :] for i in range(N_SUB)]
        return mod, gpre, gpost

    def run_prompt(x):
        a_k, a_v, c_lf = [], [], []
        c_k = jnp.zeros((DEPTH // 2, n_prompt, D_C, s_prompt), F32)
        c_v = jnp.zeros((DEPTH // 2, n_prompt, D_C, s_prompt), F32)
        b_k = jnp.zeros(((DEPTH + 1) // 2, n_prompt, s_prompt * H_B, 2 * HEAD_DIM), F32)
        b_v = jnp.zeros(((DEPTH + 1) // 2, n_prompt, s_prompt * H_B, 2 * HEAD_DIM), F32)
        for li in range(DEPTH):
            mod, gpre, gpost = sublayer_params(li, 0, n_prompt)
            j = li // 2
            if li % 2 == 0:
                lam_init = _lam_init(li)
                x, q_t, k, v_t, ka_t, va_t, b_k, b_v = _ffn_inproj_ab_prompt(
                    x, mod[0], mod[1], gpre[0], gpost[0], wg, wu, wd, li, gpre[1],
                    w_in_ab16[j], w_in_ab16_t[j], b_k, b_v, j)
                out_a = _attn_a_prompt(q_t, k, v_t, pbias[j])
                out_b = _attn_b_prompt(q_t, k, v_t, lambda_b[j], subln_b[j][:, None], lam_init)
                w = min(A_PAST, s_prompt)
                a_k.append(ka_t[:, :, s_prompt - w:])
                a_v.append(va_t[:, :, s_prompt - w:])
                outs, w_outs = [out_a, out_b], [w_out_ab16_t[j, :, :D_A], w_out_ab16_t[j, :, D_A:]]
            else:
                x, q_t, k, v_t, c_k, c_v, logf, logf_t = _ffn_inproj_c_prompt(
                    x, mod[0], mod[1], gpre[0], gpost[0], wg, wu, wd, li, gpre[1],
                    w_qkv_c16[j, :, D_C:2 * D_C], w_qkv_c16_t[j],
                    w_f16[j], w_f16_t[j], b_f_row[j], b_f_col[j], c_k, c_v, j)
                dq_t, key_extras = _logf_scan(logf)
                outs, w_outs = [_attn_c_prompt(q_t, k, v_t, dq_t, key_extras)], [w_out_c16_t[j]]
                c_lf.append(logf_t)
            x = _outproj_prompt(x, mod[1], gpost[1], outs, w_outs)
            x = _ffn(x, mod[2], gpre[2], gpost[2], wg, wu, wd, li, 1)
        nb, sb = x.shape[:2]
        return (x,
                _heads_last(jnp.stack(a_k), H_A), _heads_last(jnp.stack(a_v), H_A),
                b_k.reshape(-1, nb, sb, H_B, 2 * HEAD_DIM),
                b_v.reshape(-1, nb, sb, H_B, 2 * HEAD_DIM),
                _heads_last(c_k, H_C), _heads_last(c_v, H_C),
                jnp.swapaxes(jnp.stack(c_lf), -1, -2))

    def run_sample(x):
        a_k, a_v, b_k, b_v, c_k, c_v, c_lf = [], [], [], [], [], [], []
        for li in range(DEPTH):
            mod, gpre, gpost = sublayer_params(li, n_prompt, n_sample)
            x = _ffn(x, mod[0], gpre[0], gpost[0], wg, wu, wd, li, 0)
            j = li // 2
            if li % 2 == 0:
                lam_init = _lam_init(li)
                qkv, ka, va, kb, vb = _inproj_ab(x, mod[1], gpre[1], w_in_ab16[j])
                subln = subln_b[j][None, :]
                out_a = _attn_pair_sample(
                    _attn_a_sample_kernel, "attn_a_sample", j, qkv, cache_a_k_t, cache_a_v_t,
                    [sbias[j]],
                    [pl.BlockSpec((2 * HP_SAMPLE, t_sample, SBIAS_W), lambda b, g: (g, 0, 0))])
                out_b = _attn_b_sample(j, qkv, cache_b_k_rows, cache_b_v_rows, lambda_b[j], subln,
                                       lam_init)
                a_k.append(ka)
                a_v.append(va)
                b_k.append(kb)
                b_v.append(vb)
                x = _outproj(x, mod[1], gpost[1], [out_a, out_b],
                             [w_out_ab16[j, :D_A], w_out_ab16[j, D_A:]])
            else:
                qkv, k, v, logf = _inproj_c(x, mod[1], gpre[1], w_qkv_c16[j], w_f16[j], b_f_row[j])
                past = clogf_t.shape[-1]
                out = _attn_pair_sample(
                    _attn_c_sample_kernel, "attn_c_sample", j, qkv, cache_c_k_t, cache_c_v_t,
                    [logf, clogf_t],
                    [pl.BlockSpec((1, t_sample, LANES), lambda b, g: (b, 0, 0)),
                     pl.BlockSpec((1, 1, H_C, past), lambda b, g, j=j: (j, b, 0, 0))],
                    scratch=[pltpu.VMEM((H_C, past), F32)])
                c_k.append(k)
                c_v.append(v)
                c_lf.append(logf[:, :, :H_C])
                x = _outproj(x, mod[1], gpost[1], [out], [w_out_c16[j]])
            x = _ffn(x, mod[2], gpre[2], gpost[2], wg, wu, wd, li, 1)
        nb, sb = x.shape[:2]
        return (x,
                jnp.stack(a_k).reshape(-1, nb, sb, H_A, HEAD_DIM),
                jnp.stack(a_v).reshape(-1, nb, sb, H_A, HEAD_DIM),
                jnp.stack(b_k).reshape(-1, nb, sb, H_B, 2 * HEAD_DIM),
                jnp.stack(b_v).reshape(-1, nb, sb, H_B, 2 * HEAD_DIM),
                jnp.stack(c_k).reshape(-1, nb, sb, H_C, HEAD_DIM),
                jnp.stack(c_v).reshape(-1, nb, sb, H_C, HEAD_DIM),
                jnp.stack(c_lf))

    p = run_prompt(x_prompt)
    s = run_sample(x_sample)
    return (p[0], s[0]) + tuple(p[1:]) + tuple(s[1:])
```
